```python
import math
import jax
import jax.numpy as jnp
from jax import lax
import numpy as np

D_MODEL = 1024
BATCH = 16
SEQ = 4096
DEPTH = 1

NORM_EPS = 1e-6
GDN_HEADS = 4
GDN_HEAD_DIM = 128
GDN_WIDTH = GDN_HEADS * GDN_HEAD_DIM
GDN_CONV = 4
GDN_CHUNK = 64
NSA_HEADS = 8
NSA_KV_GROUPS = 2
NSA_GROUP_HEADS = NSA_HEADS // NSA_KV_GROUPS
NSA_HEAD_DIM = 64
NSA_WIDTH = NSA_HEADS * NSA_HEAD_DIM
NSA_KV_WIDTH = NSA_KV_GROUPS * NSA_HEAD_DIM
CMP_BLOCK = 32
CMP_STRIDE = 16
CMP_HIDDEN = 128
SEL_BLOCK = 64
SEL_COUNT = 16
WINDOW = 512
Q_BLOCK = 128
ROPE_THETA = 500000.0
ROPE_DIM = NSA_HEAD_DIM // 4
FORCED_SCORE = 1000.0
NEG_INF = -1e30
FFN_HIDDEN = -(-8 * D_MODEL // (3 * 256)) * 256
IN_SIZES = (3 * GDN_WIDTH, GDN_WIDTH, GDN_HEADS, GDN_HEADS,
            NSA_WIDTH, NSA_KV_WIDTH, NSA_KV_WIDTH, NSA_KV_WIDTH, NSA_KV_WIDTH, NSA_KV_WIDTH, NSA_KV_WIDTH,
            3 * NSA_HEADS, D_MODEL, D_MODEL)
IN_WIDTH = 4 * GDN_WIDTH + 2 * GDN_HEADS + NSA_WIDTH + 6 * NSA_KV_WIDTH + 3 * NSA_HEADS + 2 * D_MODEL

kernel_name = 'hybrid_gdn_nsa_block'


def rms_norm(x, gain):
    x32 = x.astype(jnp.float32)
    y = x32 * lax.rsqrt(jnp.mean(x32 * x32, axis=-1, keepdims=True) + NORM_EPS)
    return (y * gain.astype(jnp.float32)).astype(x.dtype)


def l2_normalize(x):
    return x * lax.rsqrt(jnp.sum(x * x, axis=-1, keepdims=True) + NORM_EPS)


def causal_depthwise_conv(x, w):
    k_len, ch = w.shape
    return lax.conv_general_dilated(x, w[:, None, :].astype(x.dtype), window_strides=(1,),
                                    padding=[(k_len - 1, 0)], dimension_numbers=('NWC', 'WIO', 'NWC'),
                                    feature_group_count=ch)


def partial_rope(x, pos):
    half = ROPE_DIM // 2
    inv_freq = ROPE_THETA ** (-jnp.arange(half, dtype=jnp.float32) / half)
    ang = pos.astype(jnp.float32)[:, None] * inv_freq
    bshape = (pos.shape[0],) + (1,) * (x.ndim - 3) + (half,)
    cos = jnp.cos(ang).reshape(bshape)
    sin = jnp.sin(ang).reshape(bshape)
    x1 = x[..., :half].astype(jnp.float32)
    x2 = x[..., half:ROPE_DIM].astype(jnp.float32)
    rot = jnp.concatenate([x1 * cos - x2 * sin, x2 * cos + x1 * sin], axis=-1).astype(x.dtype)
    return jnp.concatenate([rot, x[..., ROPE_DIM:]], axis=-1)


def gated_delta_rule_chunked(q, k, v, beta, g):
    b_, s_, h_, dk = q.shape
    dv = v.shape[-1]
    c = GDN_CHUNK
    n = s_ // c

    def to_chunks(a):
        return a.reshape(b_, n, c, h_, a.shape[-1]).transpose(0, 3, 1, 2, 4)

    q, k, v = to_chunks(q), to_chunks(k), to_chunks(v)
    beta = beta.reshape(b_, n, c, h_).transpose(0, 3, 1, 2)
    gc = jnp.cumsum(g.reshape(b_, n, c, h_).transpose(0, 3, 1, 2), axis=-1)
    causal = jnp.tril(jnp.ones((c, c), dtype=bool))
    strict = jnp.tril(jnp.ones((c, c), dtype=bool), -1)
    decay = jnp.exp(jnp.where(causal, gc[..., :, None] - gc[..., None, :], -jnp.inf))
    kb = k * beta[..., None]
    a_low = jnp.where(strict, jnp.einsum('bhnid,bhnjd->bhnij', kb, k) * decay, 0.0)
    ia = a_low + jnp.eye(c, dtype=q.dtype)
    u = lax.linalg.triangular_solve(ia, v * beta[..., None], left_side=True, lower=True, unit_diagonal=True)
    w = lax.linalg.triangular_solve(ia, kb * jnp.exp(gc)[..., None], left_side=True, lower=True, unit_diagonal=True)
    qk = jnp.einsum('bhnid,bhnjd->bhnij', q, k) * decay
    qg = q * jnp.exp(gc)[..., None]
    kd = k * jnp.exp(gc[..., -1:] - gc)[..., None]
    g_last = jnp.exp(gc[..., -1])

    def step(state, inp):
        qk_n, qg_n, kd_n, u_n, w_n, gl_n = inp
        v_new = u_n - jnp.einsum('bhcd,bhde->bhce', w_n, state)
        o_n = jnp.einsum('bhcd,bhde->bhce', qg_n, state) + jnp.einsum('bhij,bhje->bhie', qk_n, v_new)
        state = state * gl_n[..., None, None] + jnp.einsum('bhcd,bhce->bhde', kd_n, v_new)
        return state, o_n

    xs = tuple(jnp.moveaxis(a, 2, 0) for a in (qk, qg, kd, u, w, g_last))
    state0 = jnp.zeros((b_, h_, dk, dv), dtype=q.dtype)
    _, o = lax.scan(step, state0, xs)
    return o.transpose(1, 0, 3, 2, 4).reshape(b_, s_, h_, dv)


def compress_blocks(kv, pos_emb, w1, w2):
    s_ = kv.shape[1]
    n_cmp = (s_ - CMP_BLOCK) // CMP_STRIDE + 1
    idx = jnp.arange(n_cmp)[:, None] * CMP_STRIDE + jnp.arange(CMP_BLOCK)[None, :]
    blocks = kv[:, idx] + pos_emb[:, None, :]
    hid = jax.nn.silu(jnp.einsum('bnlgd,ldh->bgnh', blocks, w1))
    return jnp.einsum('bgnh,hd->bgnd', hid, w2)


def native_sparse_attention(q, k_cmp, v_cmp, k_slc, v_slc, k_win, v_win, gates,
                            cmp_pos_k, cmp_w1_k, cmp_w2_k, cmp_pos_v, cmp_w1_v, cmp_w2_v):
    b_, s_, g_, r_, dh = q.shape
    n_cmp = (s_ - CMP_BLOCK) // CMP_STRIDE + 1
    n_blk = s_ // SEL_BLOCK
    n_top = min(SEL_COUNT, n_blk)
    kc = compress_blocks(k_cmp, cmp_pos_k, cmp_w1_k, cmp_w2_k)
    vc = compress_blocks(v_cmp, cmp_pos_v, cmp_w1_v, cmp_w2_v)
    cmp_start = jnp.arange(n_cmp) * CMP_STRIDE
    cmp_end = cmp_start + CMP_BLOCK - 1
    blk_ids = jnp.arange(n_blk)
    sel_start = blk_ids * SEL_BLOCK
    overlap = ((cmp_start[:, None] <= sel_start[None, :] + SEL_BLOCK - 1)
               & (cmp_end[:, None] >= sel_start[None, :])).astype(jnp.float32)

    def to_blocks(a):
        return a.reshape(b_, n_blk, SEL_BLOCK, g_, dh).transpose(0, 3, 1, 2, 4).reshape(b_ * g_ * n_blk, SEL_BLOCK, dh)

    ks_flat = to_blocks(k_slc)
    vs_flat = to_blocks(v_slc)
    base = ((jnp.arange(b_)[:, None] * g_ + jnp.arange(g_)[None, :]) * n_blk)[:, :, None, None]
    pad = ((0, 0), (WINDOW, 0), (0, 0), (0, 0))
    kw_pad = jnp.pad(k_win, pad)
    vw_pad = jnp.pad(v_win, pad)
    span = WINDOW + Q_BLOCK
    q_off = jnp.arange(Q_BLOCK)
    sel_off = jnp.arange(SEL_BLOCK)
    win_off = jnp.arange(span) - WINDOW

    def query_block(qi):
        s0 = qi * Q_BLOCK
        t = s0 + q_off
        qb = lax.dynamic_slice_in_dim(q, s0, Q_BLOCK, axis=1)
        gb = lax.dynamic_slice_in_dim(gates, s0, Q_BLOCK, axis=1)
        vis = cmp_end[None, :] <= t[:, None]
        sc = jnp.einsum('bqgrd,bgnd->bgrqn', qb, kc).astype(jnp.float32)
        p_cmp = jnp.where(vis, jax.nn.softmax(jnp.where(vis, sc, NEG_INF), axis=-1), 0.0)
        o_cmp = jnp.einsum('bgrqn,bgnd->bqgrd', p_cmp.astype(vc.dtype), vc)
        imp = jnp.einsum('bgrqn,nj->bgqj', p_cmp, overlap)
        cur = (t // SEL_BLOCK)[:, None]
        valid = blk_ids[None, :] <= cur
        forced = (blk_ids[None, :] == 0) | (blk_ids[None, :] == cur) | (blk_ids[None, :] == cur - 1)
        score = jnp.where(valid, jnp.where(forced, FORCED_SCORE, imp), -1.0)
        _, top = lax.top_k(score, n_top)
        flat = base + top
        s_list = []
        for s in range(n_top):
            kg = ks_flat[flat[..., s]]
            kpos = top[..., s, None] * SEL_BLOCK + sel_off
            sc_s = jnp.einsum('bqgrd,bgqkd->bgrqk', qb, kg).astype(jnp.float32)
            s_list.append(jnp.where((kpos <= t[:, None])[:, :, None], sc_s, NEG_INF))
        p_sel = jax.nn.softmax(jnp.concatenate(s_list, axis=-1), axis=-1).astype(vs_flat.dtype)
        o_sel = sum(jnp.einsum('bgrqk,bgqkd->bqgrd', p_sel[..., s * SEL_BLOCK:(s + 1) * SEL_BLOCK],
                               vs_flat[flat[..., s]]) for s in range(n_top))
        kw = lax.dynamic_slice_in_dim(kw_pad, s0, span, axis=1)
        vw = lax.dynamic_slice_in_dim(vw_pad, s0, span, axis=1)
        kpos_w = s0 + win_off
        rel = t[:, None] - kpos_w[None, :]
        ok = (rel >= 0) & (rel < WINDOW) & (kpos_w[None, :] >= 0)
        sc_w = jnp.einsum('bqgrd,bkgd->bgrqk', qb, kw).astype(jnp.float32)
        p_w = jax.nn.softmax(jnp.where(ok, sc_w, NEG_INF), axis=-1)
        o_win = jnp.einsum('bgrqk,bkgd->bqgrd', p_w.astype(vw.dtype), vw)
        return gb[..., 0:1] * o_cmp + gb[..., 1:2] * o_sel + gb[..., 2:3] * o_win

    out = lax.map(query_block, jnp.arange(s_ // Q_BLOCK))
    return out.transpose(1, 0, 2, 3, 4, 5).reshape(b_, s_, g_ * r_ * dh)


def hybrid_layer(x, pos, mix_norm_gain, w_in, gdn_conv_w, gdn_a_log, gdn_dt_bias, gdn_out_norm_gain,
                 cmp_pos_k, cmp_w1_k, cmp_w2_k, cmp_pos_v, cmp_w1_v, cmp_w2_v,
                 w_branch_gdn, w_branch_nsa, w_out, ffn_norm_gain, w_gate_up, w_down):
    f32 = jnp.float32
    b_, s_, _ = x.shape
    h = rms_norm(x, mix_norm_gain)
    proj = jnp.einsum('bsd,de->bse', h, w_in)
    split_at = np.cumsum(IN_SIZES)[:-1].tolist()
    (qkv_a, z_a, beta_a, alpha_a, q_b, kc_b, vc_b, ks_b, vs_b, kw_b, vw_b,
     gate_b, merge_a, merge_b) = jnp.split(proj, split_at, axis=-1)

    qkv_a = jax.nn.silu(causal_depthwise_conv(qkv_a, gdn_conv_w))
    q_a, k_a, v_a = [a.reshape(b_, s_, GDN_HEADS, GDN_HEAD_DIM).astype(f32) for a in jnp.split(qkv_a, 3, axis=-1)]
    q_a = l2_normalize(q_a) * (GDN_HEAD_DIM ** -0.5)
    k_a = l2_normalize(k_a)
    beta = jax.nn.sigmoid(beta_a.astype(f32))
    log_decay = -jnp.exp(gdn_a_log.astype(f32)) * jax.nn.softplus(alpha_a.astype(f32) + gdn_dt_bias.astype(f32))
    o_a = gated_delta_rule_chunked(q_a, k_a, v_a, beta, log_decay)
    o_a = (o_a * lax.rsqrt(jnp.mean(o_a * o_a, axis=-1, keepdims=True) + NORM_EPS)
           * gdn_out_norm_gain.astype(f32)
           * jax.nn.silu(z_a.astype(f32).reshape(b_, s_, GDN_HEADS, GDN_HEAD_DIM)))
    o_a = o_a.reshape(b_, s_, GDN_WIDTH).astype(x.dtype)

    q_n = partial_rope(q_b.reshape(b_, s_, NSA_KV_GROUPS, NSA_GROUP_HEADS, NSA_HEAD_DIM), pos) * (NSA_HEAD_DIM ** -0.5)
    kv_shape = (b_, s_, NSA_KV_GROUPS, NSA_HEAD_DIM)
    k_cmp = partial_rope(kc_b.reshape(kv_shape), pos)
    k_slc = partial_rope(ks_b.reshape(kv_shape), pos)
    k_win = partial_rope(kw_b.reshape(kv_shape), pos)
    gates = jax.nn.sigmoid(gate_b.reshape(b_, s_, NSA_KV_GROUPS, NSA_GROUP_HEADS, 3))
    o_b = native_sparse_attention(q_n, k_cmp, vc_b.reshape(kv_shape), k_slc, vs_b.reshape(kv_shape),
                                  k_win, vw_b.reshape(kv_shape), gates,
                                  cmp_pos_k, cmp_w1_k, cmp_w2_k, cmp_pos_v, cmp_w1_v, cmp_w2_v)

    merged = (jax.nn.sigmoid(merge_a) * jnp.einsum('bse,ed->bsd', o_a, w_branch_gdn)
              + jax.nn.sigmoid(merge_b) * jnp.einsum('bse,ed->bsd', o_b, w_branch_nsa))
    x = x + jnp.einsum('bsd,de->bse', merged, w_out)

    h2 = rms_norm(x, ffn_norm_gain)
    gate, up = jnp.split(jnp.einsum('bsd,df->bsf', h2, w_gate_up), 2, axis=-1)
    return x + jnp.einsum('bsf,fd->bsd', jax.nn.silu(gate) * up, w_down)


def setup_inputs(seed: int = 0) -> dict:
    key = jax.random.key(seed)
    ks = jax.random.split(key, 24)
    f32 = jnp.float32
    L = DEPTH

    def nrm(k, shape, scale):
        return jax.random.normal(k, shape, f32) * scale

    def gain(k, shape):
        return 1.0 + 0.02 * jax.random.normal(k, shape, f32)

    dt = jnp.exp(jax.random.uniform(ks[5], (L, GDN_HEADS), f32, math.log(1e-3), math.log(1e-1)))
    return {
        'x': nrm(ks[0], (BATCH, SEQ, D_MODEL), 1.0),
        'mix_norm_gain': gain(ks[1], (L, D_MODEL)),
        'w_in': nrm(ks[2], (L, D_MODEL, IN_WIDTH), D_MODEL ** -0.5),
        'gdn_conv_w': nrm(ks[3], (L, GDN_CONV, 3 * GDN_WIDTH), GDN_CONV ** -0.5),
        'gdn_a_log': jnp.log(jax.random.uniform(ks[4], (L, GDN_HEADS), f32, 1.0, 16.0)),
        'gdn_dt_bias': dt + jnp.log(-jnp.expm1(-dt)),
        'gdn_out_norm_gain': gain(ks[6], (L, GDN_HEAD_DIM)),
        'cmp_pos_k': nrm(ks[7], (L, CMP_BLOCK, NSA_HEAD_DIM), 0.02),
        'cmp_w1_k': nrm(ks[8], (L, CMP_BLOCK, NSA_HEAD_DIM, CMP_HIDDEN), (CMP_BLOCK * NSA_HEAD_DIM) ** -0.5),
        'cmp_w2_k': nrm(ks[9], (L, CMP_HIDDEN, NSA_HEAD_DIM), CMP_HIDDEN ** -0.5),
        'cmp_pos_v': nrm(ks[10], (L, CMP_BLOCK, NSA_HEAD_DIM), 0.02),
        'cmp_w1_v': nrm(ks[11], (L, CMP_BLOCK, NSA_HEAD_DIM, CMP_HIDDEN), (CMP_BLOCK * NSA_HEAD_DIM) ** -0.5),
        'cmp_w2_v': nrm(ks[12], (L, CMP_HIDDEN, NSA_HEAD_DIM), CMP_HIDDEN ** -0.5),
        'w_branch_gdn': nrm(ks[13], (L, GDN_WIDTH, D_MODEL), GDN_WIDTH ** -0.5),
        'w_branch_nsa': nrm(ks[14], (L, NSA_WIDTH, D_MODEL), NSA_WIDTH ** -0.5),
        'w_out': nrm(ks[15], (L, D_MODEL, D_MODEL), D_MODEL ** -0.5),
        'ffn_norm_gain': gain(ks[16], (L, D_MODEL)),
        'w_gate_up': nrm(ks[17], (L, D_MODEL, 2 * FFN_HIDDEN), D_MODEL ** -0.5),
        'w_down': nrm(ks[18], (L, FFN_HIDDEN, D_MODEL), FFN_HIDDEN ** -0.5),
        'final_norm_gain': gain(ks[19], (D_MODEL,)),
    }


def reference(x, mix_norm_gain, w_in, gdn_conv_w, gdn_a_log, gdn_dt_bias, gdn_out_norm_gain,
              cmp_pos_k, cmp_w1_k, cmp_w2_k, cmp_pos_v, cmp_w1_v, cmp_w2_v,
              w_branch_gdn, w_branch_nsa, w_out, ffn_norm_gain, w_gate_up, w_down, final_norm_gain):
    pos = jnp.arange(x.shape[1])
    for layer in range(DEPTH):
        x = hybrid_layer(x, pos, mix_norm_gain[layer], w_in[layer], gdn_conv_w[layer], gdn_a_log[layer],
                         gdn_dt_bias[layer], gdn_out_norm_gain[layer],
                         cmp_pos_k[layer], cmp_w1_k[layer], cmp_w2_k[layer],
                         cmp_pos_v[layer], cmp_w1_v[layer], cmp_w2_v[layer],
                         w_branch_gdn[layer], w_branch_nsa[layer], w_out[layer],
                         ffn_norm_gain[layer], w_gate_up[layer], w_down[layer])
    return rms_norm(x, final_norm_gain)
```

```python
import functools

import numpy as np
import jax
import jax.numpy as jnp
from jax import lax
from jax.experimental import pallas as pl
from jax.experimental.pallas import tpu as pltpu

F32 = jnp.float32
BF16 = jnp.bfloat16

D_MODEL = 1024
NORM_EPS = 1e-6
GDN_HEADS = 4
GDN_HEAD_DIM = 128
GDN_WIDTH = GDN_HEADS * GDN_HEAD_DIM
GDN_CONV = 4
GDN_CHUNK = 64
NSA_HEADS = 8
NSA_KV_GROUPS = 2
NSA_GROUP_HEADS = NSA_HEADS // NSA_KV_GROUPS
NSA_HEAD_DIM = 64
NSA_WIDTH = NSA_HEADS * NSA_HEAD_DIM
NSA_KV_WIDTH = NSA_KV_GROUPS * NSA_HEAD_DIM
CMP_BLOCK = 32
CMP_STRIDE = 16
CMP_HIDDEN = 128
SEL_BLOCK = 64
SEL_COUNT = 16
WINDOW = 512
ROPE_THETA = 500000.0
ROPE_DIM = NSA_HEAD_DIM // 4
FORCED_SCORE = 1000.0
NEG_INF = -1e30
FFN_HIDDEN = 2816

LANES = 128
VMEM_LIMIT = 56 * 1024 * 1024

C_QKV = (0, 1536)
C_Z = (1536, 2048)
C_Q = (2048, 2560)
C_KC = (2560, 2688)
C_VC = (2688, 2816)
C_KVS = (2816, 3328)
C_MG = (3328, 5376)
C_SMALL = (5376, 5504)
IN_WIDTH_PADDED = 5504
SMALL_BETA = 0
SMALL_ALPHA = 4
SMALL_GATE = 8

TQ = 128
KB = 128


def _dot(a, b):
    return jnp.dot(a, b, preferred_element_type=F32)


def _dot_nt(a, b):
    return lax.dot_general(a, b, (((1,), (1,)), ((), ())), preferred_element_type=F32)


def _dot_tn(a, b):
    return lax.dot_general(a, b, (((0,), (0,)), ((), ())), preferred_element_type=F32)


def _sigmoid(x):
    return 1.0 / (1.0 + jnp.exp(-x))


def _silu(x):
    return x * _sigmoid(x)


def _rms(x, gain):
    return x * lax.rsqrt(jnp.mean(x * x, axis=-1, keepdims=True) + NORM_EPS) * gain


def _rope_slab(y, c, s1, s2):
    return y * c + pltpu.roll(y, 8, 1) * s1 + pltpu.roll(y, LANES - 8, 1) * s2


def _inproj_kernel(x_ref, gain_ref, w_ref, rc_ref, rs1_ref, rs2_ref,
                   qkv_ref, z_ref, q_ref, kc_ref, vc_ref, kvs_ref, mg_ref, small_ref):
    hb = _rms(x_ref[...], gain_ref[...]).astype(BF16)

    def proj(cols):
        return _dot(hb, w_ref[:, cols[0]:cols[1]])

    qkv_ref[...] = proj(C_QKV).astype(BF16)
    z_ref[...] = proj(C_Z).astype(BF16)
    mg_ref[...] = proj(C_MG).astype(BF16)
    small_ref[...] = proj(C_SMALL)
    vc_ref[...] = proj(C_VC).astype(BF16)

    c, s1, s2 = rc_ref[...], rs1_ref[...], rs2_ref[...]
    q = proj(C_Q)
    for r in range(NSA_GROUP_HEADS):
        sl = slice(r * LANES, (r + 1) * LANES)
        q_ref[:, sl] = (_rope_slab(q[:, sl], c, s1, s2) * (NSA_HEAD_DIM ** -0.5)).astype(BF16)
    kc_ref[...] = _rope_slab(proj(C_KC), c, s1, s2).astype(BF16)
    kvs = proj(C_KVS)
    kvs_ref[:, 0:128] = _rope_slab(kvs[:, 0:128], c, s1, s2).astype(BF16)
    kvs_ref[:, 128:256] = kvs[:, 128:256].astype(BF16)
    kvs_ref[:, 256:384] = _rope_slab(kvs[:, 256:384], c, s1, s2).astype(BF16)
    kvs_ref[:, 384:512] = kvs[:, 384:512].astype(BF16)


def _inproj_call(x2d, gain, w_r, rc, rs1, rs2, seq, tm=256):
    t = x2d.shape[0]
    n_seq_tiles = seq // tm
    row = lambda i: (i, 0)
    const = lambda i: (0, 0)
    tab = lambda i: (i % n_seq_tiles, 0)
    widths = [(1536, BF16), (512, BF16), (512, BF16), (128, BF16), (128, BF16), (512, BF16), (2048, BF16),
              (128, F32)]
    return pl.pallas_call(
        _inproj_kernel,
        grid=(t // tm,),
        in_specs=[pl.BlockSpec((tm, D_MODEL), row),
                  pl.BlockSpec((1, D_MODEL), const),
                  pl.BlockSpec((D_MODEL, IN_WIDTH_PADDED), const, pipeline_mode=pl.Buffered(1)),
                  pl.BlockSpec((tm, LANES), tab),
                  pl.BlockSpec((tm, LANES), tab),
                  pl.BlockSpec((tm, LANES), tab)],
        out_specs=[pl.BlockSpec((tm, w), row) for w, _ in widths],
        out_shape=[jax.ShapeDtypeStruct((t, w), d) for w, d in widths],
        compiler_params=pltpu.CompilerParams(dimension_semantics=("parallel",),
                                             vmem_limit_bytes=VMEM_LIMIT),
        name="inproj",
    )(x2d, gain, w_r, rc, rs1, rs2)


def _unit_lower_inverse(a_low):
    c = a_low.shape[0]
    eye = (lax.broadcasted_iota(jnp.int32, (c, c), 0) == lax.broadcasted_iota(jnp.int32, (c, c), 1)).astype(F32)
    t = eye - a_low
    p = a_low
    n = 2
    while n < c:
        pb = p.astype(BF16)
        p = _dot(pb, pb)
        t = t + _dot(t.astype(BF16), p.astype(BF16))
        n *= 2
    return t


def _gdn_kernel(qkv_ref, z_ref, small_ref, convw_ref, alog_ref, dtb_ref, ogain_ref, o_ref,
                xbuf, cbuf, state):
    cb = qkv_ref.shape[0]
    c = GDN_CHUNK
    j = pl.program_id(1)

    @pl.when(j == 0)
    def _():
        xbuf[0:8, :] = jnp.zeros((8, 3 * GDN_WIDTH), F32)
        state[...] = jnp.zeros_like(state)

    @pl.when(j > 0)
    def _():
        xbuf[0:8, :] = xbuf[cb:cb + 8, :]

    xbuf[8:8 + cb, :] = qkv_ref[...].astype(F32)
    w = convw_ref[...]
    acc = xbuf[8:8 + cb, :] * w[GDN_CONV - 1:GDN_CONV, :]
    for jj in range(GDN_CONV - 1):
        off = 8 - (GDN_CONV - 1) + jj
        acc = acc + xbuf[off:off + cb, :] * w[jj:jj + 1, :]
    cbuf[...] = _silu(acc)

    row = lax.broadcasted_iota(jnp.int32, (c, c), 0)
    col = lax.broadcasted_iota(jnp.int32, (c, c), 1)
    causal = row >= col
    strict = row > col
    tril = causal.astype(F32)
    neg_decay_rate = -jnp.exp(alog_ref[...])
    dtb = dtb_ref[...]
    ogain = ogain_ref[...]

    def chunk_body(ci, carry):
        r0 = pl.multiple_of(ci * c, c)
        sm = small_ref[pl.ds(r0, c), :]
        beta_all = _sigmoid(sm)
        xg = sm + dtb
        softplus = jnp.maximum(xg, 0.0) + jnp.log(1.0 + jnp.exp(-jnp.abs(xg)))
        g_all = neg_decay_rate * softplus
        gc_all = jnp.dot(tril, g_all, precision=lax.Precision.HIGHEST, preferred_element_type=F32)
        gc_t = gc_all.T
        for h in range(GDN_HEADS):
            hs = slice(h * GDN_HEAD_DIM, (h + 1) * GDN_HEAD_DIM)
            q = cbuf[pl.ds(r0, c), h * GDN_HEAD_DIM:(h + 1) * GDN_HEAD_DIM]
            k = cbuf[pl.ds(r0, c), GDN_WIDTH + h * GDN_HEAD_DIM:GDN_WIDTH + (h + 1) * GDN_HEAD_DIM]
            v = cbuf[pl.ds(r0, c), 2 * GDN_WIDTH + h * GDN_HEAD_DIM:2 * GDN_WIDTH + (h + 1) * GDN_HEAD_DIM]
            q = q * lax.rsqrt(jnp.sum(q * q, axis=-1, keepdims=True) + NORM_EPS) * (GDN_HEAD_DIM ** -0.5)
            k = k * lax.rsqrt(jnp.sum(k * k, axis=-1, keepdims=True) + NORM_EPS)
            beta = beta_all[:, SMALL_BETA + h:SMALL_BETA + h + 1]
            gc = gc_all[:, SMALL_ALPHA + h:SMALL_ALPHA + h + 1]
            gr = gc_t[SMALL_ALPHA + h:SMALL_ALPHA + h + 1, :]
            decay = jnp.where(causal, jnp.exp(jnp.where(causal, gc - gr, 0.0)), 0.0)
            eg = jnp.exp(gc)
            g_last = gc[c - 1:c, :]
            kb = k * beta
            kbb = kb.astype(BF16)
            kbf = k.astype(BF16)
            a_low = jnp.where(strict, _dot_nt(kbb, kbf) * decay, 0.0)
            tinv = _unit_lower_inverse(a_low).astype(BF16)
            u = _dot(tinv, (v * beta).astype(BF16))
            wm = _dot(tinv, (kb * eg).astype(BF16))
            qb = q.astype(BF16)
            qk = _dot_nt(qb, kbf) * decay
            qg = q * eg
            kd = k * jnp.exp(g_last - gc)
            s_old = state[h]
            s_b = s_old.astype(BF16)
            v_new = u - _dot(wm.astype(BF16), s_b)
            vnb = v_new.astype(BF16)
            o = _dot(qg.astype(BF16), s_b) + _dot(qk.astype(BF16), vnb)
            state[h] = s_old * jnp.exp(g_last) + _dot_tn(kd.astype(BF16), vnb)
            zh = z_ref[pl.ds(r0, c), h * GDN_HEAD_DIM:(h + 1) * GDN_HEAD_DIM].astype(F32)
            o = o * lax.rsqrt(jnp.mean(o * o, axis=-1, keepdims=True) + NORM_EPS) * ogain * _silu(zh)
            o_ref[pl.ds(r0, c), h * GDN_HEAD_DIM:(h + 1) * GDN_HEAD_DIM] = o.astype(BF16)
        return carry

    lax.fori_loop(0, cb // c, chunk_body, 0)


def _gdn_call(qkv, z, small, conv_w, alog_v, dtb_v, ogain, batch, seq, cb=256):
    nj = seq // cb
    row = lambda b, j: (b * nj + j, 0)
    const = lambda b, j: (0, 0)
    return pl.pallas_call(
        _gdn_kernel,
        grid=(batch, nj),
        in_specs=[pl.BlockSpec((cb, 3 * GDN_WIDTH), row),
                  pl.BlockSpec((cb, GDN_WIDTH), row),
                  pl.BlockSpec((cb, LANES), row),
                  pl.BlockSpec((GDN_CONV, 3 * GDN_WIDTH), const),
                  pl.BlockSpec((1, LANES), const),
                  pl.BlockSpec((1, LANES), const),
                  pl.BlockSpec((1, GDN_HEAD_DIM), const)],
        out_specs=pl.BlockSpec((cb, GDN_WIDTH), row),
        out_shape=jax.ShapeDtypeStruct((batch * seq, GDN_WIDTH), BF16),
        scratch_shapes=[pltpu.VMEM((cb + 8, 3 * GDN_WIDTH), F32),
                        pltpu.VMEM((cb, 3 * GDN_WIDTH), F32),
                        pltpu.VMEM((GDN_HEADS, GDN_HEAD_DIM, GDN_HEAD_DIM), F32)],
        compiler_params=pltpu.CompilerParams(dimension_semantics=("parallel", "arbitrary"),
                                             vmem_limit_bytes=VMEM_LIMIT),
        name="gdn",
    )(qkv, z, small, conv_w, alog_v, dtb_v, ogain)


def _cmp_kernel(kc_ref, vc_ref, posk_ref, w1k_ref, w2k_ref, posv_ref, w1v_ref, w2v_ref, kco_ref, vco_ref):
    for x_ref, pos_ref, w1_ref, w2_ref, out_ref in ((kc_ref, posk_ref, w1k_ref, w2k_ref, kco_ref),
                                                    (vc_ref, posv_ref, w1v_ref, w2v_ref, vco_ref)):
        a = x_ref[0].astype(F32)
        n = a.shape[0]
        p0 = _dot((a + pos_ref[0:1, :]).astype(BF16), w1_ref[0])
        p1 = _dot((a + pos_ref[1:2, :]).astype(BF16), w1_ref[1])
        hid = _silu(p0 + pltpu.roll(p1, n - 1, 0))
        out_ref[0] = _dot(hid.astype(BF16), w2_ref[...]).astype(BF16)


def _cmp_call(kc3, vc3, posk, w1k, w2k, posv, w1v, w2v):
    b, n, width = kc3.shape
    hid2 = NSA_KV_GROUPS * CMP_HIDDEN
    per_b = lambda i: (i, 0, 0)
    c2 = lambda i: (0, 0)
    c3 = lambda i: (0, 0, 0)
    wspecs = [pl.BlockSpec((2, width), c2), pl.BlockSpec((2, width, hid2), c3), pl.BlockSpec((hid2, NSA_KV_WIDTH), c2)]
    return pl.pallas_call(
        _cmp_kernel,
        grid=(b,),
        in_specs=[pl.BlockSpec((1, n, width), per_b), pl.BlockSpec((1, n, width), per_b)] + wspecs + wspecs,
        out_specs=[pl.BlockSpec((1, n, NSA_KV_WIDTH), per_b)] * 2,
        out_shape=[jax.ShapeDtypeStruct((b, n, NSA_KV_WIDTH), BF16)] * 2,
        compiler_params=pltpu.CompilerParams(dimension_semantics=("parallel",),
                                             vmem_limit_bytes=VMEM_LIMIT),
        name="nsa_compress",
    )(kc3, vc3, posk, w1k, w2k, posv, w1v, w2v)


def _nsa_kernel(q_ref, kc_ref, vc_ref, kvs_ref, small_ref, ovt_ref, e_ref, o_ref,
                qs_ref, m_ref, l_ref, acc_ref):
    tq = q_ref.shape[0]
    rh = NSA_GROUP_HEADS
    n_cmp = kc_ref.shape[1]
    n_blk = ovt_ref.shape[0]
    qi = pl.program_id(1)
    s0 = qi * tq

    lane_half = lax.broadcasted_iota(jnp.int32, (tq, LANES), 1) // NSA_HEAD_DIM
    t_col = s0 + lax.broadcasted_iota(jnp.int32, (tq, 1), 0)
    kc = kc_ref[0]
    vc = vc_ref[0]
    cmp_end = lax.broadcasted_iota(jnp.int32, (1, n_cmp), 1) * CMP_STRIDE + (CMP_BLOCK - 1)
    vis = cmp_end <= t_col

    blk = lax.broadcasted_iota(jnp.int32, (n_blk, tq), 0)
    cur = (s0 + lax.broadcasted_iota(jnp.int32, (n_blk, tq), 1)) // SEL_BLOCK
    valid = blk <= cur
    forced = (blk == 0) | (blk == cur) | (blk == cur - 1)
    k_off = lax.broadcasted_iota(jnp.int32, (1, KB), 1)

    gates = _sigmoid(small_ref[...])

    results = []
    for g in range(NSA_KV_GROUPS):
        in_group = lane_half == g
        o_cmp = []
        p_sum = jnp.zeros((tq, n_cmp), F32)
        for r in range(rh):
            qp = jnp.where(in_group, q_ref[:, r * LANES:(r + 1) * LANES], jnp.zeros((), BF16))
            qs_ref[r * tq:(r + 1) * tq, :] = qp
            s = jnp.where(vis, _dot_nt(qp, kc), NEG_INF)
            m = jnp.max(s, axis=-1, keepdims=True)
            e = jnp.where(vis, jnp.exp(s - m), 0.0)
            den = jnp.sum(e, axis=-1, keepdims=True)
            p = e * (1.0 / jnp.where(den > 0.0, den, 1.0))
            o_cmp.append(_dot(p.astype(BF16), vc))
            p_sum = p_sum + p
        p_hi = p_sum.astype(BF16)
        p_lo = (p_sum - p_hi.astype(F32)).astype(BF16)
        ovt = ovt_ref[...]
        imp_t = _dot_nt(ovt, p_hi) + _dot_nt(ovt, p_lo)
        score = jnp.where(valid, jnp.where(forced, FORCED_SCORE, imp_t), -1.0)
        rank = jnp.zeros((n_blk, tq), F32)
        for i in range(n_blk):
            si = score[i:i + 1, :]
            beats = (si > score) | ((si == score) & (blk > i))
            rank = rank + jnp.where(beats, 1.0, 0.0)
        sel_t = jnp.where((rank < float(SEL_COUNT)) & valid, 1.0, 0.0)
        sel = sel_t.T.astype(BF16)

        def flash(n_iter, k_lo, v_lo, mask_fn):
            m_ref[...] = jnp.full(m_ref.shape, NEG_INF, F32)
            l_ref[...] = jnp.zeros(l_ref.shape, F32)
            acc_ref[...] = jnp.zeros(acc_ref.shape, F32)

            def body(i, carry):
                kb = qi - i
                k0 = pl.multiple_of(kb * KB, KB)
                kt = kvs_ref[pl.ds(k0, KB), k_lo:k_lo + LANES]
                vt = kvs_ref[pl.ds(k0, KB), v_lo:v_lo + LANES]
                s = _dot_nt(qs_ref[...], kt).reshape(rh, tq, KB)
                bias = jnp.where(mask_fn(kb, k0 + k_off), 0.0, NEG_INF)
                s = s + bias[None]
                m_old = m_ref[...].reshape(rh, tq, 1)
                m_new = jnp.maximum(m_old, jnp.max(s, axis=-1, keepdims=True))
                alpha = jnp.exp(m_old - m_new)
                p = jnp.exp(s - m_new)
                l_ref[...] = (alpha * l_ref[...].reshape(rh, tq, 1)
                              + jnp.sum(p, axis=-1, keepdims=True)).reshape(rh * tq, 1)
                pv = _dot(p.reshape(rh * tq, KB).astype(BF16), vt)
                acc_ref[...] = alpha.reshape(rh * tq, 1) * acc_ref[...] + pv
                m_ref[...] = m_new.reshape(rh * tq, 1)
                return carry

            lax.fori_loop(0, n_iter, body, 0)
            return acc_ref[...] * (1.0 / l_ref[...])

        def sel_mask(kb, kpos):
            mk = _dot(sel, e_ref[kb])
            return (mk > 0.5) & (kpos <= t_col)

        def win_mask(kb, kpos):
            rel = t_col - kpos
            return (rel >= 0) & (rel < WINDOW)

        o_sel = flash(qi + 1, 0, LANES, sel_mask)
        o_win = flash(jnp.minimum(qi, WINDOW // KB) + 1, 2 * LANES, 3 * LANES, win_mask)

        per_head = []
        for r in range(rh):
            gcol = SMALL_GATE + (g * rh + r) * 3
            per_head.append(gates[:, gcol:gcol + 1] * o_cmp[r]
                            + gates[:, gcol + 1:gcol + 2] * o_sel[r * tq:(r + 1) * tq, :]
                            + gates[:, gcol + 2:gcol + 3] * o_win[r * tq:(r + 1) * tq, :])
        results.append(per_head)

    for r in range(rh):
        o_ref[:, r * LANES:(r + 1) * LANES] = jnp.where(lane_half == 0, results[0][r], results[1][r]).astype(BF16)


def _nsa_call(q, kc, vc, kvs, small, ovt, e_mat, batch, seq):
    nq = seq // TQ
    n_cmp = kc.shape[1]
    row = lambda b, i: (b * nq + i, 0)
    per_b3 = lambda b, i: (b, 0, 0)
    per_b2 = lambda b, i: (b, 0)
    c2 = lambda b, i: (0, 0)
    c3 = lambda b, i: (0, 0, 0)
    rows = NSA_GROUP_HEADS * TQ
    return pl.pallas_call(
        _nsa_kernel,
        grid=(batch, nq),
        in_specs=[pl.BlockSpec((TQ, NSA_WIDTH), row),
                  pl.BlockSpec((1, n_cmp, NSA_KV_WIDTH), per_b3),
                  pl.BlockSpec((1, n_cmp, NSA_KV_WIDTH), per_b3),
                  pl.BlockSpec((seq, 4 * NSA_KV_WIDTH), per_b2),
                  pl.BlockSpec((TQ, LANES), row),
                  pl.BlockSpec(ovt.shape, c2),
                  pl.BlockSpec(e_mat.shape, c3)],
        out_specs=pl.BlockSpec((TQ, NSA_WIDTH), row),
        out_shape=jax.ShapeDtypeStruct((batch * seq, NSA_WIDTH), BF16),
        scratch_shapes=[pltpu.VMEM((rows, LANES), BF16),
                        pltpu.VMEM((rows, 1), F32),
                        pltpu.VMEM((rows, 1), F32),
                        pltpu.VMEM((rows, LANES), F32)],
        compiler_params=pltpu.CompilerParams(dimension_semantics=("parallel", "arbitrary"),
                                             vmem_limit_bytes=VMEM_LIMIT),
        name="nsa_attention",
    )(q, kc, vc, kvs, small, ovt, e_mat)


def _out_kernel(x_ref, oa_ref, ob_ref, mg_ref, wa_ref, wb_ref, wo_ref, g2_ref, wgu_ref, wd_ref, gf_ref, out_ref):
    a = _dot(oa_ref[...], wa_ref[...])
    b = _dot(ob_ref[...], wb_ref[...])
    merged = (_sigmoid(mg_ref[:, 0:D_MODEL].astype(F32)) * a
              + _sigmoid(mg_ref[:, D_MODEL:2 * D_MODEL].astype(F32)) * b)
    x1 = x_ref[...] + _dot(merged.astype(BF16), wo_ref[...])
    h2 = _rms(x1, g2_ref[...]).astype(BF16)
    gate = _dot(h2, wgu_ref[:, 0:FFN_HIDDEN])
    up = _dot(h2, wgu_ref[:, FFN_HIDDEN:2 * FFN_HIDDEN])
    y = x1 + _dot((_silu(gate) * up).astype(BF16), wd_ref[...])
    out_ref[...] = _rms(y, gf_ref[...])


def _out_call(x2d, o_a, o_b, mg, wa, wb, wo, g2, wgu, wd, gf, tm=256):
    t = x2d.shape[0]
    row = lambda i: (i, 0)
    const = lambda i: (0, 0)

    def wspec(w):
        return pl.BlockSpec(w.shape, const, pipeline_mode=pl.Buffered(1))

    return pl.pallas_call(
        _out_kernel,
        grid=(t // tm,),
        in_specs=[pl.BlockSpec((tm, D_MODEL), row),
                  pl.BlockSpec((tm, GDN_WIDTH), row),
                  pl.BlockSpec((tm, NSA_WIDTH), row),
                  pl.BlockSpec((tm, 2 * D_MODEL), row),
                  wspec(wa), wspec(wb), wspec(wo),
                  pl.BlockSpec((1, D_MODEL), const),
                  wspec(wgu), wspec(wd),
                  pl.BlockSpec((1, D_MODEL), const)],
        out_specs=pl.BlockSpec((tm, D_MODEL), row),
        out_shape=jax.ShapeDtypeStruct((t, D_MODEL), F32),
        compiler_params=pltpu.CompilerParams(dimension_semantics=("parallel",),
                                             vmem_limit_bytes=VMEM_LIMIT),
        name="merge_ffn",
    )(x2d, o_a, o_b, mg, wa, wb, wo, g2, wgu, wd, gf)


def _q_head_perm():
    idx = []
    for r in range(NSA_GROUP_HEADS):
        for g in range(NSA_KV_GROUPS):
            base = (g * NSA_GROUP_HEADS + r) * NSA_HEAD_DIM
            idx.extend(range(base, base + NSA_HEAD_DIM))
    return np.asarray(idx, dtype=np.int32)


def _reorder_w_in(w_in):
    sizes = (3 * GDN_WIDTH, GDN_WIDTH, GDN_HEADS, GDN_HEADS, NSA_WIDTH) + (NSA_KV_WIDTH,) * 6 + (
        3 * NSA_HEADS, D_MODEL, D_MODEL)
    offs = np.concatenate([[0], np.cumsum(sizes)])
    seg = lambda i: w_in[:, offs[i]:offs[i + 1]]
    qkv, z, beta, alpha, q, kc, vc, ks, vs, kw, vw, gate, ma, mb = [seg(i) for i in range(14)]
    pad = jnp.zeros((D_MODEL, LANES - 2 * GDN_HEADS - 3 * NSA_HEADS), w_in.dtype)
    q = q[:, _q_head_perm()]
    return jnp.concatenate([qkv, z, q, kc, vc, ks, vs, kw, vw, ma, mb, beta, alpha, gate, pad], axis=1).astype(BF16)


def _rope_tables(seq):
    half = ROPE_DIM // 2
    inv_freq = ROPE_THETA ** (-jnp.arange(half, dtype=F32) / half)
    ang = jnp.arange(seq, dtype=F32)[:, None] * inv_freq
    cos, sin = jnp.cos(ang), jnp.sin(ang)
    ones = jnp.ones((seq, NSA_HEAD_DIM - ROPE_DIM), F32)
    zeros = jnp.zeros((seq, NSA_HEAD_DIM - ROPE_DIM), F32)
    z8 = jnp.zeros((seq, half), F32)
    c = jnp.concatenate([cos, cos, ones], axis=1)
    s1 = jnp.concatenate([z8, sin, zeros], axis=1)
    s2 = jnp.concatenate([-sin, z8, zeros], axis=1)
    tile2 = lambda a: jnp.concatenate([a, a], axis=1)
    return tile2(c), tile2(s1), tile2(s2)


def _cmp_weights(pos, w1, w2):
    g = NSA_KV_GROUPS
    seg = CMP_BLOCK // 2
    eye = jnp.eye(g, dtype=w1.dtype)
    pos_h = pos.reshape(2, seg, 1, NSA_HEAD_DIM)
    pos_flat = jnp.broadcast_to(pos_h, (2, seg, g, NSA_HEAD_DIM)).reshape(2, seg * g * NSA_HEAD_DIM)
    w1_h = w1.reshape(2, seg, NSA_HEAD_DIM, CMP_HIDDEN)
    w1_bd = jnp.einsum('alds,gk->algdks', w1_h, eye).reshape(2, seg * g * NSA_HEAD_DIM, g * CMP_HIDDEN)
    w2_bd = jnp.einsum('hd,gk->ghkd', w2, eye).reshape(g * CMP_HIDDEN, g * NSA_HEAD_DIM)
    return pos_flat.astype(F32), w1_bd.astype(BF16), w2_bd.astype(BF16)


def _selection_constants(seq):
    n_cmp_pad = seq // CMP_STRIDE
    n_blk = seq // SEL_BLOCK
    cmp_start = np.arange(n_cmp_pad) * CMP_STRIDE
    cmp_end = cmp_start + CMP_BLOCK - 1
    sel_start = np.arange(n_blk) * SEL_BLOCK
    ovt = ((cmp_start[None, :] <= sel_start[:, None] + SEL_BLOCK - 1) & (cmp_end[None, :] >= sel_start[:, None]))
    ovt[:, n_cmp_pad - 1] = False
    n_kb = seq // KB
    key_blk = (np.arange(n_kb)[:, None] * KB + np.arange(KB)[None, :]) // SEL_BLOCK
    e_mat = np.arange(n_blk)[None, :, None] == key_blk[:, None, :]
    return jnp.asarray(ovt, BF16), jnp.asarray(e_mat, BF16)


def _lane_vector(values, offset):
    v = jnp.zeros((1, LANES), F32)
    return v.at[0, offset:offset + values.shape[0]].set(values.astype(F32))


def _hybrid_block(x, mix_norm_gain, w_in, gdn_conv_w, gdn_a_log, gdn_dt_bias, gdn_out_norm_gain,
                  cmp_pos_k, cmp_w1_k, cmp_w2_k, cmp_pos_v, cmp_w1_v, cmp_w2_v,
                  w_branch_gdn, w_branch_nsa, w_out, ffn_norm_gain, w_gate_up, w_down, final_norm_gain):
    batch, seq, _ = x.shape
    x2d = x.reshape(batch * seq, D_MODEL)
    rc, rs1, rs2 = _rope_tables(seq)
    qkv, z, q, kc, vc, kvs, mg, small = _inproj_call(
        x2d, mix_norm_gain.reshape(1, D_MODEL), _reorder_w_in(w_in), rc, rs1, rs2, seq)

    o_a = _gdn_call(qkv, z, small, gdn_conv_w.astype(F32),
                    _lane_vector(gdn_a_log, SMALL_ALPHA), _lane_vector(gdn_dt_bias, SMALL_ALPHA),
                    gdn_out_norm_gain.reshape(1, GDN_HEAD_DIM).astype(F32), batch, seq)

    seg_width = (CMP_BLOCK // 2) * NSA_KV_WIDTH
    kc3 = kc.reshape(batch, seq // CMP_STRIDE, seg_width)
    vc3 = vc.reshape(batch, seq // CMP_STRIDE, seg_width)
    kcc, vcc = _cmp_call(kc3, vc3, *_cmp_weights(cmp_pos_k, cmp_w1_k, cmp_w2_k),
                         *_cmp_weights(cmp_pos_v, cmp_w1_v, cmp_w2_v))
    ovt, e_mat = _selection_constants(seq)
    o_b = _nsa_call(q, kcc, vcc, kvs, small, ovt, e_mat, batch, seq)

    out = _out_call(x2d, o_a, o_b, mg,
                    w_branch_gdn.astype(BF16), w_branch_nsa[_q_head_perm(), :].astype(BF16), w_out.astype(BF16),
                    ffn_norm_gain.reshape(1, D_MODEL), w_gate_up.astype(BF16), w_down.astype(BF16),
                    final_norm_gain.reshape(1, D_MODEL))
    return out.reshape(batch, seq, D_MODEL)


def kernel(x, mix_norm_gain, w_in, gdn_conv_w, gdn_a_log, gdn_dt_bias, gdn_out_norm_gain, cmp_pos_k, cmp_w1_k,
           cmp_w2_k, cmp_pos_v, cmp_w1_v, cmp_w2_v, w_branch_gdn, w_branch_nsa, w_out, ffn_norm_gain, w_gate_up,
           w_down, final_norm_gain):
    assert mix_norm_gain.shape[0] == 1, "single-layer block"
    return _hybrid_block(x, mix_norm_gain[0], w_in[0], gdn_conv_w[0], gdn_a_log[0], gdn_dt_bias[0],
                         gdn_out_norm_gain[0], cmp_pos_k[0], cmp_w1_k[0], cmp_w2_k[0], cmp_pos_v[0], cmp_w1_v[0],
                         cmp_w2_v[0], w_branch_gdn[0], w_branch_nsa[0], w_out[0], ffn_norm_gain[0], w_gate_up[0],
                         w_down[0], final_norm_gain)
```

```python
import functools

import numpy as np
import jax
import jax.numpy as jnp
from jax import lax
from jax.experimental import pallas as pl
from jax.experimental.pallas import tpu as pltpu

F32 = jnp.float32
BF16 = jnp.bfloat16

D_MODEL = 1024
NORM_EPS = 1e-6
GDN_HEADS = 4
GDN_HEAD_DIM = 128
GDN_WIDTH = GDN_HEADS * GDN_HEAD_DIM
GDN_CONV = 4
GDN_CHUNK = 64
NSA_HEADS = 8
NSA_KV_GROUPS = 2
NSA_GROUP_HEADS = NSA_HEADS // NSA_KV_GROUPS
NSA_HEAD_DIM = 64
NSA_WIDTH = NSA_HEADS * NSA_HEAD_DIM
NSA_KV_WIDTH = NSA_KV_GROUPS * NSA_HEAD_DIM
CMP_BLOCK = 32
CMP_STRIDE = 16
CMP_HIDDEN = 128
SEL_BLOCK = 64
SEL_COUNT = 16
WINDOW = 512
ROPE_THETA = 500000.0
ROPE_DIM = NSA_HEAD_DIM // 4
FORCED_SCORE = 1000.0
NEG_INF = -1e30
FFN_HIDDEN = 2816

LANES = 128
VMEM_LIMIT = 56 * 1024 * 1024

C_QKV = (0, 1536)
C_Z = (1536, 2048)
C_Q = (2048, 2560)
C_KC = (2560, 2688)
C_VC = (2688, 2816)
C_KVS = (2816, 3328)
C_MG = (3328, 5376)
C_SMALL = (5376, 5504)
IN_WIDTH_PADDED = 5504
SMALL_BETA = 0
SMALL_ALPHA = 4
SMALL_GATE = 8

TQ = 128
KB = 512


def _dot(a, b):
    return jnp.dot(a, b, preferred_element_type=F32)


def _dot_nt(a, b):
    return lax.dot_general(a, b, (((1,), (1,)), ((), ())), preferred_element_type=F32)


def _dot_tn(a, b):
    return lax.dot_general(a, b, (((0,), (0,)), ((), ())), preferred_element_type=F32)


def _sigmoid(x):
    return 1.0 / (1.0 + jnp.exp(-x))


def _silu(x):
    return x * _sigmoid(x)


def _rms(x, gain):
    return x * lax.rsqrt(jnp.mean(x * x, axis=-1, keepdims=True) + NORM_EPS) * gain


def _rope_slab(y, c, s1, s2):
    return y * c + pltpu.roll(y, 8, 1) * s1 + pltpu.roll(y, LANES - 8, 1) * s2


def _inproj_kernel(x_ref, gain_ref, w_ref, rc_ref, rs1_ref, rs2_ref,
                   qkv_ref, z_ref, q_ref, kc_ref, vc_ref, kvs_ref, mg_ref, small_ref):
    hb = _rms(x_ref[...], gain_ref[...]).astype(BF16)

    def proj(cols):
        return _dot(hb, w_ref[:, cols[0]:cols[1]])

    qkv_ref[...] = proj(C_QKV).astype(BF16)
    z_ref[...] = proj(C_Z).astype(BF16)
    mg_ref[...] = proj(C_MG).astype(BF16)
    small_ref[...] = proj(C_SMALL)
    vc_ref[...] = proj(C_VC).astype(BF16)

    c, s1, s2 = rc_ref[...], rs1_ref[...], rs2_ref[...]
    q = proj(C_Q)
    for r in range(NSA_GROUP_HEADS):
        sl = slice(r * LANES, (r + 1) * LANES)
        q_ref[:, sl] = (_rope_slab(q[:, sl], c, s1, s2) * (NSA_HEAD_DIM ** -0.5)).astype(BF16)
    kc_ref[...] = _rope_slab(proj(C_KC), c, s1, s2).astype(BF16)
    kvs = proj(C_KVS)
    kvs_ref[:, 0:128] = _rope_slab(kvs[:, 0:128], c, s1, s2).astype(BF16)
    kvs_ref[:, 128:256] = kvs[:, 128:256].astype(BF16)
    kvs_ref[:, 256:384] = _rope_slab(kvs[:, 256:384], c, s1, s2).astype(BF16)
    kvs_ref[:, 384:512] = kvs[:, 384:512].astype(BF16)


def _inproj_call(x2d, gain, w_r, rc, rs1, rs2, seq, tm=256):
    t = x2d.shape[0]
    n_seq_tiles = seq // tm
    row = lambda i: (i, 0)
    const = lambda i: (0, 0)
    tab = lambda i: (i % n_seq_tiles, 0)
    widths = [(1536, BF16), (512, BF16), (512, BF16), (128, BF16), (128, BF16), (512, BF16), (2048, BF16),
              (128, F32)]
    return pl.pallas_call(
        _inproj_kernel,
        grid=(t // tm,),
        in_specs=[pl.BlockSpec((tm, D_MODEL), row),
                  pl.BlockSpec((1, D_MODEL), const),
                  pl.BlockSpec((D_MODEL, IN_WIDTH_PADDED), const, pipeline_mode=pl.Buffered(1)),
                  pl.BlockSpec((tm, LANES), tab),
                  pl.BlockSpec((tm, LANES), tab),
                  pl.BlockSpec((tm, LANES), tab)],
        out_specs=[pl.BlockSpec((tm, w), row) for w, _ in widths],
        out_shape=[jax.ShapeDtypeStruct((t, w), d) for w, d in widths],
        compiler_params=pltpu.CompilerParams(dimension_semantics=("parallel",),
                                             vmem_limit_bytes=VMEM_LIMIT),
        name="inproj",
    )(x2d, gain, w_r, rc, rs1, rs2)


def _gdn_kernel(qkv_ref, z_ref, small_ref, convw_ref, alog_ref, dtb_ref, ogain_ref, o_ref,
                xbuf, cbuf, state):
    cb = qkv_ref.shape[0]
    c = GDN_CHUNK
    n_chunks = cb // c
    heads = range(GDN_HEADS)
    j = pl.program_id(1)

    @pl.when(j == 0)
    def _():
        xbuf[0:8, :] = jnp.zeros((8, 3 * GDN_WIDTH), F32)
        state[...] = jnp.zeros_like(state)

    @pl.when(j > 0)
    def _():
        xbuf[0:8, :] = xbuf[cb:cb + 8, :]

    xbuf[8:8 + cb, :] = qkv_ref[...].astype(F32)
    w = convw_ref[...]
    acc = xbuf[8:8 + cb, :] * w[GDN_CONV - 1:GDN_CONV, :]
    for jj in range(GDN_CONV - 1):
        off = 8 - (GDN_CONV - 1) + jj
        acc = acc + xbuf[off:off + cb, :] * w[jj:jj + 1, :]
    cbuf[...] = _silu(acc)

    row = lax.broadcasted_iota(jnp.int32, (c, c), 0)
    col = lax.broadcasted_iota(jnp.int32, (c, c), 1)
    causal = row >= col
    strict = row > col
    tril = causal.astype(F32)
    eye = (row == col).astype(F32)
    neg_decay_rate = -jnp.exp(alog_ref[...])
    dtb = dtb_ref[...]
    ogain = ogain_ref[...]

    pairs = [(ci, h) for ci in range(n_chunks) for h in heads]
    beta_c, gc_c, gct_c = [], [], []
    for ci in range(n_chunks):
        sm = small_ref[ci * c:(ci + 1) * c, :]
        beta_c.append(_sigmoid(sm))
        xg = sm + dtb
        softplus = jnp.maximum(xg, 0.0) + jnp.log(1.0 + jnp.exp(-jnp.abs(xg)))
        gc_all = jnp.dot(tril, neg_decay_rate * softplus, precision=lax.Precision.HIGHEST,
                         preferred_element_type=F32)
        gc_c.append(gc_all)
        gct_c.append(gc_all.T)

    def head_cols(base, ci, h):
        lo = base + h * GDN_HEAD_DIM
        return cbuf[ci * c:(ci + 1) * c, lo:lo + GDN_HEAD_DIM]

    q_l, k_l, kb_l, vb_l, decay_l, eg_l, glast_l, gc_l = [], [], [], [], [], [], [], []
    for ci, h in pairs:
        q = head_cols(0, ci, h)
        k = head_cols(GDN_WIDTH, ci, h)
        v = head_cols(2 * GDN_WIDTH, ci, h)
        q = q * lax.rsqrt(jnp.sum(q * q, axis=-1, keepdims=True) + NORM_EPS) * (GDN_HEAD_DIM ** -0.5)
        k = k * lax.rsqrt(jnp.sum(k * k, axis=-1, keepdims=True) + NORM_EPS)
        beta = beta_c[ci][:, SMALL_BETA + h:SMALL_BETA + h + 1]
        gc = gc_c[ci][:, SMALL_ALPHA + h:SMALL_ALPHA + h + 1]
        gr = gct_c[ci][SMALL_ALPHA + h:SMALL_ALPHA + h + 1, :]
        decay_l.append(jnp.where(causal, jnp.exp(jnp.where(causal, gc - gr, 0.0)), 0.0))
        eg_l.append(jnp.exp(gc))
        glast_l.append(gc[c - 1:c, :])
        gc_l.append(gc)
        q_l.append(q)
        k_l.append(k)
        kb_l.append(k * beta)
        vb_l.append((v * beta).astype(BF16))
    kbf_l = [k.astype(BF16) for k in k_l]
    a_l = [jnp.where(strict, _dot_nt(kb.astype(BF16), kbf) * d, 0.0) for kb, kbf, d in zip(kb_l, kbf_l, decay_l)]
    t_l = [eye - a for a in a_l]
    p_l = a_l
    n = 2
    while n < c:
        p_l = [_dot(p.astype(BF16), p.astype(BF16)) for p in p_l]
        t_l = [t + _dot(t.astype(BF16), p.astype(BF16)) for t, p in zip(t_l, p_l)]
        n *= 2
    tb_l = [t.astype(BF16) for t in t_l]
    u_l = [_dot(t, vb) for t, vb in zip(tb_l, vb_l)]
    w_l = [_dot(t, (kb * eg).astype(BF16)).astype(BF16) for t, kb, eg in zip(tb_l, kb_l, eg_l)]
    qk_l = [(_dot_nt(q.astype(BF16), kbf) * d).astype(BF16) for q, kbf, d in zip(q_l, kbf_l, decay_l)]
    qg_l = [(q * eg).astype(BF16) for q, eg in zip(q_l, eg_l)]
    kd_l = [(k * jnp.exp(gl - gc)).astype(BF16) for k, gl, gc in zip(k_l, glast_l, gc_l)]

    s_l = [state[h] for h in heads]
    for ci in range(n_chunks):
        idx = [ci * GDN_HEADS + h for h in heads]
        sb_l = [s.astype(BF16) for s in s_l]
        vn_l = [(u_l[i] - _dot(w_l[i], sb)).astype(BF16) for i, sb in zip(idx, sb_l)]
        o_l = [_dot(qg_l[i], sb) + _dot(qk_l[i], vn) for i, sb, vn in zip(idx, sb_l, vn_l)]
        s_l = [s * jnp.exp(glast_l[i]) + _dot_tn(kd_l[i], vn) for i, s, vn in zip(idx, s_l, vn_l)]
        for h, o in zip(heads, o_l):
            hs = slice(h * GDN_HEAD_DIM, (h + 1) * GDN_HEAD_DIM)
            zh = z_ref[ci * c:(ci + 1) * c, hs].astype(F32)
            o = o * lax.rsqrt(jnp.mean(o * o, axis=-1, keepdims=True) + NORM_EPS) * ogain * _silu(zh)
            o_ref[ci * c:(ci + 1) * c, hs] = o.astype(BF16)
    for h in heads:
        state[h] = s_l[h]


def _gdn_call(qkv, z, small, conv_w, alog_v, dtb_v, ogain, batch, seq, cb=256):
    nj = seq // cb
    row = lambda b, j: (b * nj + j, 0)
    const = lambda b, j: (0, 0)
    return pl.pallas_call(
        _gdn_kernel,
        grid=(batch, nj),
        in_specs=[pl.BlockSpec((cb, 3 * GDN_WIDTH), row),
                  pl.BlockSpec((cb, GDN_WIDTH), row),
                  pl.BlockSpec((cb, LANES), row),
                  pl.BlockSpec((GDN_CONV, 3 * GDN_WIDTH), const),
                  pl.BlockSpec((1, LANES), const),
                  pl.BlockSpec((1, LANES), const),
                  pl.BlockSpec((1, GDN_HEAD_DIM), const)],
        out_specs=pl.BlockSpec((cb, GDN_WIDTH), row),
        out_shape=jax.ShapeDtypeStruct((batch * seq, GDN_WIDTH), BF16),
        scratch_shapes=[pltpu.VMEM((cb + 8, 3 * GDN_WIDTH), F32),
                        pltpu.VMEM((cb, 3 * GDN_WIDTH), F32),
                        pltpu.VMEM((GDN_HEADS, GDN_HEAD_DIM, GDN_HEAD_DIM), F32)],
        compiler_params=pltpu.CompilerParams(dimension_semantics=("parallel", "arbitrary"),
                                             vmem_limit_bytes=VMEM_LIMIT),
        name="gdn",
    )(qkv, z, small, conv_w, alog_v, dtb_v, ogain)


def _cmp_kernel(kc_ref, vc_ref, posk_ref, w1k_ref, w2k_ref, posv_ref, w1v_ref, w2v_ref, kco_ref, vco_ref):
    for x_ref, pos_ref, w1_ref, w2_ref, out_ref in ((kc_ref, posk_ref, w1k_ref, w2k_ref, kco_ref),
                                                    (vc_ref, posv_ref, w1v_ref, w2v_ref, vco_ref)):
        a = x_ref[0].astype(F32)
        n = a.shape[0]
        p0 = _dot((a + pos_ref[0:1, :]).astype(BF16), w1_ref[0])
        p1 = _dot((a + pos_ref[1:2, :]).astype(BF16), w1_ref[1])
        hid = _silu(p0 + pltpu.roll(p1, n - 1, 0))
        out_ref[0] = _dot(hid.astype(BF16), w2_ref[...]).astype(BF16)


def _cmp_call(kc3, vc3, posk, w1k, w2k, posv, w1v, w2v):
    b, n, width = kc3.shape
    hid2 = NSA_KV_GROUPS * CMP_HIDDEN
    per_b = lambda i: (i, 0, 0)
    c2 = lambda i: (0, 0)
    c3 = lambda i: (0, 0, 0)
    wspecs = [pl.BlockSpec((2, width), c2), pl.BlockSpec((2, width, hid2), c3), pl.BlockSpec((hid2, NSA_KV_WIDTH), c2)]
    return pl.pallas_call(
        _cmp_kernel,
        grid=(b,),
        in_specs=[pl.BlockSpec((1, n, width), per_b), pl.BlockSpec((1, n, width), per_b)] + wspecs + wspecs,
        out_specs=[pl.BlockSpec((1, n, NSA_KV_WIDTH), per_b)] * 2,
        out_shape=[jax.ShapeDtypeStruct((b, n, NSA_KV_WIDTH), BF16)] * 2,
        compiler_params=pltpu.CompilerParams(dimension_semantics=("parallel",),
                                             vmem_limit_bytes=VMEM_LIMIT),
        name="nsa_compress",
    )(kc3, vc3, posk, w1k, w2k, posv, w1v, w2v)


def _nsa_kernel(q_ref, kc_ref, vc_ref, kvs_ref, small_ref, ovt_ref, et_ref, o_ref,
                qs_ref, m_ref, acc_ref):
    tq = q_ref.shape[0]
    rh = NSA_GROUP_HEADS
    n_cmp = kc_ref.shape[1]
    n_blk = ovt_ref.shape[0]
    qi = pl.program_id(1)
    s0 = qi * tq

    row_half = lax.broadcasted_iota(jnp.int32, (LANES, tq), 0) // NSA_HEAD_DIM
    t_row = s0 + lax.broadcasted_iota(jnp.int32, (1, tq), 1)
    k_col = lax.broadcasted_iota(jnp.int32, (KB, 1), 0)
    q_t = [q_ref[:, r * LANES:(r + 1) * LANES].astype(F32).T for r in range(rh)]
    gates_t = _sigmoid(small_ref[...]).T
    kc = kc_ref[0]
    vc = vc_ref[0]
    cmp_end = lax.broadcasted_iota(jnp.int32, (n_cmp, 1), 0) * CMP_STRIDE + (CMP_BLOCK - 1)
    vis1 = cmp_end <= t_row
    vis = jnp.concatenate([vis1] * rh, axis=1)

    blk = lax.broadcasted_iota(jnp.int32, (n_blk, tq), 0)
    cur = (s0 + lax.broadcasted_iota(jnp.int32, (n_blk, tq), 1)) // SEL_BLOCK
    valid = blk <= cur
    forced = (blk == 0) | (blk == cur) | (blk == cur - 1)

    def tile4(a):
        return jnp.concatenate([a] * rh, axis=1)

    results = []
    for g in range(NSA_KV_GROUPS):
        qs = jnp.concatenate([jnp.where(row_half == g, q_t[r], 0.0) for r in range(rh)], axis=1).astype(BF16)
        qs_ref[...] = qs
        s = jnp.where(vis, _dot(kc, qs), NEG_INF)
        m = jnp.max(s, axis=0, keepdims=True)
        e = jnp.where(vis, jnp.exp(s - m), 0.0)
        den = jnp.sum(e, axis=0, keepdims=True)
        p = e * (1.0 / jnp.where(den > 0.0, den, 1.0))
        o_cmp = _dot_tn(vc, p.astype(BF16))
        p_sum = p[:, 0:tq]
        for r in range(1, rh):
            p_sum = p_sum + p[:, r * tq:(r + 1) * tq]
        p_hi = p_sum.astype(BF16)
        p_lo = (p_sum - p_hi.astype(F32)).astype(BF16)
        ovt = ovt_ref[...]
        imp_t = _dot(ovt, p_hi) + _dot(ovt, p_lo)
        score = jnp.where(valid, jnp.where(forced, FORCED_SCORE, imp_t), -1.0)
        rank = jnp.zeros((n_blk, tq), F32)
        for i in range(n_blk):
            si = score[i:i + 1, :]
            beats = (si > score) | ((si == score) & (blk > i))
            rank = rank + jnp.where(beats, 1.0, 0.0)
        sel_t = jnp.where((rank < float(SEL_COUNT)) & valid, 1.0, 0.0).astype(BF16)

        def with_ones(vt):
            in_group = lax.broadcasted_iota(jnp.int32, vt.shape, 1) // NSA_HEAD_DIM == g
            return jnp.where(in_group, vt.astype(F32), 1.0).astype(BF16)

        def normalized(acc):
            l_row = acc[(1 - g) * NSA_HEAD_DIM:(1 - g) * NSA_HEAD_DIM + 1, :]
            return acc * (1.0 / l_row)

        m_ref[...] = jnp.full(m_ref.shape, NEG_INF, F32)
        acc_ref[...] = jnp.zeros(acc_ref.shape, F32)
        kb_diag = s0 // KB

        def sel_body(i, carry):
            kb = kb_diag - i
            k0 = pl.multiple_of(kb * KB, KB)
            kt = kvs_ref[pl.ds(k0, KB), 0:LANES]
            va = with_ones(kvs_ref[pl.ds(k0, KB), LANES:2 * LANES])
            mk = _dot(et_ref[kb], sel_t)
            ok = (mk > 0.5) & (k0 + k_col <= t_row)
            s = _dot(kt, qs_ref[...]) + tile4(jnp.where(ok, 0.0, NEG_INF))
            m_old = m_ref[...]
            m_new = jnp.maximum(m_old, jnp.max(s, axis=0, keepdims=True))
            alpha = jnp.exp(m_old - m_new)
            p = jnp.exp(s - m_new)
            acc_ref[...] = alpha * acc_ref[...] + _dot_tn(va, p.astype(BF16))
            m_ref[...] = m_new
            return carry

        lax.fori_loop(0, kb_diag + 1, sel_body, 0)
        o_sel = normalized(acc_ref[...])

        w0 = pl.multiple_of(jnp.maximum(s0 - WINDOW, 0), TQ)
        kw = kvs_ref[pl.ds(w0, WINDOW + TQ), 2 * LANES:3 * LANES]
        vw = with_ones(kvs_ref[pl.ds(w0, WINDOW + TQ), 3 * LANES:4 * LANES])
        rel = t_row - (w0 + lax.broadcasted_iota(jnp.int32, (WINDOW + TQ, 1), 0))
        sw = _dot(kw, qs) + tile4(jnp.where((rel >= 0) & (rel < WINDOW), 0.0, NEG_INF))
        pw = jnp.exp(sw - jnp.max(sw, axis=0, keepdims=True))
        o_win = normalized(_dot_tn(vw, pw.astype(BF16)))

        per_head = []
        for r in range(rh):
            gcol = SMALL_GATE + (g * rh + r) * 3
            cs = slice(r * tq, (r + 1) * tq)
            per_head.append(gates_t[gcol:gcol + 1, :] * o_cmp[:, cs]
                            + gates_t[gcol + 1:gcol + 2, :] * o_sel[:, cs]
                            + gates_t[gcol + 2:gcol + 3, :] * o_win[:, cs])
        results.append(per_head)

    for r in range(rh):
        slab_t = jnp.where(row_half == 0, results[0][r], results[1][r])
        o_ref[:, r * LANES:(r + 1) * LANES] = slab_t.T.astype(BF16)


def _nsa_call(q, kc, vc, kvs, small, ovt, e_mat, batch, seq):
    nq = seq // TQ
    n_cmp = kc.shape[1]
    row = lambda b, i: (b * nq + i, 0)
    per_b3 = lambda b, i: (b, 0, 0)
    per_b2 = lambda b, i: (b, 0)
    c2 = lambda b, i: (0, 0)
    c3 = lambda b, i: (0, 0, 0)
    rows = NSA_GROUP_HEADS * TQ
    return pl.pallas_call(
        _nsa_kernel,
        grid=(batch, nq),
        in_specs=[pl.BlockSpec((TQ, NSA_WIDTH), row),
                  pl.BlockSpec((1, n_cmp, NSA_KV_WIDTH), per_b3),
                  pl.BlockSpec((1, n_cmp, NSA_KV_WIDTH), per_b3),
                  pl.BlockSpec((seq, 4 * NSA_KV_WIDTH), per_b2),
                  pl.BlockSpec((TQ, LANES), row),
                  pl.BlockSpec(ovt.shape, c2),
                  pl.BlockSpec(e_mat.shape, c3)],
        out_specs=pl.BlockSpec((TQ, NSA_WIDTH), row),
        out_shape=jax.ShapeDtypeStruct((batch * seq, NSA_WIDTH), BF16),
        scratch_shapes=[pltpu.VMEM((LANES, rows), BF16),
                        pltpu.VMEM((1, rows), F32),
                        pltpu.VMEM((LANES, rows), F32)],
        compiler_params=pltpu.CompilerParams(dimension_semantics=("parallel", "arbitrary"),
                                             vmem_limit_bytes=VMEM_LIMIT),
        name="nsa_attention",
    )(q, kc, vc, kvs, small, ovt, e_mat)


def _out_kernel(x_ref, oa_ref, ob_ref, mg_ref, wa_ref, wb_ref, wo_ref, g2_ref, wgu_ref, wd_ref, gf_ref, out_ref):
    a = _dot(oa_ref[...], wa_ref[...])
    b = _dot(ob_ref[...], wb_ref[...])
    merged = (_sigmoid(mg_ref[:, 0:D_MODEL].astype(F32)) * a
              + _sigmoid(mg_ref[:, D_MODEL:2 * D_MODEL].astype(F32)) * b)
    x1 = x_ref[...] + _dot(merged.astype(BF16), wo_ref[...])
    h2 = _rms(x1, g2_ref[...]).astype(BF16)
    gate = _dot(h2, wgu_ref[:, 0:FFN_HIDDEN])
    up = _dot(h2, wgu_ref[:, FFN_HIDDEN:2 * FFN_HIDDEN])
    y = x1 + _dot((_silu(gate) * up).astype(BF16), wd_ref[...])
    out_ref[...] = _rms(y, gf_ref[...])


def _out_call(x2d, o_a, o_b, mg, wa, wb, wo, g2, wgu, wd, gf, tm=256):
    t = x2d.shape[0]
    row = lambda i: (i, 0)
    const = lambda i: (0, 0)

    def wspec(w):
        return pl.BlockSpec(w.shape, const, pipeline_mode=pl.Buffered(1))

    return pl.pallas_call(
        _out_kernel,
        grid=(t // tm,),
        in_specs=[pl.BlockSpec((tm, D_MODEL), row),
                  pl.BlockSpec((tm, GDN_WIDTH), row),
                  pl.BlockSpec((tm, NSA_WIDTH), row),
                  pl.BlockSpec((tm, 2 * D_MODEL), row),
                  wspec(wa), wspec(wb), wspec(wo),
                  pl.BlockSpec((1, D_MODEL), const),
                  wspec(wgu), wspec(wd),
                  pl.BlockSpec((1, D_MODEL), const)],
        out_specs=pl.BlockSpec((tm, D_MODEL), row),
        out_shape=jax.ShapeDtypeStruct((t, D_MODEL), F32),
        compiler_params=pltpu.CompilerParams(dimension_semantics=("parallel",),
                                             vmem_limit_bytes=VMEM_LIMIT),
        name="merge_ffn",
    )(x2d, o_a, o_b, mg, wa, wb, wo, g2, wgu, wd, gf)


def _q_head_perm():
    idx = []
    for r in range(NSA_GROUP_HEADS):
        for g in range(NSA_KV_GROUPS):
            base = (g * NSA_GROUP_HEADS + r) * NSA_HEAD_DIM
            idx.extend(range(base, base + NSA_HEAD_DIM))
    return np.asarray(idx, dtype=np.int32)


def _reorder_w_in(w_in):
    sizes = (3 * GDN_WIDTH, GDN_WIDTH, GDN_HEADS, GDN_HEADS, NSA_WIDTH) + (NSA_KV_WIDTH,) * 6 + (
        3 * NSA_HEADS, D_MODEL, D_MODEL)
    offs = np.concatenate([[0], np.cumsum(sizes)])
    seg = lambda i: w_in[:, offs[i]:offs[i + 1]]
    qkv, z, beta, alpha, q, kc, vc, ks, vs, kw, vw, gate, ma, mb = [seg(i) for i in range(14)]
    pad = jnp.zeros((D_MODEL, LANES - 2 * GDN_HEADS - 3 * NSA_HEADS), w_in.dtype)
    q = q[:, _q_head_perm()]
    return jnp.concatenate([qkv, z, q, kc, vc, ks, vs, kw, vw, ma, mb, beta, alpha, gate, pad], axis=1).astype(BF16)


def _rope_tables(seq):
    half = ROPE_DIM // 2
    inv_freq = ROPE_THETA ** (-jnp.arange(half, dtype=F32) / half)
    ang = jnp.arange(seq, dtype=F32)[:, None] * inv_freq
    cos, sin = jnp.cos(ang), jnp.sin(ang)
    ones = jnp.ones((seq, NSA_HEAD_DIM - ROPE_DIM), F32)
    zeros = jnp.zeros((seq, NSA_HEAD_DIM - ROPE_DIM), F32)
    z8 = jnp.zeros((seq, half), F32)
    c = jnp.concatenate([cos, cos, ones], axis=1)
    s1 = jnp.concatenate([z8, sin, zeros], axis=1)
    s2 = jnp.concatenate([-sin, z8, zeros], axis=1)
    tile2 = lambda a: jnp.concatenate([a, a], axis=1)
    return tile2(c), tile2(s1), tile2(s2)


def _cmp_weights(pos, w1, w2):
    g = NSA_KV_GROUPS
    seg = CMP_BLOCK // 2
    eye = jnp.eye(g, dtype=w1.dtype)
    pos_h = pos.reshape(2, seg, 1, NSA_HEAD_DIM)
    pos_flat = jnp.broadcast_to(pos_h, (2, seg, g, NSA_HEAD_DIM)).reshape(2, seg * g * NSA_HEAD_DIM)
    w1_h = w1.reshape(2, seg, NSA_HEAD_DIM, CMP_HIDDEN)
    w1_bd = jnp.einsum('alds,gk->algdks', w1_h, eye).reshape(2, seg * g * NSA_HEAD_DIM, g * CMP_HIDDEN)
    w2_bd = jnp.einsum('hd,gk->ghkd', w2, eye).reshape(g * CMP_HIDDEN, g * NSA_HEAD_DIM)
    return pos_flat.astype(F32), w1_bd.astype(BF16), w2_bd.astype(BF16)


def _selection_constants(seq):
    n_cmp_pad = seq // CMP_STRIDE
    n_blk = seq // SEL_BLOCK
    cmp_start = np.arange(n_cmp_pad) * CMP_STRIDE
    cmp_end = cmp_start + CMP_BLOCK - 1
    sel_start = np.arange(n_blk) * SEL_BLOCK
    ovt = ((cmp_start[None, :] <= sel_start[:, None] + SEL_BLOCK - 1) & (cmp_end[None, :] >= sel_start[:, None]))
    ovt[:, n_cmp_pad - 1] = False
    n_kb = seq // KB
    key_blk = (np.arange(n_kb)[:, None] * KB + np.arange(KB)[None, :]) // SEL_BLOCK
    e_mat = key_blk[:, :, None] == np.arange(n_blk)[None, None, :]
    return jnp.asarray(ovt, BF16), jnp.asarray(e_mat, BF16)


def _lane_vector(values, offset):
    v = jnp.zeros((1, LANES), F32)
    return v.at[0, offset:offset + values.shape[0]].set(values.astype(F32))


def _hybrid_block(x, mix_norm_gain, w_in, gdn_conv_w, gdn_a_log, gdn_dt_bias, gdn_out_norm_gain,
                  cmp_pos_k, cmp_w1_k, cmp_w2_k, cmp_pos_v, cmp_w1_v, cmp_w2_v,
                  w_branch_gdn, w_branch_nsa, w_out, ffn_norm_gain, w_gate_up, w_down, final_norm_gain):
    batch, seq, _ = x.shape
    x2d = x.reshape(batch * seq, D_MODEL)
    rc, rs1, rs2 = _rope_tables(seq)
    qkv, z, q, kc, vc, kvs, mg, small = _inproj_call(
        x2d, mix_norm_gain.reshape(1, D_MODEL), _reorder_w_in(w_in), rc, rs1, rs2, seq)

    o_a = _gdn_call(qkv, z, small, gdn_conv_w.astype(F32),
                    _lane_vector(gdn_a_log, SMALL_ALPHA), _lane_vector(gdn_dt_bias, SMALL_ALPHA),
                    gdn_out_norm_gain.reshape(1, GDN_HEAD_DIM).astype(F32), batch, seq)

    seg_width = (CMP_BLOCK // 2) * NSA_KV_WIDTH
    kc3 = kc.reshape(batch, seq // CMP_STRIDE, seg_width)
    vc3 = vc.reshape(batch, seq // CMP_STRIDE, seg_width)
    kcc, vcc = _cmp_call(kc3, vc3, *_cmp_weights(cmp_pos_k, cmp_w1_k, cmp_w2_k),
                         *_cmp_weights(cmp_pos_v, cmp_w1_v, cmp_w2_v))
    ovt, e_mat = _selection_constants(seq)
    o_b = _nsa_call(q, kcc, vcc, kvs, small, ovt, e_mat, batch, seq)

    out = _out_call(x2d, o_a, o_b, mg,
                    w_branch_gdn.astype(BF16), w_branch_nsa[_q_head_perm(), :].astype(BF16), w_out.astype(BF16),
                    ffn_norm_gain.reshape(1, D_MODEL), w_gate_up.astype(BF16), w_down.astype(BF16),
                    final_norm_gain.reshape(1, D_MODEL))
    return out.reshape(batch, seq, D_MODEL)


def kernel(x, mix_norm_gain, w_in, gdn_conv_w, gdn_a_log, gdn_dt_bias, gdn_out_norm_gain, cmp_pos_k, cmp_w1_k,
           cmp_w2_k, cmp_pos_v, cmp_w1_v, cmp_w2_v, w_branch_gdn, w_branch_nsa, w_out, ffn_norm_gain, w_gate_up,
           w_down, final_norm_gain):
    assert mix_norm_gain.shape[0] == 1, "single-layer block"
    return _hybrid_block(x, mix_norm_gain[0], w_in[0], gdn_conv_w[0], gdn_a_log[0], gdn_dt_bias[0],
                         gdn_out_norm_gain[0], cmp_pos_k[0], cmp_w1_k[0], cmp_w2_k[0], cmp_pos_v[0], cmp_w1_v[0],
                         cmp_w2_v[0], w_branch_gdn[0], w_branch_nsa[0], w_out[0], ffn_norm_gain[0], w_gate_up[0],
                         w_down[0], final_norm_gain)
```

```python
import functools

import numpy as np
import jax
import jax.numpy as jnp
from jax import lax
from jax.experimental import pallas as pl
from jax.experimental.pallas import tpu as pltpu

F32 = jnp.float32
BF16 = jnp.bfloat16

D_MODEL = 1024
NORM_EPS = 1e-6
GDN_HEADS = 4
GDN_HEAD_DIM = 128
GDN_WIDTH = GDN_HEADS * GDN_HEAD_DIM
GDN_CONV = 4
GDN_CHUNK = 64
NSA_HEADS = 8
NSA_KV_GROUPS = 2
NSA_GROUP_HEADS = NSA_HEADS // NSA_KV_GROUPS
NSA_HEAD_DIM = 64
NSA_WIDTH = NSA_HEADS * NSA_HEAD_DIM
NSA_KV_WIDTH = NSA_KV_GROUPS * NSA_HEAD_DIM
CMP_BLOCK = 32
CMP_STRIDE = 16
CMP_HIDDEN = 128
SEL_BLOCK = 64
SEL_COUNT = 16
WINDOW = 512
ROPE_THETA = 500000.0
ROPE_DIM = NSA_HEAD_DIM // 4
FORCED_SCORE = 1000.0
NEG_INF = -1e30
FFN_HIDDEN = 2816

LANES = 128
VMEM_LIMIT = 56 * 1024 * 1024

C_QKV = (0, 1536)
C_Z = (1536, 2048)
C_Q = (2048, 2560)
C_KC = (2560, 2688)
C_VC = (2688, 2816)
C_KVS = (2816, 3456)
C_MG = (3456, 5504)
C_SMALL = (5504, 5632)
IN_WIDTH_PADDED = 5632
SMALL_BETA = 0
SMALL_ALPHA = 4
SMALL_GATE = 8

KV_KS = 0
KV_VS = 256
KV_KW = 512
KV_VW = 640
KV_WIDTH = 896
LOG2E = 1.4426950408889634
Q_SCALE = NSA_HEAD_DIM ** -0.5 * LOG2E

TQ = 128
KB = 512


def _dot(a, b):
    return jnp.dot(a, b, preferred_element_type=F32)


def _dot_nt(a, b):
    return lax.dot_general(a, b, (((1,), (1,)), ((), ())), preferred_element_type=F32)


def _dot_tn(a, b):
    return lax.dot_general(a, b, (((0,), (0,)), ((), ())), preferred_element_type=F32)


def _sigmoid(x):
    return 1.0 / (1.0 + jnp.exp(-x))


def _silu(x):
    return x * _sigmoid(x)


def _rms(x, gain):
    return x * lax.rsqrt(jnp.mean(x * x, axis=-1, keepdims=True) + NORM_EPS) * gain


def _rope_slab(y, c, s1, s2):
    return y * c + pltpu.roll(y, 8, 1) * s1 + pltpu.roll(y, LANES - 8, 1) * s2


def _inproj_kernel(x_ref, gain_ref, w_ref, rc_ref, rs1_ref, rs2_ref,
                   qkv_ref, z_ref, q_ref, kc_ref, vc_ref, kvs_ref, mg_ref, small_ref, *, seq):
    hb = _rms(x_ref[...], gain_ref[...]).astype(BF16)

    def proj(cols):
        return _dot(hb, w_ref[:, cols[0]:cols[1]])

    qkv_ref[...] = proj(C_QKV).astype(BF16)
    z_ref[...] = proj(C_Z).astype(BF16)
    mg_ref[...] = proj(C_MG).astype(BF16)
    small_ref[...] = proj(C_SMALL)
    vc_ref[...] = proj(C_VC).astype(BF16)

    c, s1, s2 = rc_ref[...], rs1_ref[...], rs2_ref[...]
    q = proj(C_Q)
    for r in range(NSA_GROUP_HEADS):
        sl = slice(r * LANES, (r + 1) * LANES)
        q_ref[:, sl] = (_rope_slab(q[:, sl], c, s1, s2) * Q_SCALE).astype(BF16)
    kc_ref[...] = _rope_slab(proj(C_KC), c, s1, s2).astype(BF16)

    tm = x_ref.shape[0]
    tok = (pl.program_id(0) % (seq // tm)) * tm + lax.broadcasted_iota(jnp.int32, (tm, 1), 0)
    lane = lax.broadcasted_iota(jnp.int32, (tm, LANES), 1)
    group0 = lane < NSA_HEAD_DIM
    block_onehot = jnp.where(lane == NSA_HEAD_DIM + tok // SEL_BLOCK, 1.0, 0.0)
    kvs = proj(C_KVS)
    for g in range(NSA_KV_GROUPS):
        ks = _rope_slab(kvs[:, g * LANES:(g + 1) * LANES], c, s1, s2)
        kvs_ref[:, KV_KS + g * LANES:KV_KS + (g + 1) * LANES] = jnp.where(group0, ks, block_onehot).astype(BF16)
    vs, kw, vw = kvs[:, 256:384], kvs[:, 384:512], kvs[:, 512:640]
    kvs_ref[:, KV_VS:KV_VS + LANES] = jnp.where(group0, vs, 1.0).astype(BF16)
    kvs_ref[:, KV_VS + LANES:KV_VS + 2 * LANES] = jnp.where(group0, 1.0, vs).astype(BF16)
    kvs_ref[:, KV_KW:KV_KW + LANES] = _rope_slab(kw, c, s1, s2).astype(BF16)
    kvs_ref[:, KV_VW:KV_VW + LANES] = jnp.where(group0, vw, 1.0).astype(BF16)
    kvs_ref[:, KV_VW + LANES:KV_VW + 2 * LANES] = jnp.where(group0, 1.0, vw).astype(BF16)


def _inproj_call(x2d, gain, w_r, rc, rs1, rs2, seq, tm=256):
    assert seq // SEL_BLOCK <= LANES, "selection-block one-hot must fit one lane tile"
    t = x2d.shape[0]
    n_seq_tiles = seq // tm
    row = lambda i: (i, 0)
    const = lambda i: (0, 0)
    tab = lambda i: (i % n_seq_tiles, 0)
    widths = [(1536, BF16), (512, BF16), (512, BF16), (128, BF16), (128, BF16), (KV_WIDTH, BF16), (2048, BF16),
              (128, F32)]
    return pl.pallas_call(
        functools.partial(_inproj_kernel, seq=seq),
        grid=(t // tm,),
        in_specs=[pl.BlockSpec((tm, D_MODEL), row),
                  pl.BlockSpec((1, D_MODEL), const),
                  pl.BlockSpec((D_MODEL, IN_WIDTH_PADDED), const, pipeline_mode=pl.Buffered(1)),
                  pl.BlockSpec((tm, LANES), tab),
                  pl.BlockSpec((tm, LANES), tab),
                  pl.BlockSpec((tm, LANES), tab)],
        out_specs=[pl.BlockSpec((tm, w), row) for w, _ in widths],
        out_shape=[jax.ShapeDtypeStruct((t, w), d) for w, d in widths],
        compiler_params=pltpu.CompilerParams(dimension_semantics=("parallel",),
                                             vmem_limit_bytes=VMEM_LIMIT),
        name="inproj",
    )(x2d, gain, w_r, rc, rs1, rs2)


def _gdn_kernel(qkv_ref, z_ref, small_ref, convw_ref, alog_ref, dtb_ref, ogain_ref, o_ref,
                xbuf, cbuf, state):
    cb = qkv_ref.shape[0]
    c = GDN_CHUNK
    n_chunks = cb // c
    heads = range(GDN_HEADS)
    j = pl.program_id(1)

    @pl.when(j == 0)
    def _():
        xbuf[0:8, :] = jnp.zeros((8, 3 * GDN_WIDTH), F32)
        state[...] = jnp.zeros_like(state)

    @pl.when(j > 0)
    def _():
        xbuf[0:8, :] = xbuf[cb:cb + 8, :]

    xbuf[8:8 + cb, :] = qkv_ref[...].astype(F32)
    w = convw_ref[...]
    acc = xbuf[8:8 + cb, :] * w[GDN_CONV - 1:GDN_CONV, :]
    for jj in range(GDN_CONV - 1):
        off = 8 - (GDN_CONV - 1) + jj
        acc = acc + xbuf[off:off + cb, :] * w[jj:jj + 1, :]
    cbuf[...] = _silu(acc)

    row = lax.broadcasted_iota(jnp.int32, (c, c), 0)
    col = lax.broadcasted_iota(jnp.int32, (c, c), 1)
    causal = row >= col
    strict = row > col
    tril = causal.astype(F32)
    eye = (row == col).astype(F32)
    neg_decay_rate = -jnp.exp(alog_ref[...])
    dtb = dtb_ref[...]
    ogain = ogain_ref[...]

    pairs = [(ci, h) for ci in range(n_chunks) for h in heads]
    beta_c, gc_c, gct_c = [], [], []
    for ci in range(n_chunks):
        sm = small_ref[ci * c:(ci + 1) * c, :]
        beta_c.append(_sigmoid(sm))
        xg = sm + dtb
        softplus = jnp.maximum(xg, 0.0) + jnp.log(1.0 + jnp.exp(-jnp.abs(xg)))
        gc_all = jnp.dot(tril, neg_decay_rate * softplus, precision=lax.Precision.HIGHEST,
                         preferred_element_type=F32)
        gc_c.append(gc_all)
        gct_c.append(gc_all.T)

    def head_cols(base, ci, h):
        lo = base + h * GDN_HEAD_DIM
        return cbuf[ci * c:(ci + 1) * c, lo:lo + GDN_HEAD_DIM]

    q_l, k_l, kb_l, vb_l, decay_l, eg_l, glast_l, gc_l = [], [], [], [], [], [], [], []
    for ci, h in pairs:
        q = head_cols(0, ci, h)
        k = head_cols(GDN_WIDTH, ci, h)
        v = head_cols(2 * GDN_WIDTH, ci, h)
        q = q * lax.rsqrt(jnp.sum(q * q, axis=-1, keepdims=True) + NORM_EPS) * (GDN_HEAD_DIM ** -0.5)
        k = k * lax.rsqrt(jnp.sum(k * k, axis=-1, keepdims=True) + NORM_EPS)
        beta = beta_c[ci][:, SMALL_BETA + h:SMALL_BETA + h + 1]
        gc = gc_c[ci][:, SMALL_ALPHA + h:SMALL_ALPHA + h + 1]
        gr = gct_c[ci][SMALL_ALPHA + h:SMALL_ALPHA + h + 1, :]
        decay_l.append(jnp.where(causal, jnp.exp(jnp.where(causal, gc - gr, 0.0)), 0.0))
        eg_l.append(jnp.exp(gc))
        glast_l.append(gc[c - 1:c, :])
        gc_l.append(gc)
        q_l.append(q)
        k_l.append(k)
        kb_l.append(k * beta)
        vb_l.append((v * beta).astype(BF16))
    kbf_l = [k.astype(BF16) for k in k_l]
    a_l = [jnp.where(strict, _dot_nt(kb.astype(BF16), kbf) * d, 0.0) for kb, kbf, d in zip(kb_l, kbf_l, decay_l)]
    t_l = [eye - a for a in a_l]
    p_l = a_l
    n = 2
    while n < c:
        p_l = [_dot(p.astype(BF16), p.astype(BF16)) for p in p_l]
        t_l = [t + _dot(t.astype(BF16), p.astype(BF16)) for t, p in zip(t_l, p_l)]
        n *= 2
    tb_l = [t.astype(BF16) for t in t_l]
    u_l = [_dot(t, vb) for t, vb in zip(tb_l, vb_l)]
    w_l = [_dot(t, (kb * eg).astype(BF16)).astype(BF16) for t, kb, eg in zip(tb_l, kb_l, eg_l)]
    qk_l = [(_dot_nt(q.astype(BF16), kbf) * d).astype(BF16) for q, kbf, d in zip(q_l, kbf_l, decay_l)]
    qg_l = [(q * eg).astype(BF16) for q, eg in zip(q_l, eg_l)]
    kd_l = [(k * jnp.exp(gl - gc)).astype(BF16) for k, gl, gc in zip(k_l, glast_l, gc_l)]

    s_l = [state[h] for h in heads]
    for ci in range(n_chunks):
        idx = [ci * GDN_HEADS + h for h in heads]
        sb_l = [s.astype(BF16) for s in s_l]
        vn_l = [(u_l[i] - _dot(w_l[i], sb)).astype(BF16) for i, sb in zip(idx, sb_l)]
        o_l = [_dot(qg_l[i], sb) + _dot(qk_l[i], vn) for i, sb, vn in zip(idx, sb_l, vn_l)]
        s_l = [s * jnp.exp(glast_l[i]) + _dot_tn(kd_l[i], vn) for i, s, vn in zip(idx, s_l, vn_l)]
        for h, o in zip(heads, o_l):
            hs = slice(h * GDN_HEAD_DIM, (h + 1) * GDN_HEAD_DIM)
            zh = z_ref[ci * c:(ci + 1) * c, hs].astype(F32)
            o = o * lax.rsqrt(jnp.mean(o * o, axis=-1, keepdims=True) + NORM_EPS) * ogain * _silu(zh)
            o_ref[ci * c:(ci + 1) * c, hs] = o.astype(BF16)
    for h in heads:
        state[h] = s_l[h]


def _gdn_call(qkv, z, small, conv_w, alog_v, dtb_v, ogain, batch, seq, cb=256):
    nj = seq // cb
    row = lambda b, j: (b * nj + j, 0)
    const = lambda b, j: (0, 0)
    return pl.pallas_call(
        _gdn_kernel,
        grid=(batch, nj),
        in_specs=[pl.BlockSpec((cb, 3 * GDN_WIDTH), row),
                  pl.BlockSpec((cb, GDN_WIDTH), row),
                  pl.BlockSpec((cb, LANES), row),
                  pl.BlockSpec((GDN_CONV, 3 * GDN_WIDTH), const),
                  pl.BlockSpec((1, LANES), const),
                  pl.BlockSpec((1, LANES), const),
                  pl.BlockSpec((1, GDN_HEAD_DIM), const)],
        out_specs=pl.BlockSpec((cb, GDN_WIDTH), row),
        out_shape=jax.ShapeDtypeStruct((batch * seq, GDN_WIDTH), BF16),
        scratch_shapes=[pltpu.VMEM((cb + 8, 3 * GDN_WIDTH), F32),
                        pltpu.VMEM((cb, 3 * GDN_WIDTH), F32),
                        pltpu.VMEM((GDN_HEADS, GDN_HEAD_DIM, GDN_HEAD_DIM), F32)],
        compiler_params=pltpu.CompilerParams(dimension_semantics=("parallel", "arbitrary"),
                                             vmem_limit_bytes=VMEM_LIMIT),
        name="gdn",
    )(qkv, z, small, conv_w, alog_v, dtb_v, ogain)


def _cmp_kernel(kc_ref, vc_ref, posk_ref, w1k_ref, w2k_ref, posv_ref, w1v_ref, w2v_ref, kco_ref, vco_ref):
    for x_ref, pos_ref, w1_ref, w2_ref, out_ref in ((kc_ref, posk_ref, w1k_ref, w2k_ref, kco_ref),
                                                    (vc_ref, posv_ref, w1v_ref, w2v_ref, vco_ref)):
        a = x_ref[0].astype(F32)
        n = a.shape[0]
        p0 = _dot((a + pos_ref[0:1, :]).astype(BF16), w1_ref[0])
        p1 = _dot((a + pos_ref[1:2, :]).astype(BF16), w1_ref[1])
        hid = _silu(p0 + pltpu.roll(p1, n - 1, 0))
        out_ref[0] = _dot(hid.astype(BF16), w2_ref[...]).astype(BF16)


def _cmp_call(kc3, vc3, posk, w1k, w2k, posv, w1v, w2v):
    b, n, width = kc3.shape
    hid2 = NSA_KV_GROUPS * CMP_HIDDEN
    per_b = lambda i: (i, 0, 0)
    c2 = lambda i: (0, 0)
    c3 = lambda i: (0, 0, 0)
    wspecs = [pl.BlockSpec((2, width), c2), pl.BlockSpec((2, width, hid2), c3), pl.BlockSpec((hid2, NSA_KV_WIDTH), c2)]
    return pl.pallas_call(
        _cmp_kernel,
        grid=(b,),
        in_specs=[pl.BlockSpec((1, n, width), per_b), pl.BlockSpec((1, n, width), per_b)] + wspecs + wspecs,
        out_specs=[pl.BlockSpec((1, n, NSA_KV_WIDTH), per_b)] * 2,
        out_shape=[jax.ShapeDtypeStruct((b, n, NSA_KV_WIDTH), BF16)] * 2,
        compiler_params=pltpu.CompilerParams(dimension_semantics=("parallel",),
                                             vmem_limit_bytes=VMEM_LIMIT),
        name="nsa_compress",
    )(kc3, vc3, posk, w1k, w2k, posv, w1v, w2v)


def _nsa_kernel(q_ref, kc_ref, vc_ref, kvs_ref, small_ref, ovt_ref, o_ref,
                qa_ref, m_ref, acc_ref, s_ref, part_ref):
    tq = q_ref.shape[0]
    rh = NSA_GROUP_HEADS
    n_cmp = kc_ref.shape[1]
    n_blk = ovt_ref.shape[0]
    qi = pl.program_id(1)
    s0 = qi * tq

    row_half = lax.broadcasted_iota(jnp.int32, (LANES, tq), 0) // NSA_HEAD_DIM
    t_row = s0 + lax.broadcasted_iota(jnp.int32, (1, tq), 1)
    k_col = lax.broadcasted_iota(jnp.int32, (KB, 1), 0)
    q_t = [q_ref[:, r * LANES:(r + 1) * LANES].astype(F32).T for r in range(rh)]
    gates_t = _sigmoid(small_ref[...]).T
    kc = kc_ref[0]
    vc = vc_ref[0]
    cmp_end = lax.broadcasted_iota(jnp.int32, (n_cmp, 1), 0) * CMP_STRIDE + (CMP_BLOCK - 1)
    vis1 = cmp_end <= t_row
    vis = jnp.concatenate([vis1] * rh, axis=1)

    blk = lax.broadcasted_iota(jnp.int32, (n_blk, tq), 0)
    cur = (s0 + lax.broadcasted_iota(jnp.int32, (n_blk, tq), 1)) // SEL_BLOCK
    valid = blk <= cur
    forced = (blk == 0) | (blk == cur) | (blk == cur - 1)

    def tile4(a):
        return jnp.concatenate([a] * rh, axis=1)

    def normalized(acc, g):
        l_row = acc[(1 - g) * NSA_HEAD_DIM:(1 - g) * NSA_HEAD_DIM + 1, :]
        return acc * (1.0 / l_row)

    groups = range(NSA_KV_GROUPS)
    qs_l, o_cmp_l, imp_l = [], [], []
    for g in groups:
        qs = jnp.concatenate([jnp.where(row_half == g, q_t[r], 0.0) for r in range(rh)], axis=1).astype(BF16)
        qs_l.append(qs)
        s = jnp.where(vis, _dot(kc, qs), NEG_INF)
        m = jnp.max(s, axis=0, keepdims=True)
        e = jnp.where(vis, jnp.exp2(s - m), 0.0)
        den = jnp.sum(e, axis=0, keepdims=True)
        p = e * (1.0 / jnp.where(den > 0.0, den, 1.0))
        o_cmp_l.append(_dot_tn(vc, p.astype(BF16)))
        p_sum = p[:, 0:tq]
        for r in range(1, rh):
            p_sum = p_sum + p[:, r * tq:(r + 1) * tq]
        p_hi = p_sum.astype(BF16)
        p_lo = (p_sum - p_hi.astype(F32)).astype(BF16)
        ovt = ovt_ref[...]
        imp_l.append(_dot(ovt, p_hi) + _dot(ovt, p_lo))

    span = WINDOW + TQ
    w0 = pl.multiple_of(jnp.maximum(s0 - WINDOW, 0), TQ)
    kw = kvs_ref[pl.ds(w0, span), KV_KW:KV_KW + LANES]
    rel = t_row - (w0 + lax.broadcasted_iota(jnp.int32, (span, 1), 0))
    win_bias = tile4(jnp.where((rel >= 0) & (rel < WINDOW), 0.0, NEG_INF))

    def gate_row(g, branch):
        cols = [SMALL_GATE + (g * rh + r) * 3 + branch for r in range(rh)]
        return jnp.concatenate([gates_t[c:c + 1, :] for c in cols], axis=1)

    for g in groups:
        vw = kvs_ref[pl.ds(w0, span), KV_VW + g * LANES:KV_VW + (g + 1) * LANES]
        sw = _dot(kw, qs_l[g]) + win_bias
        pw = jnp.exp2(sw - jnp.max(sw, axis=0, keepdims=True))
        o_win = normalized(_dot_tn(vw, pw.astype(BF16)), g)
        part_ref[g] = gate_row(g, 0) * o_cmp_l[g] + gate_row(g, 2) * o_win

    sub = lax.broadcasted_iota(jnp.int32, (8, tq), 0)
    n_slab = n_blk // 8
    for g in groups:
        score = jnp.where(valid, jnp.where(forced, FORCED_SCORE, imp_l[g]), -1.0)
        slabs = [score[8 * v:8 * (v + 1), :] for v in range(n_slab)]
        ranks = [jnp.zeros((8, tq), F32) for _ in range(n_slab)]
        for i in range(n_blk):
            vi, ri = divmod(i, 8)
            si = jnp.broadcast_to(score[i:i + 1, :], (8, tq))
            for v in range(n_slab):
                if v > vi:
                    hit = jnp.where(si >= slabs[v], 1.0, 0.0)
                elif v < vi:
                    hit = jnp.where(si > slabs[v], 1.0, 0.0)
                else:
                    hit = jnp.where(sub > ri, jnp.where(si >= slabs[v], 1.0, 0.0), jnp.where(si > slabs[v], 1.0, 0.0))
                ranks[v] = ranks[v] + hit
        rank = jnp.concatenate(ranks, axis=0)
        sel_bias = jnp.where((rank < float(SEL_COUNT)) & valid, 0.0, NEG_INF)
        if n_blk < NSA_HEAD_DIM:
            sel_bias = jnp.concatenate([sel_bias, jnp.zeros((NSA_HEAD_DIM - n_blk, tq), F32)], axis=0)
        q_g = qs_l[g][g * NSA_HEAD_DIM:(g + 1) * NSA_HEAD_DIM, :]
        qa_ref[g] = jnp.concatenate([q_g, tile4(sel_bias).astype(BF16)], axis=0)

    kb_diag = s0 // KB

    def scores(kb):
        k0 = pl.multiple_of(kb * KB, KB)
        return [_dot(kvs_ref[pl.ds(k0, KB), KV_KS + g * LANES:KV_KS + (g + 1) * LANES], qa_ref[g])
                for g in groups]

    def accumulate(kb, s_l, first):
        k0 = pl.multiple_of(kb * KB, KB)
        for g in groups:
            va = kvs_ref[pl.ds(k0, KB), KV_VS + g * LANES:KV_VS + (g + 1) * LANES]
            if first:
                m_new = jnp.max(s_l[g], axis=0, keepdims=True)
                acc_ref[g] = _dot_tn(va, jnp.exp2(s_l[g] - m_new).astype(BF16))
            else:
                m_old = m_ref[g]
                m_new = jnp.maximum(m_old, jnp.max(s_l[g], axis=0, keepdims=True))
                acc_ref[g] = (jnp.exp2(m_old - m_new) * acc_ref[g]
                              + _dot_tn(va, jnp.exp2(s_l[g] - m_new).astype(BF16)))
            m_ref[g] = m_new

    causal_bias = tile4(jnp.where(kb_diag * KB + k_col <= t_row, 0.0, NEG_INF))
    s_diag = [s + causal_bias for s in scores(kb_diag)]
    s_next = scores(jnp.maximum(kb_diag - 1, 0))
    accumulate(kb_diag, s_diag, True)
    for g in groups:
        s_ref[g] = s_next[g]

    def sel_body(i, carry):
        kb = kb_diag - i
        s_cur = [s_ref[g] for g in groups]
        s_nxt = scores(jnp.maximum(kb - 1, 0))
        accumulate(kb, s_cur, False)
        for g in groups:
            s_ref[g] = s_nxt[g]
        return carry

    lax.fori_loop(1, kb_diag + 1, sel_body, 0)

    out_t = [part_ref[g] + gate_row(g, 1) * normalized(acc_ref[g], g) for g in groups]
    for r in range(rh):
        cs = slice(r * tq, (r + 1) * tq)
        slab_t = jnp.where(row_half == 0, out_t[0][:, cs], out_t[1][:, cs])
        o_ref[:, r * LANES:(r + 1) * LANES] = slab_t.T.astype(BF16)


def _nsa_call(q, kc, vc, kvs, small, ovt, batch, seq):
    assert seq % KB == 0 and seq >= WINDOW + TQ and seq // SEL_BLOCK <= NSA_HEAD_DIM
    nq = seq // TQ
    n_cmp = kc.shape[1]
    row = lambda b, i: (b * nq + i, 0)
    per_b3 = lambda b, i: (b, 0, 0)
    per_b2 = lambda b, i: (b, 0)
    c2 = lambda b, i: (0, 0)
    rows = NSA_GROUP_HEADS * TQ
    return pl.pallas_call(
        _nsa_kernel,
        grid=(batch, nq),
        in_specs=[pl.BlockSpec((TQ, NSA_WIDTH), row),
                  pl.BlockSpec((1, n_cmp, NSA_KV_WIDTH), per_b3),
                  pl.BlockSpec((1, n_cmp, NSA_KV_WIDTH), per_b3),
                  pl.BlockSpec((seq, KV_WIDTH), per_b2),
                  pl.BlockSpec((TQ, LANES), row),
                  pl.BlockSpec(ovt.shape, c2)],
        out_specs=pl.BlockSpec((TQ, NSA_WIDTH), row),
        out_shape=jax.ShapeDtypeStruct((batch * seq, NSA_WIDTH), BF16),
        scratch_shapes=[pltpu.VMEM((NSA_KV_GROUPS, LANES, rows), BF16),
                        pltpu.VMEM((NSA_KV_GROUPS, 1, rows), F32),
                        pltpu.VMEM((NSA_KV_GROUPS, LANES, rows), F32),
                        pltpu.VMEM((NSA_KV_GROUPS, KB, rows), F32),
                        pltpu.VMEM((NSA_KV_GROUPS, LANES, rows), F32)],
        compiler_params=pltpu.CompilerParams(dimension_semantics=("parallel", "arbitrary"),
                                             vmem_limit_bytes=VMEM_LIMIT),
        name="nsa_attention",
    )(q, kc, vc, kvs, small, ovt)


def _out_kernel(x_ref, oa_ref, ob_ref, mg_ref, wa_ref, wb_ref, wo_ref, g2_ref, wgu_ref, wd_ref, gf_ref, out_ref):
    a = _dot(oa_ref[...], wa_ref[...])
    b = _dot(ob_ref[...], wb_ref[...])
    merged = (_sigmoid(mg_ref[:, 0:D_MODEL].astype(F32)) * a
              + _sigmoid(mg_ref[:, D_MODEL:2 * D_MODEL].astype(F32)) * b)
    x1 = x_ref[...] + _dot(merged.astype(BF16), wo_ref[...])
    h2 = _rms(x1, g2_ref[...]).astype(BF16)
    gate = _dot(h2, wgu_ref[:, 0:FFN_HIDDEN])
    up = _dot(h2, wgu_ref[:, FFN_HIDDEN:2 * FFN_HIDDEN])
    y = x1 + _dot((_silu(gate) * up).astype(BF16), wd_ref[...])
    out_ref[...] = _rms(y, gf_ref[...])


def _out_call(x2d, o_a, o_b, mg, wa, wb, wo, g2, wgu, wd, gf, tm=256):
    t = x2d.shape[0]
    row = lambda i: (i, 0)
    const = lambda i: (0, 0)

    def wspec(w):
        return pl.BlockSpec(w.shape, const, pipeline_mode=pl.Buffered(1))

    return pl.pallas_call(
        _out_kernel,
        grid=(t // tm,),
        in_specs=[pl.BlockSpec((tm, D_MODEL), row),
                  pl.BlockSpec((tm, GDN_WIDTH), row),
                  pl.BlockSpec((tm, NSA_WIDTH), row),
                  pl.BlockSpec((tm, 2 * D_MODEL), row),
                  wspec(wa), wspec(wb), wspec(wo),
                  pl.BlockSpec((1, D_MODEL), const),
                  wspec(wgu), wspec(wd),
                  pl.BlockSpec((1, D_MODEL), const)],
        out_specs=pl.BlockSpec((tm, D_MODEL), row),
        out_shape=jax.ShapeDtypeStruct((t, D_MODEL), F32),
        compiler_params=pltpu.CompilerParams(dimension_semantics=("parallel",),
                                             vmem_limit_bytes=VMEM_LIMIT),
        name="merge_ffn",
    )(x2d, o_a, o_b, mg, wa, wb, wo, g2, wgu, wd, gf)


def _q_head_perm():
    idx = []
    for r in range(NSA_GROUP_HEADS):
        for g in range(NSA_KV_GROUPS):
            base = (g * NSA_GROUP_HEADS + r) * NSA_HEAD_DIM
            idx.extend(range(base, base + NSA_HEAD_DIM))
    return np.asarray(idx, dtype=np.int32)


def _reorder_w_in(w_in):
    sizes = (3 * GDN_WIDTH, GDN_WIDTH, GDN_HEADS, GDN_HEADS, NSA_WIDTH) + (NSA_KV_WIDTH,) * 6 + (
        3 * NSA_HEADS, D_MODEL, D_MODEL)
    offs = np.concatenate([[0], np.cumsum(sizes)])
    seg = lambda i: w_in[:, offs[i]:offs[i + 1]]
    qkv, z, beta, alpha, q, kc, vc, ks, vs, kw, vw, gate, ma, mb = [seg(i) for i in range(14)]
    pad = jnp.zeros((D_MODEL, LANES - 2 * GDN_HEADS - 3 * NSA_HEADS), w_in.dtype)
    q = q[:, _q_head_perm()]
    zero_half = jnp.zeros((D_MODEL, NSA_HEAD_DIM), w_in.dtype)
    ks_split = [ks[:, :NSA_HEAD_DIM], zero_half, ks[:, NSA_HEAD_DIM:], zero_half]
    return jnp.concatenate([qkv, z, q, kc, vc] + ks_split + [vs, kw, vw, ma, mb, beta, alpha, gate, pad],
                           axis=1).astype(BF16)


def _rope_tables(seq):
    half = ROPE_DIM // 2
    inv_freq = ROPE_THETA ** (-jnp.arange(half, dtype=F32) / half)
    ang = jnp.arange(seq, dtype=F32)[:, None] * inv_freq
    cos, sin = jnp.cos(ang), jnp.sin(ang)
    ones = jnp.ones((seq, NSA_HEAD_DIM - ROPE_DIM), F32)
    zeros = jnp.zeros((seq, NSA_HEAD_DIM - ROPE_DIM), F32)
    z8 = jnp.zeros((seq, half), F32)
    c = jnp.concatenate([cos, cos, ones], axis=1)
    s1 = jnp.concatenate([z8, sin, zeros], axis=1)
    s2 = jnp.concatenate([-sin, z8, zeros], axis=1)
    tile2 = lambda a: jnp.concatenate([a, a], axis=1)
    return tile2(c), tile2(s1), tile2(s2)


def _cmp_weights(pos, w1, w2):
    g = NSA_KV_GROUPS
    seg = CMP_BLOCK // 2
    eye = jnp.eye(g, dtype=w1.dtype)
    pos_h = pos.reshape(2, seg, 1, NSA_HEAD_DIM)
    pos_flat = jnp.broadcast_to(pos_h, (2, seg, g, NSA_HEAD_DIM)).reshape(2, seg * g * NSA_HEAD_DIM)
    w1_h = w1.reshape(2, seg, NSA_HEAD_DIM, CMP_HIDDEN)
    w1_bd = jnp.einsum('alds,gk->algdks', w1_h, eye).reshape(2, seg * g * NSA_HEAD_DIM, g * CMP_HIDDEN)
    w2_bd = jnp.einsum('hd,gk->ghkd', w2, eye).reshape(g * CMP_HIDDEN, g * NSA_HEAD_DIM)
    return pos_flat.astype(F32), w1_bd.astype(BF16), w2_bd.astype(BF16)


def _selection_constants(seq):
    n_cmp_pad = seq // CMP_STRIDE
    n_blk = seq // SEL_BLOCK
    cmp_start = np.arange(n_cmp_pad) * CMP_STRIDE
    cmp_end = cmp_start + CMP_BLOCK - 1
    sel_start = np.arange(n_blk) * SEL_BLOCK
    ovt = ((cmp_start[None, :] <= sel_start[:, None] + SEL_BLOCK - 1) & (cmp_end[None, :] >= sel_start[:, None]))
    ovt[:, n_cmp_pad - 1] = False
    return jnp.asarray(ovt, BF16)


def _lane_vector(values, offset):
    v = jnp.zeros((1, LANES), F32)
    return v.at[0, offset:offset + values.shape[0]].set(values.astype(F32))


def _hybrid_block(x, mix_norm_gain, w_in, gdn_conv_w, gdn_a_log, gdn_dt_bias, gdn_out_norm_gain,
                  cmp_pos_k, cmp_w1_k, cmp_w2_k, cmp_pos_v, cmp_w1_v, cmp_w2_v,
                  w_branch_gdn, w_branch_nsa, w_out, ffn_norm_gain, w_gate_up, w_down, final_norm_gain):
    batch, seq, _ = x.shape
    x2d = x.reshape(batch * seq, D_MODEL)
    rc, rs1, rs2 = _rope_tables(seq)
    qkv, z, q, kc, vc, kvs, mg, small = _inproj_call(
        x2d, mix_norm_gain.reshape(1, D_MODEL), _reorder_w_in(w_in), rc, rs1, rs2, seq)

    o_a = _gdn_call(qkv, z, small, gdn_conv_w.astype(F32),
                    _lane_vector(gdn_a_log, SMALL_ALPHA), _lane_vector(gdn_dt_bias, SMALL_ALPHA),
                    gdn_out_norm_gain.reshape(1, GDN_HEAD_DIM).astype(F32), batch, seq)

    seg_width = (CMP_BLOCK // 2) * NSA_KV_WIDTH
    kc3 = kc.reshape(batch, seq // CMP_STRIDE, seg_width)
    vc3 = vc.reshape(batch, seq // CMP_STRIDE, seg_width)
    kcc, vcc = _cmp_call(kc3, vc3, *_cmp_weights(cmp_pos_k, cmp_w1_k, cmp_w2_k),
                         *_cmp_weights(cmp_pos_v, cmp_w1_v, cmp_w2_v))
    o_b = _nsa_call(q, kcc, vcc, kvs, small, _selection_constants(seq), batch, seq)

    out = _out_call(x2d, o_a, o_b, mg,
                    w_branch_gdn.astype(BF16), w_branch_nsa[_q_head_perm(), :].astype(BF16), w_out.astype(BF16),
                    ffn_norm_gain.reshape(1, D_MODEL), w_gate_up.astype(BF16), w_down.astype(BF16),
                    final_norm_gain.reshape(1, D_MODEL))
    return out.reshape(batch, seq, D_MODEL)


def kernel(x, mix_norm_gain, w_in, gdn_conv_w, gdn_a_log, gdn_dt_bias, gdn_out_norm_gain, cmp_pos_k, cmp_w1_k,
           cmp_w2_k, cmp_pos_v, cmp_w1_v, cmp_w2_v, w_branch_gdn, w_branch_nsa, w_out, ffn_norm_gain, w_gate_up,
           w_down, final_norm_gain):
    assert mix_norm_gain.shape[0] == 1, "single-layer block"
    return _hybrid_block(x, mix_norm_gain[0], w_in[0], gdn_conv_w[0], gdn_a_log[0], gdn_dt_bias[0],
                         gdn_out_norm_gain[0], cmp_pos_k[0], cmp_w1_k[0], cmp_w2_k[0], cmp_pos_v[0], cmp_w1_v[0],
                         cmp_w2_v[0], w_branch_gdn[0], w_branch_nsa[0], w_out[0], ffn_norm_gain[0], w_gate_up[0],
                         w_down[0], final_norm_gain)
```

```python
import functools

import numpy as np
import jax
import jax.numpy as jnp
from jax import lax
from jax.experimental import pallas as pl
from jax.experimental.pallas import tpu as pltpu

F32 = jnp.float32
BF16 = jnp.bfloat16

D_MODEL = 1024
NORM_EPS = 1e-6
GDN_HEADS = 4
GDN_HEAD_DIM = 128
GDN_WIDTH = GDN_HEADS * GDN_HEAD_DIM
GDN_CONV = 4
GDN_CHUNK = 64
NSA_HEADS = 8
NSA_KV_GROUPS = 2
NSA_GROUP_HEADS = NSA_HEADS // NSA_KV_GROUPS
NSA_HEAD_DIM = 64
NSA_WIDTH = NSA_HEADS * NSA_HEAD_DIM
NSA_KV_WIDTH = NSA_KV_GROUPS * NSA_HEAD_DIM
CMP_BLOCK = 32
CMP_STRIDE = 16
CMP_HIDDEN = 128
SEL_BLOCK = 64
SEL_COUNT = 16
WINDOW = 512
ROPE_THETA = 500000.0
ROPE_DIM = NSA_HEAD_DIM // 4
FORCED_SCORE = 1000.0
NEG_INF = -1e30
FFN_HIDDEN = 2816

LANES = 128
VMEM_LIMIT = 56 * 1024 * 1024

C_QKV = (0, 1536)
C_Z = (1536, 2048)
C_Q = (2048, 2560)
C_KC = (2560, 2688)
C_VC = (2688, 2816)
C_KVS = (2816, 3456)
C_MG = (3456, 5504)
C_SMALL = (5504, 5632)
IN_WIDTH_PADDED = 5632
SMALL_BETA = 0
SMALL_ALPHA = 4
SMALL_GATE = 8

KV_KS = 0
KV_VS = 256
KV_KW = 512
KV_VW = 640
KV_WIDTH = 896
LOG2E = 1.4426950408889634
Q_SCALE = NSA_HEAD_DIM ** -0.5 * LOG2E

TQ = 128
KB = 512


def _dot(a, b):
    return jnp.dot(a, b, preferred_element_type=F32)


def _dot_nt(a, b):
    return lax.dot_general(a, b, (((1,), (1,)), ((), ())), preferred_element_type=F32)


def _dot_tn(a, b):
    return lax.dot_general(a, b, (((0,), (0,)), ((), ())), preferred_element_type=F32)


def _sigmoid(x):
    return 1.0 / (1.0 + jnp.exp(-x))


def _silu(x):
    return x * _sigmoid(x)


def _rms(x, gain):
    return x * lax.rsqrt(jnp.mean(x * x, axis=-1, keepdims=True) + NORM_EPS) * gain


def _rope_slab(y, c, s1, s2):
    return y * c + pltpu.roll(y, 8, 1) * s1 + pltpu.roll(y, LANES - 8, 1) * s2


def _inproj_kernel(x_ref, gain_ref, w_ref, rc_ref, rs1_ref, rs2_ref, convw_ref,
                   qkv_ref, z_ref, q_ref, kc_ref, vc_ref, kvs_ref, mg_ref, small_ref, xbuf, *, seq):
    tm = x_ref.shape[0]
    hb = _rms(x_ref[...], gain_ref[...]).astype(BF16)

    def proj(cols):
        return _dot(hb, w_ref[:, cols[0]:cols[1]])

    first_tile = pl.program_id(0) % (seq // tm) == 0

    @pl.when(first_tile)
    def _():
        xbuf[0:8, :] = jnp.zeros((8, 3 * GDN_WIDTH), F32)

    @pl.when(jnp.logical_not(first_tile))
    def _():
        xbuf[0:8, :] = xbuf[tm:tm + 8, :]

    xbuf[8:8 + tm, :] = proj(C_QKV)
    cw = convw_ref[...]
    conv = xbuf[8:8 + tm, :] * cw[GDN_CONV - 1:GDN_CONV, :]
    for jj in range(GDN_CONV - 1):
        off = 8 - (GDN_CONV - 1) + jj
        conv = conv + xbuf[off:off + tm, :] * cw[jj:jj + 1, :]
    act = _silu(conv)
    for h in range(2 * GDN_HEADS):
        sl = slice(h * GDN_HEAD_DIM, (h + 1) * GDN_HEAD_DIM)
        a = act[:, sl]
        a = a * lax.rsqrt(jnp.sum(a * a, axis=-1, keepdims=True) + NORM_EPS)
        qkv_ref[:, sl] = (a * (GDN_HEAD_DIM ** -0.5) if h < GDN_HEADS else a).astype(BF16)
    qkv_ref[:, 2 * GDN_WIDTH:3 * GDN_WIDTH] = act[:, 2 * GDN_WIDTH:3 * GDN_WIDTH].astype(BF16)

    z_ref[...] = proj(C_Z).astype(BF16)
    mg_ref[...] = proj(C_MG).astype(BF16)
    small_ref[...] = proj(C_SMALL)
    vc_ref[...] = proj(C_VC).astype(BF16)

    c, s1, s2 = rc_ref[...], rs1_ref[...], rs2_ref[...]
    q = proj(C_Q)
    for r in range(NSA_GROUP_HEADS):
        sl = slice(r * LANES, (r + 1) * LANES)
        q_ref[:, sl] = (_rope_slab(q[:, sl], c, s1, s2) * Q_SCALE).astype(BF16)
    kc_ref[...] = _rope_slab(proj(C_KC), c, s1, s2).astype(BF16)

    tok = (pl.program_id(0) % (seq // tm)) * tm + lax.broadcasted_iota(jnp.int32, (tm, 1), 0)
    lane = lax.broadcasted_iota(jnp.int32, (tm, LANES), 1)
    group0 = lane < NSA_HEAD_DIM
    block_onehot = jnp.where(lane == NSA_HEAD_DIM + tok // SEL_BLOCK, 1.0, 0.0)
    kvs = proj(C_KVS)
    for g in range(NSA_KV_GROUPS):
        ks = _rope_slab(kvs[:, g * LANES:(g + 1) * LANES], c, s1, s2)
        kvs_ref[:, KV_KS + g * LANES:KV_KS + (g + 1) * LANES] = jnp.where(group0, ks, block_onehot).astype(BF16)
    vs, kw, vw = kvs[:, 256:384], kvs[:, 384:512], kvs[:, 512:640]
    kvs_ref[:, KV_VS:KV_VS + LANES] = jnp.where(group0, vs, 1.0).astype(BF16)
    kvs_ref[:, KV_VS + LANES:KV_VS + 2 * LANES] = jnp.where(group0, 1.0, vs).astype(BF16)
    kvs_ref[:, KV_KW:KV_KW + LANES] = _rope_slab(kw, c, s1, s2).astype(BF16)
    kvs_ref[:, KV_VW:KV_VW + LANES] = jnp.where(group0, vw, 1.0).astype(BF16)
    kvs_ref[:, KV_VW + LANES:KV_VW + 2 * LANES] = jnp.where(group0, 1.0, vw).astype(BF16)


def _inproj_call(x2d, gain, w_r, rc, rs1, rs2, conv_w, seq, tm=256):
    assert seq // SEL_BLOCK <= LANES, "selection-block one-hot must fit one lane tile"
    t = x2d.shape[0]
    n_seq_tiles = seq // tm
    row = lambda i: (i, 0)
    const = lambda i: (0, 0)
    tab = lambda i: (i % n_seq_tiles, 0)
    widths = [(1536, BF16), (512, BF16), (512, BF16), (128, BF16), (128, BF16), (KV_WIDTH, BF16), (2048, BF16),
              (128, F32)]
    return pl.pallas_call(
        functools.partial(_inproj_kernel, seq=seq),
        grid=(t // tm,),
        in_specs=[pl.BlockSpec((tm, D_MODEL), row),
                  pl.BlockSpec((1, D_MODEL), const),
                  pl.BlockSpec((D_MODEL, IN_WIDTH_PADDED), const, pipeline_mode=pl.Buffered(1)),
                  pl.BlockSpec((tm, LANES), tab),
                  pl.BlockSpec((tm, LANES), tab),
                  pl.BlockSpec((tm, LANES), tab),
                  pl.BlockSpec((GDN_CONV, 3 * GDN_WIDTH), const)],
        out_specs=[pl.BlockSpec((tm, w), row) for w, _ in widths],
        out_shape=[jax.ShapeDtypeStruct((t, w), d) for w, d in widths],
        scratch_shapes=[pltpu.VMEM((tm + 8, 3 * GDN_WIDTH), F32)],
        compiler_params=pltpu.CompilerParams(dimension_semantics=("arbitrary",),
                                             vmem_limit_bytes=VMEM_LIMIT),
        name="inproj",
    )(x2d, gain, w_r, rc, rs1, rs2, conv_w)


def _gdn_kernel(qkv_ref, z_ref, small_ref, alog_ref, dtb_ref, ogain_ref, o_ref, state):
    cb = qkv_ref.shape[0]
    c = GDN_CHUNK
    n_chunks = cb // c
    heads = range(GDN_HEADS)
    j = pl.program_id(1)

    @pl.when(j == 0)
    def _():
        state[...] = jnp.zeros_like(state)

    row = lax.broadcasted_iota(jnp.int32, (c, c), 0)
    col = lax.broadcasted_iota(jnp.int32, (c, c), 1)
    causal = row >= col
    strict = row > col
    tril = causal.astype(F32)
    eye = (row == col).astype(F32)
    neg_decay_rate = -jnp.exp(alog_ref[...])
    dtb = dtb_ref[...]
    ogain = ogain_ref[...]

    pairs = [(ci, h) for ci in range(n_chunks) for h in heads]
    beta_c, gc_c, gct_c = [], [], []
    for ci in range(n_chunks):
        sm = small_ref[ci * c:(ci + 1) * c, :]
        beta_c.append(_sigmoid(sm))
        xg = sm + dtb
        softplus = jnp.maximum(xg, 0.0) + jnp.log(1.0 + jnp.exp(-jnp.abs(xg)))
        gc_all = jnp.dot(tril, neg_decay_rate * softplus, precision=lax.Precision.HIGHEST,
                         preferred_element_type=F32)
        gc_c.append(gc_all)
        gct_c.append(gc_all.T)

    def head_cols(base, ci, h):
        lo = base + h * GDN_HEAD_DIM
        return qkv_ref[ci * c:(ci + 1) * c, lo:lo + GDN_HEAD_DIM].astype(F32)

    q_l, k_l, kb_l, vb_l, decay_l, eg_l, glast_l, gc_l = [], [], [], [], [], [], [], []
    for ci, h in pairs:
        q = head_cols(0, ci, h)
        k = head_cols(GDN_WIDTH, ci, h)
        v = head_cols(2 * GDN_WIDTH, ci, h)
        beta = beta_c[ci][:, SMALL_BETA + h:SMALL_BETA + h + 1]
        gc = gc_c[ci][:, SMALL_ALPHA + h:SMALL_ALPHA + h + 1]
        gr = gct_c[ci][SMALL_ALPHA + h:SMALL_ALPHA + h + 1, :]
        decay_l.append(jnp.where(causal, jnp.exp(jnp.where(causal, gc - gr, 0.0)), 0.0))
        eg_l.append(jnp.exp(gc))
        glast_l.append(gc[c - 1:c, :])
        gc_l.append(gc)
        q_l.append(q)
        k_l.append(k)
        kb_l.append(k * beta)
        vb_l.append((v * beta).astype(BF16))
    kbf_l = [k.astype(BF16) for k in k_l]
    a_l = [jnp.where(strict, _dot_nt(kb.astype(BF16), kbf) * d, 0.0) for kb, kbf, d in zip(kb_l, kbf_l, decay_l)]
    t_l = [eye - a for a in a_l]
    p_l = a_l
    n = 2
    while n < c:
        p_l = [_dot(p.astype(BF16), p.astype(BF16)) for p in p_l]
        t_l = [t + _dot(t.astype(BF16), p.astype(BF16)) for t, p in zip(t_l, p_l)]
        n *= 2
    tb_l = [t.astype(BF16) for t in t_l]
    u_l = [_dot(t, vb) for t, vb in zip(tb_l, vb_l)]
    w_l = [_dot(t, (kb * eg).astype(BF16)).astype(BF16) for t, kb, eg in zip(tb_l, kb_l, eg_l)]
    qk_l = [(_dot_nt(q.astype(BF16), kbf) * d).astype(BF16) for q, kbf, d in zip(q_l, kbf_l, decay_l)]
    qg_l = [(q * eg).astype(BF16) for q, eg in zip(q_l, eg_l)]
    kd_l = [(k * jnp.exp(gl - gc)).astype(BF16) for k, gl, gc in zip(k_l, glast_l, gc_l)]

    s_l = [state[h] for h in heads]
    for ci in range(n_chunks):
        idx = [ci * GDN_HEADS + h for h in heads]
        sb_l = [s.astype(BF16) for s in s_l]
        vn_l = [(u_l[i] - _dot(w_l[i], sb)).astype(BF16) for i, sb in zip(idx, sb_l)]
        o_l = [_dot(qg_l[i], sb) + _dot(qk_l[i], vn) for i, sb, vn in zip(idx, sb_l, vn_l)]
        s_l = [s * jnp.exp(glast_l[i]) + _dot_tn(kd_l[i], vn) for i, s, vn in zip(idx, s_l, vn_l)]
        for h, o in zip(heads, o_l):
            hs = slice(h * GDN_HEAD_DIM, (h + 1) * GDN_HEAD_DIM)
            zh = z_ref[ci * c:(ci + 1) * c, hs].astype(F32)
            o = o * lax.rsqrt(jnp.mean(o * o, axis=-1, keepdims=True) + NORM_EPS) * ogain * _silu(zh)
            o_ref[ci * c:(ci + 1) * c, hs] = o.astype(BF16)
    for h in heads:
        state[h] = s_l[h]


def _gdn_call(qkv, z, small, alog_v, dtb_v, ogain, batch, seq, cb=256):
    nj = seq // cb
    row = lambda b, j: (b * nj + j, 0)
    const = lambda b, j: (0, 0)
    return pl.pallas_call(
        _gdn_kernel,
        grid=(batch, nj),
        in_specs=[pl.BlockSpec((cb, 3 * GDN_WIDTH), row),
                  pl.BlockSpec((cb, GDN_WIDTH), row),
                  pl.BlockSpec((cb, LANES), row),
                  pl.BlockSpec((1, LANES), const),
                  pl.BlockSpec((1, LANES), const),
                  pl.BlockSpec((1, GDN_HEAD_DIM), const)],
        out_specs=pl.BlockSpec((cb, GDN_WIDTH), row),
        out_shape=jax.ShapeDtypeStruct((batch * seq, GDN_WIDTH), BF16),
        scratch_shapes=[pltpu.VMEM((GDN_HEADS, GDN_HEAD_DIM, GDN_HEAD_DIM), F32)],
        compiler_params=pltpu.CompilerParams(dimension_semantics=("parallel", "arbitrary"),
                                             vmem_limit_bytes=VMEM_LIMIT),
        name="gdn",
    )(qkv, z, small, alog_v, dtb_v, ogain)


def _cmp_kernel(kc_ref, vc_ref, posk_ref, w1k_ref, w2k_ref, posv_ref, w1v_ref, w2v_ref, kco_ref, vco_ref):
    for x_ref, pos_ref, w1_ref, w2_ref, out_ref in ((kc_ref, posk_ref, w1k_ref, w2k_ref, kco_ref),
                                                    (vc_ref, posv_ref, w1v_ref, w2v_ref, vco_ref)):
        a = x_ref[0].astype(F32)
        n = a.shape[0]
        p0 = _dot((a + pos_ref[0:1, :]).astype(BF16), w1_ref[0])
        p1 = _dot((a + pos_ref[1:2, :]).astype(BF16), w1_ref[1])
        hid = _silu(p0 + pltpu.roll(p1, n - 1, 0))
        out_ref[0] = _dot(hid.astype(BF16), w2_ref[...]).astype(BF16)


def _cmp_call(kc3, vc3, posk, w1k, w2k, posv, w1v, w2v):
    b, n, width = kc3.shape
    hid2 = NSA_KV_GROUPS * CMP_HIDDEN
    per_b = lambda i: (i, 0, 0)
    c2 = lambda i: (0, 0)
    c3 = lambda i: (0, 0, 0)
    wspecs = [pl.BlockSpec((2, width), c2), pl.BlockSpec((2, width, hid2), c3), pl.BlockSpec((hid2, NSA_KV_WIDTH), c2)]
    return pl.pallas_call(
        _cmp_kernel,
        grid=(b,),
        in_specs=[pl.BlockSpec((1, n, width), per_b), pl.BlockSpec((1, n, width), per_b)] + wspecs + wspecs,
        out_specs=[pl.BlockSpec((1, n, NSA_KV_WIDTH), per_b)] * 2,
        out_shape=[jax.ShapeDtypeStruct((b, n, NSA_KV_WIDTH), BF16)] * 2,
        compiler_params=pltpu.CompilerParams(dimension_semantics=("parallel",),
                                             vmem_limit_bytes=VMEM_LIMIT),
        name="nsa_compress",
    )(kc3, vc3, posk, w1k, w2k, posv, w1v, w2v)


def _nsa_kernel(q_ref, kc_ref, vc_ref, kvs_ref, small_ref, ovt_ref, o_ref,
                qa_ref, m_ref, acc_ref, s_ref, part_ref):
    tq = q_ref.shape[0]
    rh = NSA_GROUP_HEADS
    n_cmp = kc_ref.shape[1]
    n_blk = ovt_ref.shape[0]
    qi = pl.program_id(1)
    s0 = qi * tq

    row_half = lax.broadcasted_iota(jnp.int32, (LANES, tq), 0) // NSA_HEAD_DIM
    t_row = s0 + lax.broadcasted_iota(jnp.int32, (1, tq), 1)
    k_col = lax.broadcasted_iota(jnp.int32, (KB, 1), 0)
    q_t = [q_ref[:, r * LANES:(r + 1) * LANES].astype(F32).T for r in range(rh)]
    gates_t = _sigmoid(small_ref[...]).T
    kc = kc_ref[0]
    vc = vc_ref[0]
    cmp_end = lax.broadcasted_iota(jnp.int32, (n_cmp, 1), 0) * CMP_STRIDE + (CMP_BLOCK - 1)
    vis1 = cmp_end <= t_row
    vis = jnp.concatenate([vis1] * rh, axis=1)

    blk = lax.broadcasted_iota(jnp.int32, (n_blk, tq), 0)
    cur = (s0 + lax.broadcasted_iota(jnp.int32, (n_blk, tq), 1)) // SEL_BLOCK
    valid = blk <= cur
    forced = (blk == 0) | (blk == cur) | (blk == cur - 1)

    def tile4(a):
        return jnp.concatenate([a] * rh, axis=1)

    def normalized(acc, g):
        l_row = acc[(1 - g) * NSA_HEAD_DIM:(1 - g) * NSA_HEAD_DIM + 1, :]
        return acc * (1.0 / l_row)

    groups = range(NSA_KV_GROUPS)
    qs_l, o_cmp_l, imp_l = [], [], []
    for g in groups:
        qs = jnp.concatenate([jnp.where(row_half == g, q_t[r], 0.0) for r in range(rh)], axis=1).astype(BF16)
        qs_l.append(qs)
        s = jnp.where(vis, _dot(kc, qs), NEG_INF)
        m = jnp.max(s, axis=0, keepdims=True)
        e = jnp.where(vis, jnp.exp2(s - m), 0.0)
        den = jnp.sum(e, axis=0, keepdims=True)
        p = e * (1.0 / jnp.where(den > 0.0, den, 1.0))
        o_cmp_l.append(_dot_tn(vc, p.astype(BF16)))
        p_sum = p[:, 0:tq]
        for r in range(1, rh):
            p_sum = p_sum + p[:, r * tq:(r + 1) * tq]
        p_hi = p_sum.astype(BF16)
        p_lo = (p_sum - p_hi.astype(F32)).astype(BF16)
        ovt = ovt_ref[...]
        imp_l.append(_dot(ovt, p_hi) + _dot(ovt, p_lo))

    span = WINDOW + TQ
    w0 = pl.multiple_of(jnp.maximum(s0 - WINDOW, 0), TQ)
    kw = kvs_ref[pl.ds(w0, span), KV_KW:KV_KW + LANES]
    rel = t_row - (w0 + lax.broadcasted_iota(jnp.int32, (span, 1), 0))
    win_bias = tile4(jnp.where((rel >= 0) & (rel < WINDOW), 0.0, NEG_INF))

    def gate_row(g, branch):
        cols = [SMALL_GATE + (g * rh + r) * 3 + branch for r in range(rh)]
        return jnp.concatenate([gates_t[c:c + 1, :] for c in cols], axis=1)

    for g in groups:
        vw = kvs_ref[pl.ds(w0, span), KV_VW + g * LANES:KV_VW + (g + 1) * LANES]
        sw = _dot(kw, qs_l[g]) + win_bias
        pw = jnp.exp2(sw - jnp.max(sw, axis=0, keepdims=True))
        o_win = normalized(_dot_tn(vw, pw.astype(BF16)), g)
        part_ref[g] = gate_row(g, 0) * o_cmp_l[g] + gate_row(g, 2) * o_win

    sub = lax.broadcasted_iota(jnp.int32, (8, tq), 0)
    n_slab = n_blk // 8
    for g in groups:
        score = jnp.where(valid, jnp.where(forced, FORCED_SCORE, imp_l[g]), -1.0)
        slabs = [score[8 * v:8 * (v + 1), :] for v in range(n_slab)]
        ranks = [jnp.zeros((8, tq), F32) for _ in range(n_slab)]
        for i in range(n_blk):
            vi, ri = divmod(i, 8)
            si = jnp.broadcast_to(score[i:i + 1, :], (8, tq))
            for v in range(n_slab):
                if v > vi:
                    hit = jnp.where(si >= slabs[v], 1.0, 0.0)
                elif v < vi:
                    hit = jnp.where(si > slabs[v], 1.0, 0.0)
                else:
                    hit = jnp.where(sub > ri, jnp.where(si >= slabs[v], 1.0, 0.0), jnp.where(si > slabs[v], 1.0, 0.0))
                ranks[v] = ranks[v] + hit
        rank = jnp.concatenate(ranks, axis=0)
        sel_bias = jnp.where((rank < float(SEL_COUNT)) & valid, 0.0, NEG_INF)
        if n_blk < NSA_HEAD_DIM:
            sel_bias = jnp.concatenate([sel_bias, jnp.zeros((NSA_HEAD_DIM - n_blk, tq), F32)], axis=0)
        q_g = qs_l[g][g * NSA_HEAD_DIM:(g + 1) * NSA_HEAD_DIM, :]
        qa_ref[g] = jnp.concatenate([q_g, tile4(sel_bias).astype(BF16)], axis=0)

    kb_diag = s0 // KB

    def scores(kb):
        k0 = pl.multiple_of(kb * KB, KB)
        return [_dot(kvs_ref[pl.ds(k0, KB), KV_KS + g * LANES:KV_KS + (g + 1) * LANES], qa_ref[g])
                for g in groups]

    def accumulate(kb, s_l, first):
        k0 = pl.multiple_of(kb * KB, KB)
        for g in groups:
            va = kvs_ref[pl.ds(k0, KB), KV_VS + g * LANES:KV_VS + (g + 1) * LANES]
            if first:
                m_new = jnp.max(s_l[g], axis=0, keepdims=True)
                acc_ref[g] = _dot_tn(va, jnp.exp2(s_l[g] - m_new).astype(BF16))
            else:
                m_old = m_ref[g]
                m_new = jnp.maximum(m_old, jnp.max(s_l[g], axis=0, keepdims=True))
                acc_ref[g] = (jnp.exp2(m_old - m_new) * acc_ref[g]
                              + _dot_tn(va, jnp.exp2(s_l[g] - m_new).astype(BF16)))
            m_ref[g] = m_new

    causal_bias = tile4(jnp.where(kb_diag * KB + k_col <= t_row, 0.0, NEG_INF))
    s_diag = [s + causal_bias for s in scores(kb_diag)]
    s_next = scores(jnp.maximum(kb_diag - 1, 0))
    accumulate(kb_diag, s_diag, True)
    for g in groups:
        s_ref[g] = s_next[g]

    def sel_body(i, carry):
        kb = kb_diag - i
        s_cur = [s_ref[g] for g in groups]
        s_nxt = scores(jnp.maximum(kb - 1, 0))
        accumulate(kb, s_cur, False)
        for g in groups:
            s_ref[g] = s_nxt[g]
        return carry

    lax.fori_loop(1, kb_diag + 1, sel_body, 0)

    out_t = [part_ref[g] + gate_row(g, 1) * normalized(acc_ref[g], g) for g in groups]
    for r in range(rh):
        cs = slice(r * tq, (r + 1) * tq)
        slab_t = jnp.where(row_half == 0, out_t[0][:, cs], out_t[1][:, cs])
        o_ref[:, r * LANES:(r + 1) * LANES] = slab_t.T.astype(BF16)


def _nsa_call(q, kc, vc, kvs, small, ovt, batch, seq):
    assert seq % KB == 0 and seq >= WINDOW + TQ and seq // SEL_BLOCK <= NSA_HEAD_DIM
    nq = seq // TQ
    n_cmp = kc.shape[1]
    row = lambda b, i: (b * nq + i, 0)
    per_b3 = lambda b, i: (b, 0, 0)
    per_b2 = lambda b, i: (b, 0)
    c2 = lambda b, i: (0, 0)
    rows = NSA_GROUP_HEADS * TQ
    return pl.pallas_call(
        _nsa_kernel,
        grid=(batch, nq),
        in_specs=[pl.BlockSpec((TQ, NSA_WIDTH), row),
                  pl.BlockSpec((1, n_cmp, NSA_KV_WIDTH), per_b3),
                  pl.BlockSpec((1, n_cmp, NSA_KV_WIDTH), per_b3),
                  pl.BlockSpec((seq, KV_WIDTH), per_b2),
                  pl.BlockSpec((TQ, LANES), row),
                  pl.BlockSpec(ovt.shape, c2)],
        out_specs=pl.BlockSpec((TQ, NSA_WIDTH), row),
        out_shape=jax.ShapeDtypeStruct((batch * seq, NSA_WIDTH), BF16),
        scratch_shapes=[pltpu.VMEM((NSA_KV_GROUPS, LANES, rows), BF16),
                        pltpu.VMEM((NSA_KV_GROUPS, 1, rows), F32),
                        pltpu.VMEM((NSA_KV_GROUPS, LANES, rows), F32),
                        pltpu.VMEM((NSA_KV_GROUPS, KB, rows), F32),
                        pltpu.VMEM((NSA_KV_GROUPS, LANES, rows), F32)],
        compiler_params=pltpu.CompilerParams(dimension_semantics=("parallel", "arbitrary"),
                                             vmem_limit_bytes=VMEM_LIMIT),
        name="nsa_attention",
    )(q, kc, vc, kvs, small, ovt)


def _out_kernel(x_ref, oa_ref, ob_ref, mg_ref, wa_ref, wb_ref, wo_ref, g2_ref, wgu_ref, wd_ref, gf_ref, out_ref):
    a = _dot(oa_ref[...], wa_ref[...])
    b = _dot(ob_ref[...], wb_ref[...])
    merged = (_sigmoid(mg_ref[:, 0:D_MODEL].astype(F32)) * a
              + _sigmoid(mg_ref[:, D_MODEL:2 * D_MODEL].astype(F32)) * b)
    x1 = x_ref[...] + _dot(merged.astype(BF16), wo_ref[...])
    h2 = _rms(x1, g2_ref[...]).astype(BF16)
    gate = _dot(h2, wgu_ref[:, 0:FFN_HIDDEN])
    up = _dot(h2, wgu_ref[:, FFN_HIDDEN:2 * FFN_HIDDEN])
    y = x1 + _dot((_silu(gate) * up).astype(BF16), wd_ref[...])
    out_ref[...] = _rms(y, gf_ref[...])


def _out_call(x2d, o_a, o_b, mg, wa, wb, wo, g2, wgu, wd, gf, tm=256):
    t = x2d.shape[0]
    row = lambda i: (i, 0)
    const = lambda i: (0, 0)

    def wspec(w):
        return pl.BlockSpec(w.shape, const, pipeline_mode=pl.Buffered(1))

    return pl.pallas_call(
        _out_kernel,
        grid=(t // tm,),
        in_specs=[pl.BlockSpec((tm, D_MODEL), row),
                  pl.BlockSpec((tm, GDN_WIDTH), row),
                  pl.BlockSpec((tm, NSA_WIDTH), row),
                  pl.BlockSpec((tm, 2 * D_MODEL), row),
                  wspec(wa), wspec(wb), wspec(wo),
                  pl.BlockSpec((1, D_MODEL), const),
                  wspec(wgu), wspec(wd),
                  pl.BlockSpec((1, D_MODEL), const)],
        out_specs=pl.BlockSpec((tm, D_MODEL), row),
        out_shape=jax.ShapeDtypeStruct((t, D_MODEL), F32),
        compiler_params=pltpu.CompilerParams(dimension_semantics=("parallel",),
                                             vmem_limit_bytes=VMEM_LIMIT),
        name="merge_ffn",
    )(x2d, o_a, o_b, mg, wa, wb, wo, g2, wgu, wd, gf)


def _q_head_perm():
    idx = []
    for r in range(NSA_GROUP_HEADS):
        for g in range(NSA_KV_GROUPS):
            base = (g * NSA_GROUP_HEADS + r) * NSA_HEAD_DIM
            idx.extend(range(base, base + NSA_HEAD_DIM))
    return np.asarray(idx, dtype=np.int32)


def _reorder_w_in(w_in):
    sizes = (3 * GDN_WIDTH, GDN_WIDTH, GDN_HEADS, GDN_HEADS, NSA_WIDTH) + (NSA_KV_WIDTH,) * 6 + (
        3 * NSA_HEADS, D_MODEL, D_MODEL)
    offs = np.concatenate([[0], np.cumsum(sizes)])
    seg = lambda i: w_in[:, offs[i]:offs[i + 1]]
    qkv, z, beta, alpha, q, kc, vc, ks, vs, kw, vw, gate, ma, mb = [seg(i) for i in range(14)]
    pad = jnp.zeros((D_MODEL, LANES - 2 * GDN_HEADS - 3 * NSA_HEADS), w_in.dtype)
    q = q[:, _q_head_perm()]
    zero_half = jnp.zeros((D_MODEL, NSA_HEAD_DIM), w_in.dtype)
    ks_split = [ks[:, :NSA_HEAD_DIM], zero_half, ks[:, NSA_HEAD_DIM:], zero_half]
    return jnp.concatenate([qkv, z, q, kc, vc] + ks_split + [vs, kw, vw, ma, mb, beta, alpha, gate, pad],
                           axis=1).astype(BF16)


def _rope_tables(seq):
    half = ROPE_DIM // 2
    inv_freq = ROPE_THETA ** (-jnp.arange(half, dtype=F32) / half)
    ang = jnp.arange(seq, dtype=F32)[:, None] * inv_freq
    cos, sin = jnp.cos(ang), jnp.sin(ang)
    ones = jnp.ones((seq, NSA_HEAD_DIM - ROPE_DIM), F32)
    zeros = jnp.zeros((seq, NSA_HEAD_DIM - ROPE_DIM), F32)
    z8 = jnp.zeros((seq, half), F32)
    c = jnp.concatenate([cos, cos, ones], axis=1)
    s1 = jnp.concatenate([z8, sin, zeros], axis=1)
    s2 = jnp.concatenate([-sin, z8, zeros], axis=1)
    tile2 = lambda a: jnp.concatenate([a, a], axis=1)
    return tile2(c), tile2(s1), tile2(s2)


def _cmp_weights(pos, w1, w2):
    g = NSA_KV_GROUPS
    seg = CMP_BLOCK // 2
    eye = jnp.eye(g, dtype=w1.dtype)
    pos_h = pos.reshape(2, seg, 1, NSA_HEAD_DIM)
    pos_flat = jnp.broadcast_to(pos_h, (2, seg, g, NSA_HEAD_DIM)).reshape(2, seg * g * NSA_HEAD_DIM)
    w1_h = w1.reshape(2, seg, NSA_HEAD_DIM, CMP_HIDDEN)
    w1_bd = jnp.einsum('alds,gk->algdks', w1_h, eye).reshape(2, seg * g * NSA_HEAD_DIM, g * CMP_HIDDEN)
    w2_bd = jnp.einsum('hd,gk->ghkd', w2, eye).reshape(g * CMP_HIDDEN, g * NSA_HEAD_DIM)
    return pos_flat.astype(F32), w1_bd.astype(BF16), w2_bd.astype(BF16)


def _selection_constants(seq):
    n_cmp_pad = seq // CMP_STRIDE
    n_blk = seq // SEL_BLOCK
    cmp_start = np.arange(n_cmp_pad) * CMP_STRIDE
    cmp_end = cmp_start + CMP_BLOCK - 1
    sel_start = np.arange(n_blk) * SEL_BLOCK
    ovt = ((cmp_start[None, :] <= sel_start[:, None] + SEL_BLOCK - 1) & (cmp_end[None, :] >= sel_start[:, None]))
    ovt[:, n_cmp_pad - 1] = False
    return jnp.asarray(ovt, BF16)


def _lane_vector(values, offset):
    v = jnp.zeros((1, LANES), F32)
    return v.at[0, offset:offset + values.shape[0]].set(values.astype(F32))


def _hybrid_block(x, mix_norm_gain, w_in, gdn_conv_w, gdn_a_log, gdn_dt_bias, gdn_out_norm_gain,
                  cmp_pos_k, cmp_w1_k, cmp_w2_k, cmp_pos_v, cmp_w1_v, cmp_w2_v,
                  w_branch_gdn, w_branch_nsa, w_out, ffn_norm_gain, w_gate_up, w_down, final_norm_gain):
    batch, seq, _ = x.shape
    x2d = x.reshape(batch * seq, D_MODEL)
    rc, rs1, rs2 = _rope_tables(seq)
    qkv, z, q, kc, vc, kvs, mg, small = _inproj_call(
        x2d, mix_norm_gain.reshape(1, D_MODEL), _reorder_w_in(w_in), rc, rs1, rs2,
        gdn_conv_w.astype(F32), seq)

    o_a = _gdn_call(qkv, z, small,
                    _lane_vector(gdn_a_log, SMALL_ALPHA), _lane_vector(gdn_dt_bias, SMALL_ALPHA),
                    gdn_out_norm_gain.reshape(1, GDN_HEAD_DIM).astype(F32), batch, seq)

    seg_width = (CMP_BLOCK // 2) * NSA_KV_WIDTH
    kc3 = kc.reshape(batch, seq // CMP_STRIDE, seg_width)
    vc3 = vc.reshape(batch, seq // CMP_STRIDE, seg_width)
    kcc, vcc = _cmp_call(kc3, vc3, *_cmp_weights(cmp_pos_k, cmp_w1_k, cmp_w2_k),
                         *_cmp_weights(cmp_pos_v, cmp_w1_v, cmp_w2_v))
    o_b = _nsa_call(q, kcc, vcc, kvs, small, _selection_constants(seq), batch, seq)

    out = _out_call(x2d, o_a, o_b, mg,
                    w_branch_gdn.astype(BF16), w_branch_nsa[_q_head_perm(), :].astype(BF16), w_out.astype(BF16),
                    ffn_norm_gain.reshape(1, D_MODEL), w_gate_up.astype(BF16), w_down.astype(BF16),
                    final_norm_gain.reshape(1, D_MODEL))
    return out.reshape(batch, seq, D_MODEL)


def kernel(x, mix_norm_gain, w_in, gdn_conv_w, gdn_a_log, gdn_dt_bias, gdn_out_norm_gain, cmp_pos_k, cmp_w1_k,
           cmp_w2_k, cmp_pos_v, cmp_w1_v, cmp_w2_v, w_branch_gdn, w_branch_nsa, w_out, ffn_norm_gain, w_gate_up,
           w_down, final_norm_gain):
    assert mix_norm_gain.shape[0] == 1, "single-layer block"
    return _hybrid_block(x, mix_norm_gain[0], w_in[0], gdn_conv_w[0], gdn_a_log[0], gdn_dt_bias[0],
                         gdn_out_norm_gain[0], cmp_pos_k[0], cmp_w1_k[0], cmp_w2_k[0], cmp_pos_v[0], cmp_w1_v[0],
                         cmp_w2_v[0], w_branch_gdn[0], w_branch_nsa[0], w_out[0], ffn_norm_gain[0], w_gate_up[0],
                         w_down[0], final_norm_gain)
```

```python
import functools

import numpy as np
import jax
import jax.numpy as jnp
from jax import lax
from jax.experimental import pallas as pl
from jax.experimental.pallas import tpu as pltpu

F32 = jnp.float32
BF16 = jnp.bfloat16

D_MODEL = 1024
NORM_EPS = 1e-6
GDN_HEADS = 4
GDN_HEAD_DIM = 128
GDN_WIDTH = GDN_HEADS * GDN_HEAD_DIM
GDN_CONV = 4
GDN_CHUNK = 64
NSA_HEADS = 8
NSA_KV_GROUPS = 2
NSA_GROUP_HEADS = NSA_HEADS // NSA_KV_GROUPS
NSA_HEAD_DIM = 64
NSA_WIDTH = NSA_HEADS * NSA_HEAD_DIM
NSA_KV_WIDTH = NSA_KV_GROUPS * NSA_HEAD_DIM
CMP_BLOCK = 32
CMP_STRIDE = 16
CMP_HIDDEN = 128
SEL_BLOCK = 64
SEL_COUNT = 16
WINDOW = 512
ROPE_THETA = 500000.0
ROPE_DIM = NSA_HEAD_DIM // 4
FORCED_SCORE = 1000.0
NEG_INF = -1e30
FFN_HIDDEN = 2816

LANES = 128
VMEM_LIMIT = 56 * 1024 * 1024

C_QKV = (0, 1536)
C_Z = (1536, 2048)
C_Q = (2048, 2560)
C_KC = (2560, 2688)
C_VC = (2688, 2816)
C_KVS = (2816, 3456)
C_MG = (3456, 5504)
C_SMALL = (5504, 5632)
IN_WIDTH_PADDED = 5632
SMALL_BETA = 0
SMALL_ALPHA = 4
SMALL_GATE = 8

KV_KS = 0
KV_VS = 256
KV_KW = 512
KV_VW = 640
KV_WIDTH = 896
LOG2E = 1.4426950408889634
Q_SCALE = NSA_HEAD_DIM ** -0.5 * LOG2E

TQ = 128
KB = 512


def _dot(a, b):
    return jnp.dot(a, b, preferred_element_type=F32)


def _dot_nt(a, b):
    return lax.dot_general(a, b, (((1,), (1,)), ((), ())), preferred_element_type=F32)


def _dot_tn(a, b):
    return lax.dot_general(a, b, (((0,), (0,)), ((), ())), preferred_element_type=F32)


def _sigmoid(x):
    return 1.0 / (1.0 + jnp.exp(-x))


def _silu(x):
    return x * _sigmoid(x)


def _rms(x, gain):
    return x * lax.rsqrt(jnp.mean(x * x, axis=-1, keepdims=True) + NORM_EPS) * gain


def _rope_slab(y, c, s1, s2):
    return y * c + pltpu.roll(y, 8, 1) * s1 + pltpu.roll(y, LANES - 8, 1) * s2


def _inproj_kernel(x_ref, gain_ref, w_ref, rc_ref, rs1_ref, rs2_ref, convw_ref,
                   qkv_ref, z_ref, q_ref, kc_ref, vc_ref, kvs_ref, mg_ref, small_ref, xbuf, *, seq):
    tm = x_ref.shape[0]
    hb = _rms(x_ref[...], gain_ref[...]).astype(BF16)

    def proj(cols):
        return _dot(hb, w_ref[:, cols[0]:cols[1]])

    first_tile = pl.program_id(0) % (seq // tm) == 0

    @pl.when(first_tile)
    def _():
        xbuf[0:8, :] = jnp.zeros((8, 3 * GDN_WIDTH), F32)

    @pl.when(jnp.logical_not(first_tile))
    def _():
        xbuf[0:8, :] = xbuf[tm:tm + 8, :]

    xbuf[8:8 + tm, :] = proj(C_QKV)
    cw = convw_ref[...]
    conv = xbuf[8:8 + tm, :] * cw[GDN_CONV - 1:GDN_CONV, :]
    for jj in range(GDN_CONV - 1):
        off = 8 - (GDN_CONV - 1) + jj
        conv = conv + xbuf[off:off + tm, :] * cw[jj:jj + 1, :]
    act = _silu(conv)
    for h in range(2 * GDN_HEADS):
        sl = slice(h * GDN_HEAD_DIM, (h + 1) * GDN_HEAD_DIM)
        a = act[:, sl]
        a = a * lax.rsqrt(jnp.sum(a * a, axis=-1, keepdims=True) + NORM_EPS)
        qkv_ref[:, sl] = (a * (GDN_HEAD_DIM ** -0.5) if h < GDN_HEADS else a).astype(BF16)
    qkv_ref[:, 2 * GDN_WIDTH:3 * GDN_WIDTH] = act[:, 2 * GDN_WIDTH:3 * GDN_WIDTH].astype(BF16)

    z_ref[...] = proj(C_Z).astype(BF16)
    mg_ref[...] = proj(C_MG).astype(BF16)
    small_ref[...] = proj(C_SMALL)
    vc_ref[...] = proj(C_VC).astype(BF16)

    c, s1, s2 = rc_ref[...], rs1_ref[...], rs2_ref[...]
    q = proj(C_Q)
    for r in range(NSA_GROUP_HEADS):
        sl = slice(r * LANES, (r + 1) * LANES)
        q_ref[:, sl] = (_rope_slab(q[:, sl], c, s1, s2) * Q_SCALE).astype(BF16)
    kc_ref[...] = _rope_slab(proj(C_KC), c, s1, s2).astype(BF16)

    tok = (pl.program_id(0) % (seq // tm)) * tm + lax.broadcasted_iota(jnp.int32, (tm, 1), 0)
    lane = lax.broadcasted_iota(jnp.int32, (tm, LANES), 1)
    group0 = lane < NSA_HEAD_DIM
    block_onehot = jnp.where(lane == NSA_HEAD_DIM + tok // SEL_BLOCK, 1.0, 0.0)
    kvs = proj(C_KVS)
    for g in range(NSA_KV_GROUPS):
        ks = _rope_slab(kvs[:, g * LANES:(g + 1) * LANES], c, s1, s2)
        kvs_ref[:, KV_KS + g * LANES:KV_KS + (g + 1) * LANES] = jnp.where(group0, ks, block_onehot).astype(BF16)
    vs, kw, vw = kvs[:, 256:384], kvs[:, 384:512], kvs[:, 512:640]
    kvs_ref[:, KV_VS:KV_VS + LANES] = jnp.where(group0, vs, 1.0).astype(BF16)
    kvs_ref[:, KV_VS + LANES:KV_VS + 2 * LANES] = jnp.where(group0, 1.0, vs).astype(BF16)
    kvs_ref[:, KV_KW:KV_KW + LANES] = _rope_slab(kw, c, s1, s2).astype(BF16)
    kvs_ref[:, KV_VW:KV_VW + LANES] = jnp.where(group0, vw, 1.0).astype(BF16)
    kvs_ref[:, KV_VW + LANES:KV_VW + 2 * LANES] = jnp.where(group0, 1.0, vw).astype(BF16)


def _inproj_call(x2d, gain, w_r, rc, rs1, rs2, conv_w, seq, tm=256):
    assert seq // SEL_BLOCK <= LANES, "selection-block one-hot must fit one lane tile"
    t = x2d.shape[0]
    n_seq_tiles = seq // tm
    row = lambda i: (i, 0)
    const = lambda i: (0, 0)
    tab = lambda i: (i % n_seq_tiles, 0)
    widths = [(1536, BF16), (512, BF16), (512, BF16), (128, BF16), (128, BF16), (KV_WIDTH, BF16), (2048, BF16),
              (128, F32)]
    return pl.pallas_call(
        functools.partial(_inproj_kernel, seq=seq),
        grid=(t // tm,),
        in_specs=[pl.BlockSpec((tm, D_MODEL), row),
                  pl.BlockSpec((1, D_MODEL), const),
                  pl.BlockSpec((D_MODEL, IN_WIDTH_PADDED), const, pipeline_mode=pl.Buffered(1)),
                  pl.BlockSpec((tm, LANES), tab),
                  pl.BlockSpec((tm, LANES), tab),
                  pl.BlockSpec((tm, LANES), tab),
                  pl.BlockSpec((GDN_CONV, 3 * GDN_WIDTH), const)],
        out_specs=[pl.BlockSpec((tm, w), row) for w, _ in widths],
        out_shape=[jax.ShapeDtypeStruct((t, w), d) for w, d in widths],
        scratch_shapes=[pltpu.VMEM((tm + 8, 3 * GDN_WIDTH), F32)],
        compiler_params=pltpu.CompilerParams(dimension_semantics=("arbitrary",),
                                             vmem_limit_bytes=VMEM_LIMIT),
        name="inproj",
    )(x2d, gain, w_r, rc, rs1, rs2, conv_w)


def _gdn_kernel(qkv_ref, z_ref, small_ref, alog_ref, dtb_ref, ogain_ref, o_ref,
                state, u_ref, w_ref, qk_ref, qg_ref, kd_ref, gl_ref, *, blocks_per_seq):
    t = pl.program_id(0)
    bufs = (u_ref, w_ref, qk_ref, qg_ref, kd_ref, gl_ref)

    @pl.when(t == 0)
    def _():
        state[...] = jnp.zeros_like(state)
        for ref in bufs:
            ref[...] = jnp.zeros_like(ref)

    fresh = (t + blocks_per_seq - 1) % blocks_per_seq == 0
    for slot in range(2):
        @pl.when(t % 2 == slot)
        def _(slot=slot):
            _gdn_step(qkv_ref, z_ref, small_ref, alog_ref, dtb_ref, ogain_ref, o_ref, state, bufs, slot, fresh)


def _gdn_step(qkv_ref, z_ref, small_ref, alog_ref, dtb_ref, ogain_ref, o_ref, state, bufs, cur, fresh):
    u_ref, w_ref, qk_ref, qg_ref, kd_ref, gl_ref = bufs
    prev = 1 - cur
    cb = qkv_ref.shape[0]
    c = GDN_CHUNK
    n_chunks = cb // c
    heads = range(GDN_HEADS)

    row = lax.broadcasted_iota(jnp.int32, (c, c), 0)
    col = lax.broadcasted_iota(jnp.int32, (c, c), 1)
    causal = row >= col
    strict = row > col
    tril = causal.astype(F32)
    eye = (row == col).astype(F32)
    neg_decay_rate = -jnp.exp(alog_ref[...])
    dtb = dtb_ref[...]
    ogain = ogain_ref[...]

    pairs = [(ci, h) for ci in range(n_chunks) for h in heads]
    n_pairs = len(pairs)
    A = {}

    def a_prepare():
        beta_c, gc_c, gct_c = [], [], []
        for ci in range(n_chunks):
            sm = small_ref[ci * c:(ci + 1) * c, :]
            beta_c.append(_sigmoid(sm))
            xg = sm + dtb
            softplus = jnp.maximum(xg, 0.0) + jnp.log(1.0 + jnp.exp(-jnp.abs(xg)))
            gc_all = jnp.dot(tril, neg_decay_rate * softplus, precision=lax.Precision.HIGHEST,
                             preferred_element_type=F32)
            gc_c.append(gc_all)
            gct_c.append(gc_all.T)

        def head_cols(base, ci, h):
            lo = base + h * GDN_HEAD_DIM
            return qkv_ref[ci * c:(ci + 1) * c, lo:lo + GDN_HEAD_DIM].astype(F32)

        A["qb"], A["kbf"], A["vb"], A["kbeg"], A["decay"], a_l = [], [], [], [], [], []
        for i, (ci, h) in enumerate(pairs):
            q = head_cols(0, ci, h)
            k = head_cols(GDN_WIDTH, ci, h)
            v = head_cols(2 * GDN_WIDTH, ci, h)
            beta = beta_c[ci][:, SMALL_BETA + h:SMALL_BETA + h + 1]
            gc = gc_c[ci][:, SMALL_ALPHA + h:SMALL_ALPHA + h + 1]
            gr = gct_c[ci][SMALL_ALPHA + h:SMALL_ALPHA + h + 1, :]
            decay = jnp.where(causal, jnp.exp(jnp.where(causal, gc - gr, 0.0)), 0.0)
            eg = jnp.exp(gc)
            g_last = gc[c - 1:c, :]
            kb = k * beta
            kbf = k.astype(BF16)
            qg_ref[cur, i] = (q * eg).astype(BF16)
            kd_ref[cur, i] = (k * jnp.exp(g_last - gc)).astype(BF16)
            gl_ref[cur, i] = jnp.broadcast_to(jnp.exp(g_last), gl_ref.shape[2:])
            A["qb"].append(q.astype(BF16))
            A["kbf"].append(kbf)
            A["vb"].append((v * beta).astype(BF16))
            A["kbeg"].append((kb * eg).astype(BF16))
            A["decay"].append(decay)
            a_l.append(jnp.where(strict, _dot_nt(kb.astype(BF16), kbf) * decay, 0.0))
        A["t"] = [eye - a for a in a_l]
        A["p"] = a_l

    def a_qk():
        for i in range(n_pairs):
            qk_ref[cur, i] = (_dot_nt(A["qb"][i], A["kbf"][i]) * A["decay"][i]).astype(BF16)

    def a_square():
        A["p"] = [_dot(p.astype(BF16), p.astype(BF16)) for p in A["p"]]

    def a_extend():
        A["t"] = [t_ + _dot(t_.astype(BF16), p.astype(BF16)) for t_, p in zip(A["t"], A["p"])]

    def a_u():
        A["tb"] = [t_.astype(BF16) for t_ in A["t"]]
        for i in range(n_pairs):
            u_ref[cur, i] = _dot(A["tb"][i], A["vb"][i])

    def a_w():
        for i in range(n_pairs):
            w_ref[cur, i] = _dot(A["tb"][i], A["kbeg"][i]).astype(BF16)

    n_doublings = c.bit_length() - 2
    a_stages = [a_prepare, a_qk] + [a_square, a_extend] * n_doublings + [a_u, a_w]

    B = {"s": [jnp.where(fresh, 0.0, state[h]) for h in heads]}

    def b_new_values(ci):
        idx = [ci * GDN_HEADS + h for h in heads]
        B["sb"] = [s.astype(BF16) for s in B["s"]]
        B["vn"] = [(u_ref[prev, i] - _dot(w_ref[prev, i], sb)).astype(BF16) for i, sb in zip(idx, B["sb"])]

    def b_output(ci):
        idx = [ci * GDN_HEADS + h for h in heads]
        o_l = [_dot(qg_ref[prev, i], sb) + _dot(qk_ref[prev, i], vn) for i, sb, vn in zip(idx, B["sb"], B["vn"])]
        B["s"] = [s * gl_ref[prev, i][0:1, :] + _dot_tn(kd_ref[prev, i], vn)
                  for i, s, vn in zip(idx, B["s"], B["vn"])]
        for h, o in zip(heads, o_l):
            hs = slice(h * GDN_HEAD_DIM, (h + 1) * GDN_HEAD_DIM)
            zh = z_ref[ci * c:(ci + 1) * c, hs].astype(F32)
            o = o * lax.rsqrt(jnp.mean(o * o, axis=-1, keepdims=True) + NORM_EPS) * ogain * _silu(zh)
            o_ref[ci * c:(ci + 1) * c, hs] = o.astype(BF16)

    b_stages = []
    for ci in range(n_chunks):
        b_stages += [functools.partial(b_new_values, ci), functools.partial(b_output, ci)]

    emitted_b = 0
    for k, stage in enumerate(a_stages):
        stage()
        while emitted_b < len(b_stages) and emitted_b * len(a_stages) < (k + 1) * len(b_stages):
            b_stages[emitted_b]()
            emitted_b += 1
    for h in heads:
        state[h] = B["s"][h]


def _gdn_call(qkv, z, small, alog_v, dtb_v, ogain, batch, seq, cb=256):
    blocks_per_seq = seq // cb
    n_blocks = batch * blocks_per_seq
    n_pairs = (cb // GDN_CHUNK) * GDN_HEADS
    c, d = GDN_CHUNK, GDN_HEAD_DIM
    phase_a_block = lambda t: (jnp.minimum(t, n_blocks - 1), 0)
    phase_b_block = lambda t: (jnp.maximum(t - 1, 0), 0)
    const = lambda t: (0, 0)
    return pl.pallas_call(
        functools.partial(_gdn_kernel, blocks_per_seq=blocks_per_seq),
        grid=(n_blocks + 1,),
        in_specs=[pl.BlockSpec((cb, 3 * GDN_WIDTH), phase_a_block),
                  pl.BlockSpec((cb, GDN_WIDTH), phase_b_block),
                  pl.BlockSpec((cb, LANES), phase_a_block),
                  pl.BlockSpec((1, LANES), const),
                  pl.BlockSpec((1, LANES), const),
                  pl.BlockSpec((1, GDN_HEAD_DIM), const)],
        out_specs=pl.BlockSpec((cb, GDN_WIDTH), phase_b_block),
        out_shape=jax.ShapeDtypeStruct((batch * seq, GDN_WIDTH), BF16),
        scratch_shapes=[pltpu.VMEM((GDN_HEADS, d, d), F32),
                        pltpu.VMEM((2, n_pairs, c, d), F32),
                        pltpu.VMEM((2, n_pairs, c, d), BF16),
                        pltpu.VMEM((2, n_pairs, c, c), BF16),
                        pltpu.VMEM((2, n_pairs, c, d), BF16),
                        pltpu.VMEM((2, n_pairs, c, d), BF16),
                        pltpu.VMEM((2, n_pairs, 8, d), F32)],
        compiler_params=pltpu.CompilerParams(dimension_semantics=("arbitrary",),
                                             vmem_limit_bytes=VMEM_LIMIT),
        name="gdn",
    )(qkv, z, small, alog_v, dtb_v, ogain)


def _cmp_kernel(kc_ref, vc_ref, posk_ref, w1k_ref, w2k_ref, posv_ref, w1v_ref, w2v_ref, kco_ref, vco_ref):
    for x_ref, pos_ref, w1_ref, w2_ref, out_ref in ((kc_ref, posk_ref, w1k_ref, w2k_ref, kco_ref),
                                                    (vc_ref, posv_ref, w1v_ref, w2v_ref, vco_ref)):
        a = x_ref[0].astype(F32)
        n = a.shape[0]
        p0 = _dot((a + pos_ref[0:1, :]).astype(BF16), w1_ref[0])
        p1 = _dot((a + pos_ref[1:2, :]).astype(BF16), w1_ref[1])
        hid = _silu(p0 + pltpu.roll(p1, n - 1, 0))
        out_ref[0] = _dot(hid.astype(BF16), w2_ref[...]).astype(BF16)


def _cmp_call(kc3, vc3, posk, w1k, w2k, posv, w1v, w2v):
    b, n, width = kc3.shape
    hid2 = NSA_KV_GROUPS * CMP_HIDDEN
    per_b = lambda i: (i, 0, 0)
    c2 = lambda i: (0, 0)
    c3 = lambda i: (0, 0, 0)
    wspecs = [pl.BlockSpec((2, width), c2), pl.BlockSpec((2, width, hid2), c3), pl.BlockSpec((hid2, NSA_KV_WIDTH), c2)]
    return pl.pallas_call(
        _cmp_kernel,
        grid=(b,),
        in_specs=[pl.BlockSpec((1, n, width), per_b), pl.BlockSpec((1, n, width), per_b)] + wspecs + wspecs,
        out_specs=[pl.BlockSpec((1, n, NSA_KV_WIDTH), per_b)] * 2,
        out_shape=[jax.ShapeDtypeStruct((b, n, NSA_KV_WIDTH), BF16)] * 2,
        compiler_params=pltpu.CompilerParams(dimension_semantics=("parallel",),
                                             vmem_limit_bytes=VMEM_LIMIT),
        name="nsa_compress",
    )(kc3, vc3, posk, w1k, w2k, posv, w1v, w2v)


def _nsa_kernel(q_ref, kc_ref, vc_ref, kvs_ref, small_ref, ovt_ref, o_ref,
                qa_ref, m_ref, acc_ref, s_ref, part_ref):
    tq = q_ref.shape[0]
    rh = NSA_GROUP_HEADS
    n_cmp = kc_ref.shape[1]
    n_blk = ovt_ref.shape[0]
    qi = pl.program_id(1)
    s0 = qi * tq

    row_half = lax.broadcasted_iota(jnp.int32, (LANES, tq), 0) // NSA_HEAD_DIM
    t_row = s0 + lax.broadcasted_iota(jnp.int32, (1, tq), 1)
    k_col = lax.broadcasted_iota(jnp.int32, (KB, 1), 0)
    q_t = [q_ref[:, r * LANES:(r + 1) * LANES].astype(F32).T for r in range(rh)]
    gates_t = _sigmoid(small_ref[...]).T
    kc = kc_ref[0]
    vc = vc_ref[0]
    cmp_end = lax.broadcasted_iota(jnp.int32, (n_cmp, 1), 0) * CMP_STRIDE + (CMP_BLOCK - 1)
    vis1 = cmp_end <= t_row
    vis = jnp.concatenate([vis1] * rh, axis=1)

    blk = lax.broadcasted_iota(jnp.int32, (n_blk, tq), 0)
    cur = (s0 + lax.broadcasted_iota(jnp.int32, (n_blk, tq), 1)) // SEL_BLOCK
    valid = blk <= cur
    forced = (blk == 0) | (blk == cur) | (blk == cur - 1)

    def tile4(a):
        return jnp.concatenate([a] * rh, axis=1)

    def normalized(acc, g):
        l_row = acc[(1 - g) * NSA_HEAD_DIM:(1 - g) * NSA_HEAD_DIM + 1, :]
        return acc * (1.0 / l_row)

    groups = range(NSA_KV_GROUPS)
    qs_l, o_cmp_l, imp_l = [], [], []
    for g in groups:
        qs = jnp.concatenate([jnp.where(row_half == g, q_t[r], 0.0) for r in range(rh)], axis=1).astype(BF16)
        qs_l.append(qs)
        s = jnp.where(vis, _dot(kc, qs), NEG_INF)
        m = jnp.max(s, axis=0, keepdims=True)
        e = jnp.where(vis, jnp.exp2(s - m), 0.0)
        den = jnp.sum(e, axis=0, keepdims=True)
        p = e * (1.0 / jnp.where(den > 0.0, den, 1.0))
        o_cmp_l.append(_dot_tn(vc, p.astype(BF16)))
        p_sum = p[:, 0:tq]
        for r in range(1, rh):
            p_sum = p_sum + p[:, r * tq:(r + 1) * tq]
        p_hi = p_sum.astype(BF16)
        p_lo = (p_sum - p_hi.astype(F32)).astype(BF16)
        ovt = ovt_ref[...]
        imp_l.append(_dot(ovt, p_hi) + _dot(ovt, p_lo))

    span = WINDOW + TQ
    w0 = pl.multiple_of(jnp.maximum(s0 - WINDOW, 0), TQ)
    kw = kvs_ref[pl.ds(w0, span), KV_KW:KV_KW + LANES]
    rel = t_row - (w0 + lax.broadcasted_iota(jnp.int32, (span, 1), 0))
    win_bias = tile4(jnp.where((rel >= 0) & (rel < WINDOW), 0.0, NEG_INF))

    def gate_row(g, branch):
        cols = [SMALL_GATE + (g * rh + r) * 3 + branch for r in range(rh)]
        return jnp.concatenate([gates_t[c:c + 1, :] for c in cols], axis=1)

    for g in groups:
        vw = kvs_ref[pl.ds(w0, span), KV_VW + g * LANES:KV_VW + (g + 1) * LANES]
        sw = _dot(kw, qs_l[g]) + win_bias
        pw = jnp.exp2(sw - jnp.max(sw, axis=0, keepdims=True))
        o_win = normalized(_dot_tn(vw, pw.astype(BF16)), g)
        part_ref[g] = gate_row(g, 0) * o_cmp_l[g] + gate_row(g, 2) * o_win

    sub = lax.broadcasted_iota(jnp.int32, (8, tq), 0)
    n_slab = n_blk // 8
    for g in groups:
        score = jnp.where(valid, jnp.where(forced, FORCED_SCORE, imp_l[g]), -1.0)
        slabs = [score[8 * v:8 * (v + 1), :] for v in range(n_slab)]
        ranks = [jnp.zeros((8, tq), F32) for _ in range(n_slab)]
        for i in range(n_blk):
            vi, ri = divmod(i, 8)
            si = jnp.broadcast_to(score[i:i + 1, :], (8, tq))
            for v in range(n_slab):
                if v > vi:
                    hit = jnp.where(si >= slabs[v], 1.0, 0.0)
                elif v < vi:
                    hit = jnp.where(si > slabs[v], 1.0, 0.0)
                else:
                    hit = jnp.where(sub > ri, jnp.where(si >= slabs[v], 1.0, 0.0), jnp.where(si > slabs[v], 1.0, 0.0))
                ranks[v] = ranks[v] + hit
        rank = jnp.concatenate(ranks, axis=0)
        sel_bias = jnp.where((rank < float(SEL_COUNT)) & valid, 0.0, NEG_INF)
        if n_blk < NSA_HEAD_DIM:
            sel_bias = jnp.concatenate([sel_bias, jnp.zeros((NSA_HEAD_DIM - n_blk, tq), F32)], axis=0)
        q_g = qs_l[g][g * NSA_HEAD_DIM:(g + 1) * NSA_HEAD_DIM, :]
        qa_ref[g] = jnp.concatenate([q_g, tile4(sel_bias).astype(BF16)], axis=0)

    kb_diag = s0 // KB

    def scores(kb):
        k0 = pl.multiple_of(kb * KB, KB)
        return [_dot(kvs_ref[pl.ds(k0, KB), KV_KS + g * LANES:KV_KS + (g + 1) * LANES], qa_ref[g])
                for g in groups]

    def accumulate(kb, s_l, first):
        k0 = pl.multiple_of(kb * KB, KB)
        for g in groups:
            va = kvs_ref[pl.ds(k0, KB), KV_VS + g * LANES:KV_VS + (g + 1) * LANES]
            if first:
                m_new = jnp.max(s_l[g], axis=0, keepdims=True)
                acc_ref[g] = _dot_tn(va, jnp.exp2(s_l[g] - m_new).astype(BF16))
            else:
                m_old = m_ref[g]
                m_new = jnp.maximum(m_old, jnp.max(s_l[g], axis=0, keepdims=True))
                acc_ref[g] = (jnp.exp2(m_old - m_new) * acc_ref[g]
                              + _dot_tn(va, jnp.exp2(s_l[g] - m_new).astype(BF16)))
            m_ref[g] = m_new

    causal_bias = tile4(jnp.where(kb_diag * KB + k_col <= t_row, 0.0, NEG_INF))
    s_diag = [s + causal_bias for s in scores(kb_diag)]
    s_next = scores(jnp.maximum(kb_diag - 1, 0))
    accumulate(kb_diag, s_diag, True)
    for g in groups:
        s_ref[g] = s_next[g]

    def sel_body(i, carry):
        kb = kb_diag - i
        s_cur = [s_ref[g] for g in groups]
        s_nxt = scores(jnp.maximum(kb - 1, 0))
        accumulate(kb, s_cur, False)
        for g in groups:
            s_ref[g] = s_nxt[g]
        return carry

    lax.fori_loop(1, kb_diag + 1, sel_body, 0)

    out_t = [part_ref[g] + gate_row(g, 1) * normalized(acc_ref[g], g) for g in groups]
    for r in range(rh):
        cs = slice(r * tq, (r + 1) * tq)
        slab_t = jnp.where(row_half == 0, out_t[0][:, cs], out_t[1][:, cs])
        o_ref[:, r * LANES:(r + 1) * LANES] = slab_t.T.astype(BF16)


def _nsa_call(q, kc, vc, kvs, small, ovt, batch, seq):
    assert seq % KB == 0 and seq >= WINDOW + TQ and seq // SEL_BLOCK <= NSA_HEAD_DIM
    nq = seq // TQ
    n_cmp = kc.shape[1]
    row = lambda b, i: (b * nq + i, 0)
    per_b3 = lambda b, i: (b, 0, 0)
    per_b2 = lambda b, i: (b, 0)
    c2 = lambda b, i: (0, 0)
    rows = NSA_GROUP_HEADS * TQ
    return pl.pallas_call(
        _nsa_kernel,
        grid=(batch, nq),
        in_specs=[pl.BlockSpec((TQ, NSA_WIDTH), row),
                  pl.BlockSpec((1, n_cmp, NSA_KV_WIDTH), per_b3),
                  pl.BlockSpec((1, n_cmp, NSA_KV_WIDTH), per_b3),
                  pl.BlockSpec((seq, KV_WIDTH), per_b2),
                  pl.BlockSpec((TQ, LANES), row),
                  pl.BlockSpec(ovt.shape, c2)],
        out_specs=pl.BlockSpec((TQ, NSA_WIDTH), row),
        out_shape=jax.ShapeDtypeStruct((batch * seq, NSA_WIDTH), BF16),
        scratch_shapes=[pltpu.VMEM((NSA_KV_GROUPS, LANES, rows), BF16),
                        pltpu.VMEM((NSA_KV_GROUPS, 1, rows), F32),
                        pltpu.VMEM((NSA_KV_GROUPS, LANES, rows), F32),
                        pltpu.VMEM((NSA_KV_GROUPS, KB, rows), F32),
                        pltpu.VMEM((NSA_KV_GROUPS, LANES, rows), F32)],
        compiler_params=pltpu.CompilerParams(dimension_semantics=("parallel", "arbitrary"),
                                             vmem_limit_bytes=VMEM_LIMIT),
        name="nsa_attention",
    )(q, kc, vc, kvs, small, ovt)


def _out_kernel(x_ref, oa_ref, ob_ref, mg_ref, wa_ref, wb_ref, wo_ref, g2_ref, wgu_ref, wd_ref, gf_ref, out_ref):
    a = _dot(oa_ref[...], wa_ref[...])
    b = _dot(ob_ref[...], wb_ref[...])
    merged = (_sigmoid(mg_ref[:, 0:D_MODEL].astype(F32)) * a
              + _sigmoid(mg_ref[:, D_MODEL:2 * D_MODEL].astype(F32)) * b)
    x1 = x_ref[...] + _dot(merged.astype(BF16), wo_ref[...])
    h2 = _rms(x1, g2_ref[...]).astype(BF16)
    gate = _dot(h2, wgu_ref[:, 0:FFN_HIDDEN])
    up = _dot(h2, wgu_ref[:, FFN_HIDDEN:2 * FFN_HIDDEN])
    y = x1 + _dot((_silu(gate) * up).astype(BF16), wd_ref[...])
    out_ref[...] = _rms(y, gf_ref[...])


def _out_call(x2d, o_a, o_b, mg, wa, wb, wo, g2, wgu, wd, gf, tm=256):
    t = x2d.shape[0]
    row = lambda i: (i, 0)
    const = lambda i: (0, 0)

    def wspec(w):
        return pl.BlockSpec(w.shape, const, pipeline_mode=pl.Buffered(1))

    return pl.pallas_call(
        _out_kernel,
        grid=(t // tm,),
        in_specs=[pl.BlockSpec((tm, D_MODEL), row),
                  pl.BlockSpec((tm, GDN_WIDTH), row),
                  pl.BlockSpec((tm, NSA_WIDTH), row),
                  pl.BlockSpec((tm, 2 * D_MODEL), row),
                  wspec(wa), wspec(wb), wspec(wo),
                  pl.BlockSpec((1, D_MODEL), const),
                  wspec(wgu), wspec(wd),
                  pl.BlockSpec((1, D_MODEL), const)],
        out_specs=pl.BlockSpec((tm, D_MODEL), row),
        out_shape=jax.ShapeDtypeStruct((t, D_MODEL), F32),
        compiler_params=pltpu.CompilerParams(dimension_semantics=("parallel",),
                                             vmem_limit_bytes=VMEM_LIMIT),
        name="merge_ffn",
    )(x2d, o_a, o_b, mg, wa, wb, wo, g2, wgu, wd, gf)


def _q_head_perm():
    idx = []
    for r in range(NSA_GROUP_HEADS):
        for g in range(NSA_KV_GROUPS):
            base = (g * NSA_GROUP_HEADS + r) * NSA_HEAD_DIM
            idx.extend(range(base, base + NSA_HEAD_DIM))
    return np.asarray(idx, dtype=np.int32)


def _reorder_w_in(w_in):
    sizes = (3 * GDN_WIDTH, GDN_WIDTH, GDN_HEADS, GDN_HEADS, NSA_WIDTH) + (NSA_KV_WIDTH,) * 6 + (
        3 * NSA_HEADS, D_MODEL, D_MODEL)
    offs = np.concatenate([[0], np.cumsum(sizes)])
    seg = lambda i: w_in[:, offs[i]:offs[i + 1]]
    qkv, z, beta, alpha, q, kc, vc, ks, vs, kw, vw, gate, ma, mb = [seg(i) for i in range(14)]
    pad = jnp.zeros((D_MODEL, LANES - 2 * GDN_HEADS - 3 * NSA_HEADS), w_in.dtype)
    q = q[:, _q_head_perm()]
    zero_half = jnp.zeros((D_MODEL, NSA_HEAD_DIM), w_in.dtype)
    ks_split = [ks[:, :NSA_HEAD_DIM], zero_half, ks[:, NSA_HEAD_DIM:], zero_half]
    return jnp.concatenate([qkv, z, q, kc, vc] + ks_split + [vs, kw, vw, ma, mb, beta, alpha, gate, pad],
                           axis=1).astype(BF16)


def _rope_tables(seq):
    half = ROPE_DIM // 2
    inv_freq = ROPE_THETA ** (-jnp.arange(half, dtype=F32) / half)
    ang = jnp.arange(seq, dtype=F32)[:, None] * inv_freq
    cos, sin = jnp.cos(ang), jnp.sin(ang)
    ones = jnp.ones((seq, NSA_HEAD_DIM - ROPE_DIM), F32)
    zeros = jnp.zeros((seq, NSA_HEAD_DIM - ROPE_DIM), F32)
    z8 = jnp.zeros((seq, half), F32)
    c = jnp.concatenate([cos, cos, ones], axis=1)
    s1 = jnp.concatenate([z8, sin, zeros], axis=1)
    s2 = jnp.concatenate([-sin, z8, zeros], axis=1)
    tile2 = lambda a: jnp.concatenate([a, a], axis=1)
    return tile2(c), tile2(s1), tile2(s2)


def _cmp_weights(pos, w1, w2):
    g = NSA_KV_GROUPS
    seg = CMP_BLOCK // 2
    eye = jnp.eye(g, dtype=w1.dtype)
    pos_h = pos.reshape(2, seg, 1, NSA_HEAD_DIM)
    pos_flat = jnp.broadcast_to(pos_h, (2, seg, g, NSA_HEAD_DIM)).reshape(2, seg * g * NSA_HEAD_DIM)
    w1_h = w1.reshape(2, seg, NSA_HEAD_DIM, CMP_HIDDEN)
    w1_bd = jnp.einsum('alds,gk->algdks', w1_h, eye).reshape(2, seg * g * NSA_HEAD_DIM, g * CMP_HIDDEN)
    w2_bd = jnp.einsum('hd,gk->ghkd', w2, eye).reshape(g * CMP_HIDDEN, g * NSA_HEAD_DIM)
    return pos_flat.astype(F32), w1_bd.astype(BF16), w2_bd.astype(BF16)


def _selection_constants(seq):
    n_cmp_pad = seq // CMP_STRIDE
    n_blk = seq // SEL_BLOCK
    cmp_start = np.arange(n_cmp_pad) * CMP_STRIDE
    cmp_end = cmp_start + CMP_BLOCK - 1
    sel_start = np.arange(n_blk) * SEL_BLOCK
    ovt = ((cmp_start[None, :] <= sel_start[:, None] + SEL_BLOCK - 1) & (cmp_end[None, :] >= sel_start[:, None]))
    ovt[:, n_cmp_pad - 1] = False
    return jnp.asarray(ovt, BF16)


def _lane_vector(values, offset):
    v = jnp.zeros((1, LANES), F32)
    return v.at[0, offset:offset + values.shape[0]].set(values.astype(F32))


def _hybrid_block(x, mix_norm_gain, w_in, gdn_conv_w, gdn_a_log, gdn_dt_bias, gdn_out_norm_gain,
                  cmp_pos_k, cmp_w1_k, cmp_w2_k, cmp_pos_v, cmp_w1_v, cmp_w2_v,
                  w_branch_gdn, w_branch_nsa, w_out, ffn_norm_gain, w_gate_up, w_down, final_norm_gain):
    batch, seq, _ = x.shape
    x2d = x.reshape(batch * seq, D_MODEL)
    rc, rs1, rs2 = _rope_tables(seq)
    qkv, z, q, kc, vc, kvs, mg, small = _inproj_call(
        x2d, mix_norm_gain.reshape(1, D_MODEL), _reorder_w_in(w_in), rc, rs1, rs2,
        gdn_conv_w.astype(F32), seq)

    o_a = _gdn_call(qkv, z, small,
                    _lane_vector(gdn_a_log, SMALL_ALPHA), _lane_vector(gdn_dt_bias, SMALL_ALPHA),
                    gdn_out_norm_gain.reshape(1, GDN_HEAD_DIM).astype(F32), batch, seq)

    seg_width = (CMP_BLOCK // 2) * NSA_KV_WIDTH
    kc3 = kc.reshape(batch, seq // CMP_STRIDE, seg_width)
    vc3 = vc.reshape(batch, seq // CMP_STRIDE, seg_width)
    kcc, vcc = _cmp_call(kc3, vc3, *_cmp_weights(cmp_pos_k, cmp_w1_k, cmp_w2_k),
                         *_cmp_weights(cmp_pos_v, cmp_w1_v, cmp_w2_v))
    o_b = _nsa_call(q, kcc, vcc, kvs, small, _selection_constants(seq), batch, seq)

    out = _out_call(x2d, o_a, o_b, mg,
                    w_branch_gdn.astype(BF16), w_branch_nsa[_q_head_perm(), :].astype(BF16), w_out.astype(BF16),
                    ffn_norm_gain.reshape(1, D_MODEL), w_gate_up.astype(BF16), w_down.astype(BF16),
                    final_norm_gain.reshape(1, D_MODEL))
    return out.reshape(batch, seq, D_MODEL)


def kernel(x, mix_norm_gain, w_in, gdn_conv_w, gdn_a_log, gdn_dt_bias, gdn_out_norm_gain, cmp_pos_k, cmp_w1_k,
           cmp_w2_k, cmp_pos_v, cmp_w1_v, cmp_w2_v, w_branch_gdn, w_branch_nsa, w_out, ffn_norm_gain, w_gate_up,
           w_down, final_norm_gain):
    assert mix_norm_gain.shape[0] == 1, "single-layer block"
    return _hybrid_block(x, mix_norm_gain[0], w_in[0], gdn_conv_w[0], gdn_a_log[0], gdn_dt_bias[0],
                         gdn_out_norm_gain[0], cmp_pos_k[0], cmp_w1_k[0], cmp_w2_k[0], cmp_pos_v[0], cmp_w1_v[0],
                         cmp_w2_v[0], w_branch_gdn[0], w_branch_nsa[0], w_out[0], ffn_norm_gain[0], w_gate_up[0],
                         w_down[0], final_norm_gain)
```

```python
import functools

import numpy as np
import jax
import jax.numpy as jnp
from jax import lax
from jax.experimental import pallas as pl
from jax.experimental.pallas import tpu as pltpu

F32 = jnp.float32
BF16 = jnp.bfloat16

D_MODEL = 1024
NORM_EPS = 1e-6
GDN_HEADS = 4
GDN_HEAD_DIM = 128
GDN_WIDTH = GDN_HEADS * GDN_HEAD_DIM
GDN_CONV = 4
GDN_CHUNK = 64
NSA_HEADS = 8
NSA_KV_GROUPS = 2
NSA_GROUP_HEADS = NSA_HEADS // NSA_KV_GROUPS
NSA_HEAD_DIM = 64
NSA_WIDTH = NSA_HEADS * NSA_HEAD_DIM
NSA_KV_WIDTH = NSA_KV_GROUPS * NSA_HEAD_DIM
CMP_BLOCK = 32
CMP_STRIDE = 16
CMP_HIDDEN = 128
SEL_BLOCK = 64
SEL_COUNT = 16
WINDOW = 512
ROPE_THETA = 500000.0
ROPE_DIM = NSA_HEAD_DIM // 4
FORCED_SCORE = 1000.0
NEG_INF = -1e30
FFN_HIDDEN = 2816

LANES = 128
VMEM_LIMIT = 56 * 1024 * 1024

C_QKV = (0, 1536)
C_Z = (1536, 2048)
C_Q = (2048, 2560)
C_KC = (2560, 2688)
C_VC = (2688, 2816)
C_KVS = (2816, 3456)
C_MG = (3456, 5504)
C_SMALL = (5504, 5632)
IN_WIDTH_PADDED = 5632
SMALL_BETA = 0
SMALL_ALPHA = 4
SMALL_GATE = 8

KV_KS = 0
KV_VS = 256
KV_KW = 512
KV_VW = 640
KV_WIDTH = 896
LOG2E = 1.4426950408889634
Q_SCALE = NSA_HEAD_DIM ** -0.5 * LOG2E

TQ = 128
KB = 512


def _dot(a, b):
    return jnp.dot(a, b, preferred_element_type=F32)


def _dot_nt(a, b):
    return lax.dot_general(a, b, (((1,), (1,)), ((), ())), preferred_element_type=F32)


def _dot_tn(a, b):
    return lax.dot_general(a, b, (((0,), (0,)), ((), ())), preferred_element_type=F32)


def _sigmoid(x):
    return 1.0 / (1.0 + jnp.exp(-x))


def _silu(x):
    return x * _sigmoid(x)


def _rms(x, gain):
    return x * lax.rsqrt(jnp.mean(x * x, axis=-1, keepdims=True) + NORM_EPS) * gain


def _rope_slab(y, c, s1, s2):
    return y * c + pltpu.roll(y, 8, 1) * s1 + pltpu.roll(y, LANES - 8, 1) * s2


def _inproj_kernel(x_ref, gain_ref, w_ref, rc_ref, rs1_ref, rs2_ref, convw_ref,
                   qkv_ref, z_ref, q_ref, kc_ref, vc_ref, kvs_ref, mg_ref, small_ref, xbuf, *, seq):
    tm = x_ref.shape[0]
    hb = _rms(x_ref[...], gain_ref[...]).astype(BF16)

    def proj(cols):
        return _dot(hb, w_ref[:, cols[0]:cols[1]])

    first_tile = pl.program_id(0) % (seq // tm) == 0

    @pl.when(first_tile)
    def _():
        xbuf[0:8, :] = jnp.zeros((8, 3 * GDN_WIDTH), F32)

    @pl.when(jnp.logical_not(first_tile))
    def _():
        xbuf[0:8, :] = xbuf[tm:tm + 8, :]

    cw = convw_ref[...]
    slab_w = 2 * GDN_HEAD_DIM

    def gdn_slab(k):
        lo = k * slab_w
        cols = slice(lo, lo + slab_w)
        xbuf[8:8 + tm, cols] = proj((C_QKV[0] + lo, C_QKV[0] + lo + slab_w))
        conv = xbuf[8:8 + tm, cols] * cw[GDN_CONV - 1:GDN_CONV, cols]
        for jj in range(GDN_CONV - 1):
            off = 8 - (GDN_CONV - 1) + jj
            conv = conv + xbuf[off:off + tm, cols] * cw[jj:jj + 1, cols]
        act = _silu(conv)
        for h in range(lo // GDN_HEAD_DIM, (lo + slab_w) // GDN_HEAD_DIM):
            a = act[:, h * GDN_HEAD_DIM - lo:(h + 1) * GDN_HEAD_DIM - lo]
            if h < 2 * GDN_HEADS:
                a = a * lax.rsqrt(jnp.sum(a * a, axis=-1, keepdims=True) + NORM_EPS)
            if h < GDN_HEADS:
                a = a * (GDN_HEAD_DIM ** -0.5)
            qkv_ref[:, h * GDN_HEAD_DIM:(h + 1) * GDN_HEAD_DIM] = a.astype(BF16)

    c, s1, s2 = rc_ref[...], rs1_ref[...], rs2_ref[...]

    def task_z():
        z_ref[...] = proj(C_Z).astype(BF16)

    def task_merge(half):
        lo = C_MG[0] + half * D_MODEL
        mg_ref[:, half * D_MODEL:(half + 1) * D_MODEL] = proj((lo, lo + D_MODEL)).astype(BF16)

    def task_small():
        small_ref[...] = proj(C_SMALL)
        vc_ref[...] = proj(C_VC).astype(BF16)
        kc_ref[...] = _rope_slab(proj(C_KC), c, s1, s2).astype(BF16)

    def task_q():
        q = proj(C_Q)
        for r in range(NSA_GROUP_HEADS):
            sl = slice(r * LANES, (r + 1) * LANES)
            q_ref[:, sl] = (_rope_slab(q[:, sl], c, s1, s2) * Q_SCALE).astype(BF16)

    def task_kv():
        tok = (pl.program_id(0) % (seq // tm)) * tm + lax.broadcasted_iota(jnp.int32, (tm, 1), 0)
        lane = lax.broadcasted_iota(jnp.int32, (tm, LANES), 1)
        group0 = lane < NSA_HEAD_DIM
        block_onehot = jnp.where(lane == NSA_HEAD_DIM + tok // SEL_BLOCK, 1.0, 0.0)
        kvs = proj(C_KVS)
        for g in range(NSA_KV_GROUPS):
            ks = _rope_slab(kvs[:, g * LANES:(g + 1) * LANES], c, s1, s2)
            kvs_ref[:, KV_KS + g * LANES:KV_KS + (g + 1) * LANES] = jnp.where(group0, ks, block_onehot).astype(BF16)
        vs, kw, vw = kvs[:, 256:384], kvs[:, 384:512], kvs[:, 512:640]
        kvs_ref[:, KV_VS:KV_VS + LANES] = jnp.where(group0, vs, 1.0).astype(BF16)
        kvs_ref[:, KV_VS + LANES:KV_VS + 2 * LANES] = jnp.where(group0, 1.0, vs).astype(BF16)
        kvs_ref[:, KV_KW:KV_KW + LANES] = _rope_slab(kw, c, s1, s2).astype(BF16)
        kvs_ref[:, KV_VW:KV_VW + LANES] = jnp.where(group0, vw, 1.0).astype(BF16)
        kvs_ref[:, KV_VW + LANES:KV_VW + 2 * LANES] = jnp.where(group0, 1.0, vw).astype(BF16)

    others = [task_z, functools.partial(task_merge, 0), task_q, functools.partial(task_merge, 1), task_kv,
              task_small]
    n_slabs = 3 * GDN_WIDTH // slab_w
    assert len(others) == n_slabs
    for k in range(n_slabs):
        gdn_slab(k)
        others[k]()


def _inproj_call(x2d, gain, w_r, rc, rs1, rs2, conv_w, seq, tm=256):
    assert seq // SEL_BLOCK <= LANES, "selection-block one-hot must fit one lane tile"
    t = x2d.shape[0]
    n_seq_tiles = seq // tm
    row = lambda i: (i, 0)
    const = lambda i: (0, 0)
    tab = lambda i: (i % n_seq_tiles, 0)
    widths = [(1536, BF16), (512, BF16), (512, BF16), (128, BF16), (128, BF16), (KV_WIDTH, BF16), (2048, BF16),
              (128, F32)]
    return pl.pallas_call(
        functools.partial(_inproj_kernel, seq=seq),
        grid=(t // tm,),
        in_specs=[pl.BlockSpec((tm, D_MODEL), row),
                  pl.BlockSpec((1, D_MODEL), const),
                  pl.BlockSpec((D_MODEL, IN_WIDTH_PADDED), const, pipeline_mode=pl.Buffered(1)),
                  pl.BlockSpec((tm, LANES), tab),
                  pl.BlockSpec((tm, LANES), tab),
                  pl.BlockSpec((tm, LANES), tab),
                  pl.BlockSpec((GDN_CONV, 3 * GDN_WIDTH), const)],
        out_specs=[pl.BlockSpec((tm, w), row) for w, _ in widths],
        out_shape=[jax.ShapeDtypeStruct((t, w), d) for w, d in widths],
        scratch_shapes=[pltpu.VMEM((tm + 8, 3 * GDN_WIDTH), F32)],
        compiler_params=pltpu.CompilerParams(dimension_semantics=("arbitrary",),
                                             vmem_limit_bytes=VMEM_LIMIT),
        name="inproj",
    )(x2d, gain, w_r, rc, rs1, rs2, conv_w)


def _gdn_kernel(qkv_ref, z_ref, small_ref, alog_ref, dtb_ref, ogain_ref, o_ref,
                state, u_ref, w_ref, qk_ref, qg_ref, kd_ref, gl_ref, *, blocks_per_seq):
    t = pl.program_id(0)
    bufs = (u_ref, w_ref, qk_ref, qg_ref, kd_ref, gl_ref)

    @pl.when(t == 0)
    def _():
        state[...] = jnp.zeros_like(state)
        for ref in bufs:
            ref[...] = jnp.zeros_like(ref)

    fresh = (t + blocks_per_seq - 1) % blocks_per_seq == 0
    for slot in range(2):
        @pl.when(t % 2 == slot)
        def _(slot=slot):
            _gdn_step(qkv_ref, z_ref, small_ref, alog_ref, dtb_ref, ogain_ref, o_ref, state, bufs, slot, fresh)


def _gdn_step(qkv_ref, z_ref, small_ref, alog_ref, dtb_ref, ogain_ref, o_ref, state, bufs, cur, fresh):
    u_ref, w_ref, qk_ref, qg_ref, kd_ref, gl_ref = bufs
    prev = 1 - cur
    cb = qkv_ref.shape[0]
    c = GDN_CHUNK
    n_chunks = cb // c
    heads = range(GDN_HEADS)

    row = lax.broadcasted_iota(jnp.int32, (c, c), 0)
    col = lax.broadcasted_iota(jnp.int32, (c, c), 1)
    causal = row >= col
    strict = row > col
    tril = causal.astype(F32)
    eye = (row == col).astype(F32)
    neg_decay_rate = -jnp.exp(alog_ref[...])
    dtb = dtb_ref[...]
    ogain = ogain_ref[...]

    pairs = [(ci, h) for ci in range(n_chunks) for h in heads]
    n_pairs = len(pairs)
    A = {}

    def a_prepare():
        beta_c, gc_c, gct_c = [], [], []
        for ci in range(n_chunks):
            sm = small_ref[ci * c:(ci + 1) * c, :]
            beta_c.append(_sigmoid(sm))
            xg = sm + dtb
            softplus = jnp.maximum(xg, 0.0) + jnp.log(1.0 + jnp.exp(-jnp.abs(xg)))
            gc_all = jnp.dot(tril, neg_decay_rate * softplus, precision=lax.Precision.HIGHEST,
                             preferred_element_type=F32)
            gc_c.append(gc_all)
            gct_c.append(gc_all.T)

        def head_cols(base, ci, h):
            lo = base + h * GDN_HEAD_DIM
            return qkv_ref[ci * c:(ci + 1) * c, lo:lo + GDN_HEAD_DIM].astype(F32)

        A["qb"], A["kbf"], A["vb"], A["kbeg"], A["decay"], a_l = [], [], [], [], [], []
        for i, (ci, h) in enumerate(pairs):
            q = head_cols(0, ci, h)
            k = head_cols(GDN_WIDTH, ci, h)
            v = head_cols(2 * GDN_WIDTH, ci, h)
            beta = beta_c[ci][:, SMALL_BETA + h:SMALL_BETA + h + 1]
            gc = gc_c[ci][:, SMALL_ALPHA + h:SMALL_ALPHA + h + 1]
            gr = gct_c[ci][SMALL_ALPHA + h:SMALL_ALPHA + h + 1, :]
            decay = jnp.where(causal, jnp.exp(jnp.where(causal, gc - gr, 0.0)), 0.0)
            eg = jnp.exp(gc)
            g_last = gc[c - 1:c, :]
            kb = k * beta
            kbf = k.astype(BF16)
            qg_ref[cur, i] = (q * eg).astype(BF16)
            kd_ref[cur, i] = (k * jnp.exp(g_last - gc)).astype(BF16)
            gl_ref[cur, i] = jnp.broadcast_to(jnp.exp(g_last), gl_ref.shape[2:])
            A["qb"].append(q.astype(BF16))
            A["kbf"].append(kbf)
            A["vb"].append((v * beta).astype(BF16))
            A["kbeg"].append((kb * eg).astype(BF16))
            A["decay"].append(decay)
            a_l.append(jnp.where(strict, _dot_nt(kb.astype(BF16), kbf) * decay, 0.0))
        A["t"] = [eye - a for a in a_l]
        A["p"] = a_l

    def a_qk():
        for i in range(n_pairs):
            qk_ref[cur, i] = (_dot_nt(A["qb"][i], A["kbf"][i]) * A["decay"][i]).astype(BF16)

    def a_square():
        A["p"] = [_dot(p.astype(BF16), p.astype(BF16)) for p in A["p"]]

    def a_extend():
        A["t"] = [t_ + _dot(t_.astype(BF16), p.astype(BF16)) for t_, p in zip(A["t"], A["p"])]

    def a_u():
        A["tb"] = [t_.astype(BF16) for t_ in A["t"]]
        for i in range(n_pairs):
            u_ref[cur, i] = _dot(A["tb"][i], A["vb"][i])

    def a_w():
        for i in range(n_pairs):
            w_ref[cur, i] = _dot(A["tb"][i], A["kbeg"][i]).astype(BF16)

    n_doublings = c.bit_length() - 2
    a_stages = [a_prepare, a_qk] + [a_square, a_extend] * n_doublings + [a_u, a_w]

    B = {"s": [jnp.where(fresh, 0.0, state[h]) for h in heads]}

    def b_new_values(ci):
        idx = [ci * GDN_HEADS + h for h in heads]
        B["sb"] = [s.astype(BF16) for s in B["s"]]
        B["vn"] = [(u_ref[prev, i] - _dot(w_ref[prev, i], sb)).astype(BF16) for i, sb in zip(idx, B["sb"])]

    def b_output(ci):
        idx = [ci * GDN_HEADS + h for h in heads]
        o_l = [_dot(qg_ref[prev, i], sb) + _dot(qk_ref[prev, i], vn) for i, sb, vn in zip(idx, B["sb"], B["vn"])]
        B["s"] = [s * gl_ref[prev, i][0:1, :] + _dot_tn(kd_ref[prev, i], vn)
                  for i, s, vn in zip(idx, B["s"], B["vn"])]
        for h, o in zip(heads, o_l):
            hs = slice(h * GDN_HEAD_DIM, (h + 1) * GDN_HEAD_DIM)
            zh = z_ref[ci * c:(ci + 1) * c, hs].astype(F32)
            o = o * lax.rsqrt(jnp.mean(o * o, axis=-1, keepdims=True) + NORM_EPS) * ogain * _silu(zh)
            o_ref[ci * c:(ci + 1) * c, hs] = o.astype(BF16)

    b_stages = []
    for ci in range(n_chunks):
        b_stages += [functools.partial(b_new_values, ci), functools.partial(b_output, ci)]

    emitted_b = 0
    for k, stage in enumerate(a_stages):
        stage()
        while emitted_b < len(b_stages) and emitted_b * len(a_stages) < (k + 1) * len(b_stages):
            b_stages[emitted_b]()
            emitted_b += 1
    for h in heads:
        state[h] = B["s"][h]


def _gdn_call(qkv, z, small, alog_v, dtb_v, ogain, batch, seq, cb=256):
    blocks_per_seq = seq // cb
    n_blocks = batch * blocks_per_seq
    n_pairs = (cb // GDN_CHUNK) * GDN_HEADS
    c, d = GDN_CHUNK, GDN_HEAD_DIM
    phase_a_block = lambda t: (jnp.minimum(t, n_blocks - 1), 0)
    phase_b_block = lambda t: (jnp.maximum(t - 1, 0), 0)
    const = lambda t: (0, 0)
    return pl.pallas_call(
        functools.partial(_gdn_kernel, blocks_per_seq=blocks_per_seq),
        grid=(n_blocks + 1,),
        in_specs=[pl.BlockSpec((cb, 3 * GDN_WIDTH), phase_a_block),
                  pl.BlockSpec((cb, GDN_WIDTH), phase_b_block),
                  pl.BlockSpec((cb, LANES), phase_a_block),
                  pl.BlockSpec((1, LANES), const),
                  pl.BlockSpec((1, LANES), const),
                  pl.BlockSpec((1, GDN_HEAD_DIM), const)],
        out_specs=pl.BlockSpec((cb, GDN_WIDTH), phase_b_block),
        out_shape=jax.ShapeDtypeStruct((batch * seq, GDN_WIDTH), BF16),
        scratch_shapes=[pltpu.VMEM((GDN_HEADS, d, d), F32),
                        pltpu.VMEM((2, n_pairs, c, d), F32),
                        pltpu.VMEM((2, n_pairs, c, d), BF16),
                        pltpu.VMEM((2, n_pairs, c, c), BF16),
                        pltpu.VMEM((2, n_pairs, c, d), BF16),
                        pltpu.VMEM((2, n_pairs, c, d), BF16),
                        pltpu.VMEM((2, n_pairs, 8, d), F32)],
        compiler_params=pltpu.CompilerParams(dimension_semantics=("arbitrary",),
                                             vmem_limit_bytes=VMEM_LIMIT),
        name="gdn",
    )(qkv, z, small, alog_v, dtb_v, ogain)


def _cmp_kernel(kc_ref, vc_ref, posk_ref, w1k_ref, w2k_ref, posv_ref, w1v_ref, w2v_ref, kco_ref, vco_ref):
    for x_ref, pos_ref, w1_ref, w2_ref, out_ref in ((kc_ref, posk_ref, w1k_ref, w2k_ref, kco_ref),
                                                    (vc_ref, posv_ref, w1v_ref, w2v_ref, vco_ref)):
        a = x_ref[0].astype(F32)
        n = a.shape[0]
        p0 = _dot((a + pos_ref[0:1, :]).astype(BF16), w1_ref[0])
        p1 = _dot((a + pos_ref[1:2, :]).astype(BF16), w1_ref[1])
        hid = _silu(p0 + pltpu.roll(p1, n - 1, 0))
        out_ref[0] = _dot(hid.astype(BF16), w2_ref[...]).astype(BF16)


def _cmp_call(kc3, vc3, posk, w1k, w2k, posv, w1v, w2v):
    b, n, width = kc3.shape
    hid2 = NSA_KV_GROUPS * CMP_HIDDEN
    per_b = lambda i: (i, 0, 0)
    c2 = lambda i: (0, 0)
    c3 = lambda i: (0, 0, 0)
    wspecs = [pl.BlockSpec((2, width), c2), pl.BlockSpec((2, width, hid2), c3), pl.BlockSpec((hid2, NSA_KV_WIDTH), c2)]
    return pl.pallas_call(
        _cmp_kernel,
        grid=(b,),
        in_specs=[pl.BlockSpec((1, n, width), per_b), pl.BlockSpec((1, n, width), per_b)] + wspecs + wspecs,
        out_specs=[pl.BlockSpec((1, n, NSA_KV_WIDTH), per_b)] * 2,
        out_shape=[jax.ShapeDtypeStruct((b, n, NSA_KV_WIDTH), BF16)] * 2,
        compiler_params=pltpu.CompilerParams(dimension_semantics=("parallel",),
                                             vmem_limit_bytes=VMEM_LIMIT),
        name="nsa_compress",
    )(kc3, vc3, posk, w1k, w2k, posv, w1v, w2v)


def _nsa_kernel(q_ref, kc_ref, vc_ref, kvs_ref, small_ref, ovt_ref, o_ref,
                qa_ref, m_ref, acc_ref, s_ref, part_ref):
    tq = q_ref.shape[0]
    rh = NSA_GROUP_HEADS
    n_cmp = kc_ref.shape[1]
    n_blk = ovt_ref.shape[0]
    qi = pl.program_id(1)
    s0 = qi * tq

    row_half = lax.broadcasted_iota(jnp.int32, (LANES, tq), 0) // NSA_HEAD_DIM
    t_row = s0 + lax.broadcasted_iota(jnp.int32, (1, tq), 1)
    k_col = lax.broadcasted_iota(jnp.int32, (KB, 1), 0)
    q_t = [q_ref[:, r * LANES:(r + 1) * LANES].astype(F32).T for r in range(rh)]
    gates_t = _sigmoid(small_ref[...]).T
    kc = kc_ref[0]
    vc = vc_ref[0]
    cmp_end = lax.broadcasted_iota(jnp.int32, (n_cmp, 1), 0) * CMP_STRIDE + (CMP_BLOCK - 1)
    vis1 = cmp_end <= t_row
    vis = jnp.concatenate([vis1] * rh, axis=1)

    blk = lax.broadcasted_iota(jnp.int32, (n_blk, tq), 0)
    cur = (s0 + lax.broadcasted_iota(jnp.int32, (n_blk, tq), 1)) // SEL_BLOCK
    valid = blk <= cur
    forced = (blk == 0) | (blk == cur) | (blk == cur - 1)

    def tile4(a):
        return jnp.concatenate([a] * rh, axis=1)

    def normalized(acc, g):
        l_row = acc[(1 - g) * NSA_HEAD_DIM:(1 - g) * NSA_HEAD_DIM + 1, :]
        return acc * (1.0 / l_row)

    groups = range(NSA_KV_GROUPS)
    def gate_row(g, branch):
        cols = [SMALL_GATE + (g * rh + r) * 3 + branch for r in range(rh)]
        return jnp.concatenate([gates_t[c:c + 1, :] for c in cols], axis=1)

    qs_l = [jnp.concatenate([jnp.where(row_half == g, q_t[r], 0.0) for r in range(rh)], axis=1).astype(BF16)
            for g in groups]
    span = WINDOW + TQ
    w0 = pl.multiple_of(jnp.maximum(s0 - WINDOW, 0), TQ)
    kw = kvs_ref[pl.ds(w0, span), KV_KW:KV_KW + LANES]
    s_cmp = [_dot(kc, qs_l[g]) for g in groups]
    s_win = [_dot(kw, qs_l[g]) for g in groups]

    def cmp_branch(g):
        s = jnp.where(vis, s_cmp[g], NEG_INF)
        m = jnp.max(s, axis=0, keepdims=True)
        e = jnp.where(vis, jnp.exp2(s - m), 0.0)
        den = jnp.sum(e, axis=0, keepdims=True)
        p = e * (1.0 / jnp.where(den > 0.0, den, 1.0))
        o_cmp = _dot_tn(vc, p.astype(BF16))
        p_sum = p[:, 0:tq]
        for r in range(1, rh):
            p_sum = p_sum + p[:, r * tq:(r + 1) * tq]
        p_hi = p_sum.astype(BF16)
        p_lo = (p_sum - p_hi.astype(F32)).astype(BF16)
        ovt = ovt_ref[...]
        return o_cmp, _dot(ovt, p_hi) + _dot(ovt, p_lo)

    rel = t_row - (w0 + lax.broadcasted_iota(jnp.int32, (span, 1), 0))
    win_bias = tile4(jnp.where((rel >= 0) & (rel < WINDOW), 0.0, NEG_INF))

    def window_branch(g, o_cmp):
        vw = kvs_ref[pl.ds(w0, span), KV_VW + g * LANES:KV_VW + (g + 1) * LANES]
        sw = s_win[g] + win_bias
        pw = jnp.exp2(sw - jnp.max(sw, axis=0, keepdims=True))
        o_win = normalized(_dot_tn(vw, pw.astype(BF16)), g)
        part_ref[g] = gate_row(g, 0) * o_cmp + gate_row(g, 2) * o_win

    sub = lax.broadcasted_iota(jnp.int32, (8, tq), 0)
    n_slab = n_blk // 8

    def select_blocks(g, imp):
        score = jnp.where(valid, jnp.where(forced, FORCED_SCORE, imp), -1.0)
        slabs = [score[8 * v:8 * (v + 1), :] for v in range(n_slab)]
        ranks = [jnp.zeros((8, tq), F32) for _ in range(n_slab)]
        for i in range(n_blk):
            vi, ri = divmod(i, 8)
            si = jnp.broadcast_to(score[i:i + 1, :], (8, tq))
            for v in range(n_slab):
                if v > vi:
                    hit = jnp.where(si >= slabs[v], 1.0, 0.0)
                elif v < vi:
                    hit = jnp.where(si > slabs[v], 1.0, 0.0)
                else:
                    hit = jnp.where(sub > ri, jnp.where(si >= slabs[v], 1.0, 0.0), jnp.where(si > slabs[v], 1.0, 0.0))
                ranks[v] = ranks[v] + hit
        rank = jnp.concatenate(ranks, axis=0)
        sel_bias = jnp.where((rank < float(SEL_COUNT)) & valid, 0.0, NEG_INF)
        if n_blk < NSA_HEAD_DIM:
            sel_bias = jnp.concatenate([sel_bias, jnp.zeros((NSA_HEAD_DIM - n_blk, tq), F32)], axis=0)
        q_g = qs_l[g][g * NSA_HEAD_DIM:(g + 1) * NSA_HEAD_DIM, :]
        qa_ref[g] = jnp.concatenate([q_g, tile4(sel_bias).astype(BF16)], axis=0)

    o_cmp0, imp0 = cmp_branch(0)
    o_cmp1, imp1 = cmp_branch(1)
    select_blocks(0, imp0)
    window_branch(0, o_cmp0)
    select_blocks(1, imp1)
    window_branch(1, o_cmp1)

    kb_diag = s0 // KB

    def scores(kb):
        k0 = pl.multiple_of(kb * KB, KB)
        return [_dot(kvs_ref[pl.ds(k0, KB), KV_KS + g * LANES:KV_KS + (g + 1) * LANES], qa_ref[g])
                for g in groups]

    def accumulate(kb, s_l, first):
        k0 = pl.multiple_of(kb * KB, KB)
        for g in groups:
            va = kvs_ref[pl.ds(k0, KB), KV_VS + g * LANES:KV_VS + (g + 1) * LANES]
            if first:
                m_new = jnp.max(s_l[g], axis=0, keepdims=True)
                acc_ref[g] = _dot_tn(va, jnp.exp2(s_l[g] - m_new).astype(BF16))
            else:
                m_old = m_ref[g]
                m_new = jnp.maximum(m_old, jnp.max(s_l[g], axis=0, keepdims=True))
                acc_ref[g] = (jnp.exp2(m_old - m_new) * acc_ref[g]
                              + _dot_tn(va, jnp.exp2(s_l[g] - m_new).astype(BF16)))
            m_ref[g] = m_new

    causal_bias = tile4(jnp.where(kb_diag * KB + k_col <= t_row, 0.0, NEG_INF))
    s_diag = [s + causal_bias for s in scores(kb_diag)]
    s_next = scores(jnp.maximum(kb_diag - 1, 0))
    accumulate(kb_diag, s_diag, True)
    for g in groups:
        s_ref[g] = s_next[g]

    def sel_body(i, carry):
        kb = kb_diag - i
        s_cur = [s_ref[g] for g in groups]
        s_nxt = scores(jnp.maximum(kb - 1, 0))
        accumulate(kb, s_cur, False)
        for g in groups:
            s_ref[g] = s_nxt[g]
        return carry

    lax.fori_loop(1, kb_diag + 1, sel_body, 0)

    out_t = [part_ref[g] + gate_row(g, 1) * normalized(acc_ref[g], g) for g in groups]
    for r in range(rh):
        cs = slice(r * tq, (r + 1) * tq)
        slab_t = jnp.where(row_half == 0, out_t[0][:, cs], out_t[1][:, cs])
        o_ref[:, r * LANES:(r + 1) * LANES] = slab_t.T.astype(BF16)


def _nsa_call(q, kc, vc, kvs, small, ovt, batch, seq):
    assert seq % KB == 0 and seq >= WINDOW + TQ and seq // SEL_BLOCK <= NSA_HEAD_DIM
    nq = seq // TQ
    n_cmp = kc.shape[1]
    row = lambda b, i: (b * nq + i, 0)
    per_b3 = lambda b, i: (b, 0, 0)
    per_b2 = lambda b, i: (b, 0)
    c2 = lambda b, i: (0, 0)
    rows = NSA_GROUP_HEADS * TQ
    return pl.pallas_call(
        _nsa_kernel,
        grid=(batch, nq),
        in_specs=[pl.BlockSpec((TQ, NSA_WIDTH), row),
                  pl.BlockSpec((1, n_cmp, NSA_KV_WIDTH), per_b3),
                  pl.BlockSpec((1, n_cmp, NSA_KV_WIDTH), per_b3),
                  pl.BlockSpec((seq, KV_WIDTH), per_b2),
                  pl.BlockSpec((TQ, LANES), row),
                  pl.BlockSpec(ovt.shape, c2)],
        out_specs=pl.BlockSpec((TQ, NSA_WIDTH), row),
        out_shape=jax.ShapeDtypeStruct((batch * seq, NSA_WIDTH), BF16),
        scratch_shapes=[pltpu.VMEM((NSA_KV_GROUPS, LANES, rows), BF16),
                        pltpu.VMEM((NSA_KV_GROUPS, 1, rows), F32),
                        pltpu.VMEM((NSA_KV_GROUPS, LANES, rows), F32),
                        pltpu.VMEM((NSA_KV_GROUPS, KB, rows), F32),
                        pltpu.VMEM((NSA_KV_GROUPS, LANES, rows), F32)],
        compiler_params=pltpu.CompilerParams(dimension_semantics=("parallel", "arbitrary"),
                                             vmem_limit_bytes=VMEM_LIMIT),
        name="nsa_attention",
    )(q, kc, vc, kvs, small, ovt)


def _out_kernel(x_ref, oa_ref, ob_ref, mg_ref, wa_ref, wb_ref, wo_ref, g2_ref, wgu_ref, wd_ref, gf_ref, out_ref):
    tm = x_ref.shape[0]
    halves = [slice(i * tm // 2, (i + 1) * tm // 2) for i in range(2)]
    st = [{} for _ in halves]

    def merge(i, rows):
        a = _dot(oa_ref[rows, :], wa_ref[...])
        b = _dot(ob_ref[rows, :], wb_ref[...])
        st[i]["merged"] = (_sigmoid(mg_ref[rows, 0:D_MODEL].astype(F32)) * a
                           + _sigmoid(mg_ref[rows, D_MODEL:2 * D_MODEL].astype(F32)) * b).astype(BF16)

    def out_proj(i, rows):
        st[i]["x1"] = x_ref[rows, :] + _dot(st[i]["merged"], wo_ref[...])
        st[i]["h2"] = _rms(st[i]["x1"], g2_ref[...]).astype(BF16)

    def ffn_up(i, rows):
        gate = _dot(st[i]["h2"], wgu_ref[:, 0:FFN_HIDDEN])
        up = _dot(st[i]["h2"], wgu_ref[:, FFN_HIDDEN:2 * FFN_HIDDEN])
        st[i]["act"] = (_silu(gate) * up).astype(BF16)

    def ffn_down(i, rows):
        y = st[i]["x1"] + _dot(st[i]["act"], wd_ref[...])
        out_ref[rows, :] = _rms(y, gf_ref[...])

    for stage in (merge, out_proj, ffn_up, ffn_down):
        for i, rows in enumerate(halves):
            stage(i, rows)


def _out_call(x2d, o_a, o_b, mg, wa, wb, wo, g2, wgu, wd, gf, tm=256):
    t = x2d.shape[0]
    row = lambda i: (i, 0)
    const = lambda i: (0, 0)

    def wspec(w):
        return pl.BlockSpec(w.shape, const, pipeline_mode=pl.Buffered(1))

    return pl.pallas_call(
        _out_kernel,
        grid=(t // tm,),
        in_specs=[pl.BlockSpec((tm, D_MODEL), row),
                  pl.BlockSpec((tm, GDN_WIDTH), row),
                  pl.BlockSpec((tm, NSA_WIDTH), row),
                  pl.BlockSpec((tm, 2 * D_MODEL), row),
                  wspec(wa), wspec(wb), wspec(wo),
                  pl.BlockSpec((1, D_MODEL), const),
                  wspec(wgu), wspec(wd),
                  pl.BlockSpec((1, D_MODEL), const)],
        out_specs=pl.BlockSpec((tm, D_MODEL), row),
        out_shape=jax.ShapeDtypeStruct((t, D_MODEL), F32),
        compiler_params=pltpu.CompilerParams(dimension_semantics=("parallel",),
                                             vmem_limit_bytes=VMEM_LIMIT),
        name="merge_ffn",
    )(x2d, o_a, o_b, mg, wa, wb, wo, g2, wgu, wd, gf)


def _q_head_perm():
    idx = []
    for r in range(NSA_GROUP_HEADS):
        for g in range(NSA_KV_GROUPS):
            base = (g * NSA_GROUP_HEADS + r) * NSA_HEAD_DIM
            idx.extend(range(base, base + NSA_HEAD_DIM))
    return np.asarray(idx, dtype=np.int32)


def _reorder_w_in(w_in):
    sizes = (3 * GDN_WIDTH, GDN_WIDTH, GDN_HEADS, GDN_HEADS, NSA_WIDTH) + (NSA_KV_WIDTH,) * 6 + (
        3 * NSA_HEADS, D_MODEL, D_MODEL)
    offs = np.concatenate([[0], np.cumsum(sizes)])
    seg = lambda i: w_in[:, offs[i]:offs[i + 1]]
    qkv, z, beta, alpha, q, kc, vc, ks, vs, kw, vw, gate, ma, mb = [seg(i) for i in range(14)]
    pad = jnp.zeros((D_MODEL, LANES - 2 * GDN_HEADS - 3 * NSA_HEADS), w_in.dtype)
    q = q[:, _q_head_perm()]
    zero_half = jnp.zeros((D_MODEL, NSA_HEAD_DIM), w_in.dtype)
    ks_split = [ks[:, :NSA_HEAD_DIM], zero_half, ks[:, NSA_HEAD_DIM:], zero_half]
    return jnp.concatenate([qkv, z, q, kc, vc] + ks_split + [vs, kw, vw, ma, mb, beta, alpha, gate, pad],
                           axis=1).astype(BF16)


def _rope_tables(seq):
    half = ROPE_DIM // 2
    inv_freq = ROPE_THETA ** (-jnp.arange(half, dtype=F32) / half)
    ang = jnp.arange(seq, dtype=F32)[:, None] * inv_freq
    cos, sin = jnp.cos(ang), jnp.sin(ang)
    ones = jnp.ones((seq, NSA_HEAD_DIM - ROPE_DIM), F32)
    zeros = jnp.zeros((seq, NSA_HEAD_DIM - ROPE_DIM), F32)
    z8 = jnp.zeros((seq, half), F32)
    c = jnp.concatenate([cos, cos, ones], axis=1)
    s1 = jnp.concatenate([z8, sin, zeros], axis=1)
    s2 = jnp.concatenate([-sin, z8, zeros], axis=1)
    tile2 = lambda a: jnp.concatenate([a, a], axis=1)
    return tile2(c), tile2(s1), tile2(s2)


def _cmp_weights(pos, w1, w2):
    g = NSA_KV_GROUPS
    seg = CMP_BLOCK // 2
    eye = jnp.eye(g, dtype=w1.dtype)
    pos_h = pos.reshape(2, seg, 1, NSA_HEAD_DIM)
    pos_flat = jnp.broadcast_to(pos_h, (2, seg, g, NSA_HEAD_DIM)).reshape(2, seg * g * NSA_HEAD_DIM)
    w1_h = w1.reshape(2, seg, NSA_HEAD_DIM, CMP_HIDDEN)
    w1_bd = jnp.einsum('alds,gk->algdks', w1_h, eye).reshape(2, seg * g * NSA_HEAD_DIM, g * CMP_HIDDEN)
    w2_bd = jnp.einsum('hd,gk->ghkd', w2, eye).reshape(g * CMP_HIDDEN, g * NSA_HEAD_DIM)
    return pos_flat.astype(F32), w1_bd.astype(BF16), w2_bd.astype(BF16)


def _selection_constants(seq):
    n_cmp_pad = seq // CMP_STRIDE
    n_blk = seq // SEL_BLOCK
    cmp_start = np.arange(n_cmp_pad) * CMP_STRIDE
    cmp_end = cmp_start + CMP_BLOCK - 1
    sel_start = np.arange(n_blk) * SEL_BLOCK
    ovt = ((cmp_start[None, :] <= sel_start[:, None] + SEL_BLOCK - 1) & (cmp_end[None, :] >= sel_start[:, None]))
    ovt[:, n_cmp_pad - 1] = False
    return jnp.asarray(ovt, BF16)


def _lane_vector(values, offset):
    v = jnp.zeros((1, LANES), F32)
    return v.at[0, offset:offset + values.shape[0]].set(values.astype(F32))


def _hybrid_block(x, mix_norm_gain, w_in, gdn_conv_w, gdn_a_log, gdn_dt_bias, gdn_out_norm_gain,
                  cmp_pos_k, cmp_w1_k, cmp_w2_k, cmp_pos_v, cmp_w1_v, cmp_w2_v,
                  w_branch_gdn, w_branch_nsa, w_out, ffn_norm_gain, w_gate_up, w_down, final_norm_gain):
    batch, seq, _ = x.shape
    x2d = x.reshape(batch * seq, D_MODEL)
    rc, rs1, rs2 = _rope_tables(seq)
    qkv, z, q, kc, vc, kvs, mg, small = _inproj_call(
        x2d, mix_norm_gain.reshape(1, D_MODEL), _reorder_w_in(w_in), rc, rs1, rs2,
        gdn_conv_w.astype(F32), seq)

    o_a = _gdn_call(qkv, z, small,
                    _lane_vector(gdn_a_log, SMALL_ALPHA), _lane_vector(gdn_dt_bias, SMALL_ALPHA),
                    gdn_out_norm_gain.reshape(1, GDN_HEAD_DIM).astype(F32), batch, seq)

    seg_width = (CMP_BLOCK // 2) * NSA_KV_WIDTH
    kc3 = kc.reshape(batch, seq // CMP_STRIDE, seg_width)
    vc3 = vc.reshape(batch, seq // CMP_STRIDE, seg_width)
    kcc, vcc = _cmp_call(kc3, vc3, *_cmp_weights(cmp_pos_k, cmp_w1_k, cmp_w2_k),
                         *_cmp_weights(cmp_pos_v, cmp_w1_v, cmp_w2_v))
    o_b = _nsa_call(q, kcc, vcc, kvs, small, _selection_constants(seq), batch, seq)

    out = _out_call(x2d, o_a, o_b, mg,
                    w_branch_gdn.astype(BF16), w_branch_nsa[_q_head_perm(), :].astype(BF16), w_out.astype(BF16),
                    ffn_norm_gain.reshape(1, D_MODEL), w_gate_up.astype(BF16), w_down.astype(BF16),
                    final_norm_gain.reshape(1, D_MODEL))
    return out.reshape(batch, seq, D_MODEL)


def kernel(x, mix_norm_gain, w_in, gdn_conv_w, gdn_a_log, gdn_dt_bias, gdn_out_norm_gain, cmp_pos_k, cmp_w1_k,
           cmp_w2_k, cmp_pos_v, cmp_w1_v, cmp_w2_v, w_branch_gdn, w_branch_nsa, w_out, ffn_norm_gain, w_gate_up,
           w_down, final_norm_gain):
    assert mix_norm_gain.shape[0] == 1, "single-layer block"
    return _hybrid_block(x, mix_norm_gain[0], w_in[0], gdn_conv_w[0], gdn_a_log[0], gdn_dt_bias[0],
                         gdn_out_norm_gain[0], cmp_pos_k[0], cmp_w1_k[0], cmp_w2_k[0], cmp_pos_v[0], cmp_w1_v[0],
                         cmp_w2_v[0], w_branch_gdn[0], w_branch_nsa[0], w_out[0], ffn_norm_gain[0], w_gate_up[0],
                         w_down[0], final_norm_gain)
```

```python
import functools

import numpy as np
import jax
import jax.numpy as jnp
from jax import lax
from jax.experimental import pallas as pl
from jax.experimental.pallas import tpu as pltpu

F32 = jnp.float32
BF16 = jnp.bfloat16

D_MODEL = 1024
NORM_EPS = 1e-6
GDN_HEADS = 4
GDN_HEAD_DIM = 128
GDN_WIDTH = GDN_HEADS * GDN_HEAD_DIM
GDN_CONV = 4
GDN_CHUNK = 64
NSA_HEADS = 8
NSA_KV_GROUPS = 2
NSA_GROUP_HEADS = NSA_HEADS // NSA_KV_GROUPS
NSA_HEAD_DIM = 64
NSA_WIDTH = NSA_HEADS * NSA_HEAD_DIM
NSA_KV_WIDTH = NSA_KV_GROUPS * NSA_HEAD_DIM
CMP_BLOCK = 32
CMP_STRIDE = 16
CMP_HIDDEN = 128
SEL_BLOCK = 64
SEL_COUNT = 16
WINDOW = 512
ROPE_THETA = 500000.0
ROPE_DIM = NSA_HEAD_DIM // 4
FORCED_SCORE = 1000.0
NEG_INF = -1e30
FFN_HIDDEN = 2816

LANES = 128
VMEM_LIMIT = 56 * 1024 * 1024

C_QKV = (0, 1536)
C_Z = (1536, 2048)
C_Q = (2048, 2560)
C_KC = (2560, 2688)
C_VC = (2688, 2816)
C_KVS = (2816, 3456)
C_MG = (3456, 5504)
C_SMALL = (5504, 5632)
IN_WIDTH_PADDED = 5632
SMALL_BETA = 0
SMALL_ALPHA = 4
SMALL_GATE = 8

KV_KS = 0
KV_VS = 256
KV_KW = 512
KV_VW = 640
KV_WIDTH = 896
LOG2E = 1.4426950408889634
Q_SCALE = NSA_HEAD_DIM ** -0.5 * LOG2E

TQ = 128
KB = 512


def _dot(a, b):
    return jnp.dot(a, b, preferred_element_type=F32)


def _dot_nt(a, b):
    return lax.dot_general(a, b, (((1,), (1,)), ((), ())), preferred_element_type=F32)


def _dot_tn(a, b):
    return lax.dot_general(a, b, (((0,), (0,)), ((), ())), preferred_element_type=F32)


def _sigmoid(x):
    return 1.0 / (1.0 + jnp.exp(-x))


def _silu(x):
    return x * _sigmoid(x)


def _rms(x, gain):
    return x * lax.rsqrt(jnp.mean(x * x, axis=-1, keepdims=True) + NORM_EPS) * gain


def _rope_slab(y, c, s1, s2):
    return y * c + pltpu.roll(y, 8, 1) * s1 + pltpu.roll(y, LANES - 8, 1) * s2


def _inproj_kernel(x_ref, gain_ref, w_ref, rc_ref, rs1_ref, rs2_ref, convw_ref,
                   qkv_ref, z_ref, q_ref, kc_ref, vc_ref, kvs_ref, mg_ref, small_ref, xbuf, segbuf, *, seq):
    tm = x_ref.shape[0]
    hb = _rms(x_ref[...], gain_ref[...]).astype(BF16)

    def proj(cols):
        return _dot(hb, w_ref[:, cols[0]:cols[1]])

    first_tile = pl.program_id(0) % (seq // tm) == 0

    @pl.when(first_tile)
    def _():
        xbuf[0:8, :] = jnp.zeros((8, 3 * GDN_WIDTH), F32)

    @pl.when(jnp.logical_not(first_tile))
    def _():
        xbuf[0:8, :] = xbuf[tm:tm + 8, :]

    cw = convw_ref[...]
    slab_w = 2 * GDN_HEAD_DIM

    def gdn_slab(k):
        lo = k * slab_w
        cols = slice(lo, lo + slab_w)
        xbuf[8:8 + tm, cols] = proj((C_QKV[0] + lo, C_QKV[0] + lo + slab_w))
        conv = xbuf[8:8 + tm, cols] * cw[GDN_CONV - 1:GDN_CONV, cols]
        for jj in range(GDN_CONV - 1):
            off = 8 - (GDN_CONV - 1) + jj
            conv = conv + xbuf[off:off + tm, cols] * cw[jj:jj + 1, cols]
        act = _silu(conv)
        for h in range(lo // GDN_HEAD_DIM, (lo + slab_w) // GDN_HEAD_DIM):
            a = act[:, h * GDN_HEAD_DIM - lo:(h + 1) * GDN_HEAD_DIM - lo]
            if h < 2 * GDN_HEADS:
                a = a * lax.rsqrt(jnp.sum(a * a, axis=-1, keepdims=True) + NORM_EPS)
            if h < GDN_HEADS:
                a = a * (GDN_HEAD_DIM ** -0.5)
            qkv_ref[:, h * GDN_HEAD_DIM:(h + 1) * GDN_HEAD_DIM] = a.astype(BF16)

    c, s1, s2 = rc_ref[...], rs1_ref[...], rs2_ref[...]

    def task_z():
        z_ref[...] = proj(C_Z).astype(BF16)

    def task_merge(half):
        lo = C_MG[0] + half * D_MODEL
        mg_ref[:, half * D_MODEL:(half + 1) * D_MODEL] = proj((lo, lo + D_MODEL)).astype(BF16)

    def task_small():
        small_ref[...] = proj(C_SMALL)
        segbuf[0] = proj(C_VC)
        segbuf[1] = _rope_slab(proj(C_KC), c, s1, s2)
        n_seg = tm // CMP_STRIDE
        for which, out_ref in ((0, vc_ref), (1, kc_ref)):
            for l in range(CMP_STRIDE):
                rows = segbuf[which, pl.ds(l, n_seg, stride=CMP_STRIDE), :]
                out_ref[:, l * NSA_KV_WIDTH:(l + 1) * NSA_KV_WIDTH] = rows.astype(BF16)

    def task_q():
        q = proj(C_Q)
        for r in range(NSA_GROUP_HEADS):
            sl = slice(r * LANES, (r + 1) * LANES)
            q_ref[:, sl] = (_rope_slab(q[:, sl], c, s1, s2) * Q_SCALE).astype(BF16)

    def task_kv():
        tok = (pl.program_id(0) % (seq // tm)) * tm + lax.broadcasted_iota(jnp.int32, (tm, 1), 0)
        lane = lax.broadcasted_iota(jnp.int32, (tm, LANES), 1)
        group0 = lane < NSA_HEAD_DIM
        block_onehot = jnp.where(lane == NSA_HEAD_DIM + tok // SEL_BLOCK, 1.0, 0.0)
        kvs = proj(C_KVS)
        for g in range(NSA_KV_GROUPS):
            ks = _rope_slab(kvs[:, g * LANES:(g + 1) * LANES], c, s1, s2)
            kvs_ref[:, KV_KS + g * LANES:KV_KS + (g + 1) * LANES] = jnp.where(group0, ks, block_onehot).astype(BF16)
        vs, kw, vw = kvs[:, 256:384], kvs[:, 384:512], kvs[:, 512:640]
        kvs_ref[:, KV_VS:KV_VS + LANES] = jnp.where(group0, vs, 1.0).astype(BF16)
        kvs_ref[:, KV_VS + LANES:KV_VS + 2 * LANES] = jnp.where(group0, 1.0, vs).astype(BF16)
        kvs_ref[:, KV_KW:KV_KW + LANES] = _rope_slab(kw, c, s1, s2).astype(BF16)
        kvs_ref[:, KV_VW:KV_VW + LANES] = jnp.where(group0, vw, 1.0).astype(BF16)
        kvs_ref[:, KV_VW + LANES:KV_VW + 2 * LANES] = jnp.where(group0, 1.0, vw).astype(BF16)

    others = [task_z, functools.partial(task_merge, 0), task_q, functools.partial(task_merge, 1), task_kv,
              task_small]
    n_slabs = 3 * GDN_WIDTH // slab_w
    assert len(others) == n_slabs
    for k in range(n_slabs):
        gdn_slab(k)
        others[k]()


def _inproj_call(x2d, gain, w_r, rc, rs1, rs2, conv_w, seq, tm=256):
    assert seq // SEL_BLOCK <= LANES, "selection-block one-hot must fit one lane tile"
    t = x2d.shape[0]
    n_seq_tiles = seq // tm
    row = lambda i: (i, 0)
    const = lambda i: (0, 0)
    tab = lambda i: (i % n_seq_tiles, 0)
    seg = CMP_STRIDE * NSA_KV_WIDTH
    outs = [(tm, t, 1536, BF16), (tm, t, 512, BF16), (tm, t, 512, BF16),
            (tm // CMP_STRIDE, t // CMP_STRIDE, seg, BF16), (tm // CMP_STRIDE, t // CMP_STRIDE, seg, BF16),
            (tm, t, KV_WIDTH, BF16), (tm, t, 2048, BF16), (tm, t, 128, F32)]
    return pl.pallas_call(
        functools.partial(_inproj_kernel, seq=seq),
        grid=(t // tm,),
        in_specs=[pl.BlockSpec((tm, D_MODEL), row),
                  pl.BlockSpec((1, D_MODEL), const),
                  pl.BlockSpec((D_MODEL, IN_WIDTH_PADDED), const, pipeline_mode=pl.Buffered(1)),
                  pl.BlockSpec((tm, LANES), tab),
                  pl.BlockSpec((tm, LANES), tab),
                  pl.BlockSpec((tm, LANES), tab),
                  pl.BlockSpec((GDN_CONV, 3 * GDN_WIDTH), const)],
        out_specs=[pl.BlockSpec((r, w), row) for r, _, w, _ in outs],
        out_shape=[jax.ShapeDtypeStruct((n, w), d) for _, n, w, d in outs],
        scratch_shapes=[pltpu.VMEM((tm + 8, 3 * GDN_WIDTH), F32),
                        pltpu.VMEM((2, tm, NSA_KV_WIDTH), F32)],
        compiler_params=pltpu.CompilerParams(dimension_semantics=("arbitrary",),
                                             vmem_limit_bytes=VMEM_LIMIT),
        name="inproj",
    )(x2d, gain, w_r, rc, rs1, rs2, conv_w)


def _gdn_kernel(qkv_ref, z_ref, small_ref, alog_ref, dtb_ref, ogain_ref, o_ref,
                state, u_ref, w_ref, qk_ref, qg_ref, kd_ref, gl_ref, *, blocks_per_seq):
    t = pl.program_id(0)
    bufs = (u_ref, w_ref, qk_ref, qg_ref, kd_ref, gl_ref)

    @pl.when(t == 0)
    def _():
        state[...] = jnp.zeros_like(state)
        for ref in bufs:
            ref[...] = jnp.zeros_like(ref)

    fresh = (t + blocks_per_seq - 1) % blocks_per_seq == 0
    for slot in range(2):
        @pl.when(t % 2 == slot)
        def _(slot=slot):
            _gdn_step(qkv_ref, z_ref, small_ref, alog_ref, dtb_ref, ogain_ref, o_ref, state, bufs, slot, fresh)


def _gdn_step(qkv_ref, z_ref, small_ref, alog_ref, dtb_ref, ogain_ref, o_ref, state, bufs, cur, fresh):
    u_ref, w_ref, qk_ref, qg_ref, kd_ref, gl_ref = bufs
    prev = 1 - cur
    cb = qkv_ref.shape[0]
    c = GDN_CHUNK
    n_chunks = cb // c
    heads = range(GDN_HEADS)

    row = lax.broadcasted_iota(jnp.int32, (c, c), 0)
    col = lax.broadcasted_iota(jnp.int32, (c, c), 1)
    causal = row >= col
    strict = row > col
    tril = causal.astype(F32)
    eye = (row == col).astype(F32)
    neg_decay_rate = -jnp.exp(alog_ref[...])
    dtb = dtb_ref[...]
    ogain = ogain_ref[...]

    pairs = [(ci, h) for ci in range(n_chunks) for h in heads]
    n_pairs = len(pairs)
    A = {}

    def a_prepare():
        beta_c, gc_c, gct_c = [], [], []
        for ci in range(n_chunks):
            sm = small_ref[ci * c:(ci + 1) * c, :]
            beta_c.append(_sigmoid(sm))
            xg = sm + dtb
            softplus = jnp.maximum(xg, 0.0) + jnp.log(1.0 + jnp.exp(-jnp.abs(xg)))
            gc_all = jnp.dot(tril, neg_decay_rate * softplus, precision=lax.Precision.HIGHEST,
                             preferred_element_type=F32)
            gc_c.append(gc_all)
            gct_c.append(gc_all.T)

        def head_cols(base, ci, h):
            lo = base + h * GDN_HEAD_DIM
            return qkv_ref[ci * c:(ci + 1) * c, lo:lo + GDN_HEAD_DIM].astype(F32)

        A["qb"], A["kbf"], A["vb"], A["kbeg"], A["decay"], a_l = [], [], [], [], [], []
        for i, (ci, h) in enumerate(pairs):
            q = head_cols(0, ci, h)
            k = head_cols(GDN_WIDTH, ci, h)
            v = head_cols(2 * GDN_WIDTH, ci, h)
            beta = beta_c[ci][:, SMALL_BETA + h:SMALL_BETA + h + 1]
            gc = gc_c[ci][:, SMALL_ALPHA + h:SMALL_ALPHA + h + 1]
            gr = gct_c[ci][SMALL_ALPHA + h:SMALL_ALPHA + h + 1, :]
            decay = jnp.where(causal, jnp.exp(jnp.where(causal, gc - gr, 0.0)), 0.0)
            eg = jnp.exp(gc)
            g_last = gc[c - 1:c, :]
            kb = k * beta
            kbf = k.astype(BF16)
            qg_ref[cur, i] = (q * eg).astype(BF16)
            kd_ref[cur, i] = (k * jnp.exp(g_last - gc)).astype(BF16)
            gl_ref[cur, i] = jnp.broadcast_to(jnp.exp(g_last), gl_ref.shape[2:])
            A["qb"].append(q.astype(BF16))
            A["kbf"].append(kbf)
            A["vb"].append((v * beta).astype(BF16))
            A["kbeg"].append((kb * eg).astype(BF16))
            A["decay"].append(decay)
            a_l.append(jnp.where(strict, _dot_nt(kb.astype(BF16), kbf) * decay, 0.0))
        A["t"] = [eye - a for a in a_l]
        A["p"] = a_l

    def a_qk():
        for i in range(n_pairs):
            qk_ref[cur, i] = (_dot_nt(A["qb"][i], A["kbf"][i]) * A["decay"][i]).astype(BF16)

    def a_square():
        A["p"] = [_dot(p.astype(BF16), p.astype(BF16)) for p in A["p"]]

    def a_extend():
        A["t"] = [t_ + _dot(t_.astype(BF16), p.astype(BF16)) for t_, p in zip(A["t"], A["p"])]

    def a_u():
        A["tb"] = [t_.astype(BF16) for t_ in A["t"]]
        for i in range(n_pairs):
            u_ref[cur, i] = _dot(A["tb"][i], A["vb"][i])

    def a_w():
        for i in range(n_pairs):
            w_ref[cur, i] = _dot(A["tb"][i], A["kbeg"][i]).astype(BF16)

    n_doublings = c.bit_length() - 2
    a_stages = [a_prepare, a_qk] + [a_square, a_extend] * n_doublings + [a_u, a_w]

    B = {"s": [jnp.where(fresh, 0.0, state[h]) for h in heads]}

    def b_new_values(ci):
        idx = [ci * GDN_HEADS + h for h in heads]
        B["sb"] = [s.astype(BF16) for s in B["s"]]
        B["vn"] = [(u_ref[prev, i] - _dot(w_ref[prev, i], sb)).astype(BF16) for i, sb in zip(idx, B["sb"])]

    def b_output(ci):
        idx = [ci * GDN_HEADS + h for h in heads]
        o_l = [_dot(qg_ref[prev, i], sb) + _dot(qk_ref[prev, i], vn) for i, sb, vn in zip(idx, B["sb"], B["vn"])]
        B["s"] = [s * gl_ref[prev, i][0:1, :] + _dot_tn(kd_ref[prev, i], vn)
                  for i, s, vn in zip(idx, B["s"], B["vn"])]
        for h, o in zip(heads, o_l):
            hs = slice(h * GDN_HEAD_DIM, (h + 1) * GDN_HEAD_DIM)
            zh = z_ref[ci * c:(ci + 1) * c, hs].astype(F32)
            o = o * lax.rsqrt(jnp.mean(o * o, axis=-1, keepdims=True) + NORM_EPS) * ogain * _silu(zh)
            o_ref[ci * c:(ci + 1) * c, hs] = o.astype(BF16)

    b_stages = []
    for ci in range(n_chunks):
        b_stages += [functools.partial(b_new_values, ci), functools.partial(b_output, ci)]

    emitted_b = 0
    for k, stage in enumerate(a_stages):
        stage()
        while emitted_b < len(b_stages) and emitted_b * len(a_stages) < (k + 1) * len(b_stages):
            b_stages[emitted_b]()
            emitted_b += 1
    for h in heads:
        state[h] = B["s"][h]


def _gdn_call(qkv, z, small, alog_v, dtb_v, ogain, batch, seq, cb=256):
    blocks_per_seq = seq // cb
    n_blocks = batch * blocks_per_seq
    n_pairs = (cb // GDN_CHUNK) * GDN_HEADS
    c, d = GDN_CHUNK, GDN_HEAD_DIM
    phase_a_block = lambda t: (jnp.minimum(t, n_blocks - 1), 0)
    phase_b_block = lambda t: (jnp.maximum(t - 1, 0), 0)
    const = lambda t: (0, 0)
    return pl.pallas_call(
        functools.partial(_gdn_kernel, blocks_per_seq=blocks_per_seq),
        grid=(n_blocks + 1,),
        in_specs=[pl.BlockSpec((cb, 3 * GDN_WIDTH), phase_a_block),
                  pl.BlockSpec((cb, GDN_WIDTH), phase_b_block),
                  pl.BlockSpec((cb, LANES), phase_a_block),
                  pl.BlockSpec((1, LANES), const),
                  pl.BlockSpec((1, LANES), const),
                  pl.BlockSpec((1, GDN_HEAD_DIM), const)],
        out_specs=pl.BlockSpec((cb, GDN_WIDTH), phase_b_block),
        out_shape=jax.ShapeDtypeStruct((batch * seq, GDN_WIDTH), BF16),
        scratch_shapes=[pltpu.VMEM((GDN_HEADS, d, d), F32),
                        pltpu.VMEM((2, n_pairs, c, d), F32),
                        pltpu.VMEM((2, n_pairs, c, d), BF16),
                        pltpu.VMEM((2, n_pairs, c, c), BF16),
                        pltpu.VMEM((2, n_pairs, c, d), BF16),
                        pltpu.VMEM((2, n_pairs, c, d), BF16),
                        pltpu.VMEM((2, n_pairs, 8, d), F32)],
        compiler_params=pltpu.CompilerParams(dimension_semantics=("arbitrary",),
                                             vmem_limit_bytes=VMEM_LIMIT),
        name="gdn",
    )(qkv, z, small, alog_v, dtb_v, ogain)


def _cmp_kernel(kc_ref, vc_ref, posk_ref, w1k_ref, w2k_ref, posv_ref, w1v_ref, w2v_ref, kco_ref, vco_ref):
    for x_ref, pos_ref, w1_ref, w2_ref, out_ref in ((kc_ref, posk_ref, w1k_ref, w2k_ref, kco_ref),
                                                    (vc_ref, posv_ref, w1v_ref, w2v_ref, vco_ref)):
        a = x_ref[0].astype(F32)
        n = a.shape[0]
        p0 = _dot((a + pos_ref[0:1, :]).astype(BF16), w1_ref[0])
        p1 = _dot((a + pos_ref[1:2, :]).astype(BF16), w1_ref[1])
        hid = _silu(p0 + pltpu.roll(p1, n - 1, 0))
        out_ref[0] = _dot(hid.astype(BF16), w2_ref[...]).astype(BF16)


def _cmp_call(kc3, vc3, posk, w1k, w2k, posv, w1v, w2v):
    b, n, width = kc3.shape
    hid2 = NSA_KV_GROUPS * CMP_HIDDEN
    per_b = lambda i: (i, 0, 0)
    c2 = lambda i: (0, 0)
    c3 = lambda i: (0, 0, 0)
    wspecs = [pl.BlockSpec((2, width), c2), pl.BlockSpec((2, width, hid2), c3), pl.BlockSpec((hid2, NSA_KV_WIDTH), c2)]
    return pl.pallas_call(
        _cmp_kernel,
        grid=(b,),
        in_specs=[pl.BlockSpec((1, n, width), per_b), pl.BlockSpec((1, n, width), per_b)] + wspecs + wspecs,
        out_specs=[pl.BlockSpec((1, n, NSA_KV_WIDTH), per_b)] * 2,
        out_shape=[jax.ShapeDtypeStruct((b, n, NSA_KV_WIDTH), BF16)] * 2,
        compiler_params=pltpu.CompilerParams(dimension_semantics=("parallel",),
                                             vmem_limit_bytes=VMEM_LIMIT),
        name="nsa_compress",
    )(kc3, vc3, posk, w1k, w2k, posv, w1v, w2v)


def _nsa_kernel(q_ref, kc_ref, vc_ref, kvs_ref, small_ref, ovt_ref, o_ref,
                qa_ref, m_ref, acc_ref, s_ref, part_ref):
    tq = q_ref.shape[0]
    rh = NSA_GROUP_HEADS
    n_cmp = kc_ref.shape[1]
    n_blk = ovt_ref.shape[0]
    qi = pl.program_id(1)
    s0 = qi * tq

    row_half = lax.broadcasted_iota(jnp.int32, (LANES, tq), 0) // NSA_HEAD_DIM
    t_row = s0 + lax.broadcasted_iota(jnp.int32, (1, tq), 1)
    k_col = lax.broadcasted_iota(jnp.int32, (KB, 1), 0)
    q_t = [q_ref[:, r * LANES:(r + 1) * LANES].astype(F32).T for r in range(rh)]
    gates_t = _sigmoid(small_ref[...]).T
    kc = kc_ref[0]
    vc = vc_ref[0]
    cmp_end = lax.broadcasted_iota(jnp.int32, (n_cmp, 1), 0) * CMP_STRIDE + (CMP_BLOCK - 1)
    vis1 = cmp_end <= t_row
    vis = jnp.concatenate([vis1] * rh, axis=1)

    blk = lax.broadcasted_iota(jnp.int32, (n_blk, tq), 0)
    cur = (s0 + lax.broadcasted_iota(jnp.int32, (n_blk, tq), 1)) // SEL_BLOCK
    valid = blk <= cur
    forced = (blk == 0) | (blk == cur) | (blk == cur - 1)

    def tile4(a):
        return jnp.concatenate([a] * rh, axis=1)

    def normalized(acc, g):
        l_row = acc[(1 - g) * NSA_HEAD_DIM:(1 - g) * NSA_HEAD_DIM + 1, :]
        return acc * (1.0 / l_row)

    groups = range(NSA_KV_GROUPS)
    def gate_row(g, branch):
        cols = [SMALL_GATE + (g * rh + r) * 3 + branch for r in range(rh)]
        return jnp.concatenate([gates_t[c:c + 1, :] for c in cols], axis=1)

    qs_l = [jnp.concatenate([jnp.where(row_half == g, q_t[r], 0.0) for r in range(rh)], axis=1).astype(BF16)
            for g in groups]
    span = WINDOW + TQ
    w0 = pl.multiple_of(jnp.maximum(s0 - WINDOW, 0), TQ)
    kw = kvs_ref[pl.ds(w0, span), KV_KW:KV_KW + LANES]
    s_cmp = [_dot(kc, qs_l[g]) for g in groups]
    s_win = [_dot(kw, qs_l[g]) for g in groups]

    def cmp_branch(g):
        s = jnp.where(vis, s_cmp[g], NEG_INF)
        m = jnp.max(s, axis=0, keepdims=True)
        e = jnp.where(vis, jnp.exp2(s - m), 0.0)
        den = jnp.sum(e, axis=0, keepdims=True)
        p = e * (1.0 / jnp.where(den > 0.0, den, 1.0))
        o_cmp = _dot_tn(vc, p.astype(BF16))
        p_sum = p[:, 0:tq]
        for r in range(1, rh):
            p_sum = p_sum + p[:, r * tq:(r + 1) * tq]
        p_hi = p_sum.astype(BF16)
        p_lo = (p_sum - p_hi.astype(F32)).astype(BF16)
        ovt = ovt_ref[...]
        return o_cmp, _dot(ovt, p_hi) + _dot(ovt, p_lo)

    rel = t_row - (w0 + lax.broadcasted_iota(jnp.int32, (span, 1), 0))
    win_bias = tile4(jnp.where((rel >= 0) & (rel < WINDOW), 0.0, NEG_INF))

    def window_branch(g, o_cmp):
        vw = kvs_ref[pl.ds(w0, span), KV_VW + g * LANES:KV_VW + (g + 1) * LANES]
        sw = s_win[g] + win_bias
        pw = jnp.exp2(sw - jnp.max(sw, axis=0, keepdims=True))
        o_win = normalized(_dot_tn(vw, pw.astype(BF16)), g)
        part_ref[g] = gate_row(g, 0) * o_cmp + gate_row(g, 2) * o_win

    sub = lax.broadcasted_iota(jnp.int32, (8, tq), 0)
    n_slab = n_blk // 8

    def select_blocks(g, imp):
        score = jnp.where(valid, jnp.where(forced, FORCED_SCORE, imp), -1.0)
        slabs = [score[8 * v:8 * (v + 1), :] for v in range(n_slab)]
        ranks = [jnp.zeros((8, tq), F32) for _ in range(n_slab)]
        for i in range(n_blk):
            vi, ri = divmod(i, 8)
            si = jnp.broadcast_to(score[i:i + 1, :], (8, tq))
            for v in range(n_slab):
                if v > vi:
                    hit = jnp.where(si >= slabs[v], 1.0, 0.0)
                elif v < vi:
                    hit = jnp.where(si > slabs[v], 1.0, 0.0)
                else:
                    hit = jnp.where(sub > ri, jnp.where(si >= slabs[v], 1.0, 0.0), jnp.where(si > slabs[v], 1.0, 0.0))
                ranks[v] = ranks[v] + hit
        rank = jnp.concatenate(ranks, axis=0)
        sel_bias = jnp.where((rank < float(SEL_COUNT)) & valid, 0.0, NEG_INF)
        if n_blk < NSA_HEAD_DIM:
            sel_bias = jnp.concatenate([sel_bias, jnp.zeros((NSA_HEAD_DIM - n_blk, tq), F32)], axis=0)
        q_g = qs_l[g][g * NSA_HEAD_DIM:(g + 1) * NSA_HEAD_DIM, :]
        qa_ref[g] = jnp.concatenate([q_g, tile4(sel_bias).astype(BF16)], axis=0)

    o_cmp0, imp0 = cmp_branch(0)
    o_cmp1, imp1 = cmp_branch(1)
    select_blocks(0, imp0)
    window_branch(0, o_cmp0)
    select_blocks(1, imp1)
    window_branch(1, o_cmp1)

    kb_diag = s0 // KB

    def scores(kb):
        k0 = pl.multiple_of(kb * KB, KB)
        return [_dot(kvs_ref[pl.ds(k0, KB), KV_KS + g * LANES:KV_KS + (g + 1) * LANES], qa_ref[g])
                for g in groups]

    def accumulate(kb, s_l, first):
        k0 = pl.multiple_of(kb * KB, KB)
        for g in groups:
            va = kvs_ref[pl.ds(k0, KB), KV_VS + g * LANES:KV_VS + (g + 1) * LANES]
            if first:
                m_new = jnp.max(s_l[g], axis=0, keepdims=True)
                acc_ref[g] = _dot_tn(va, jnp.exp2(s_l[g] - m_new).astype(BF16))
            else:
                m_old = m_ref[g]
                m_new = jnp.maximum(m_old, jnp.max(s_l[g], axis=0, keepdims=True))
                acc_ref[g] = (jnp.exp2(m_old - m_new) * acc_ref[g]
                              + _dot_tn(va, jnp.exp2(s_l[g] - m_new).astype(BF16)))
            m_ref[g] = m_new

    causal_bias = tile4(jnp.where(kb_diag * KB + k_col <= t_row, 0.0, NEG_INF))
    s_diag = [s + causal_bias for s in scores(kb_diag)]
    s_next = scores(jnp.maximum(kb_diag - 1, 0))
    accumulate(kb_diag, s_diag, True)
    for g in groups:
        s_ref[g] = s_next[g]

    def sel_body(i, carry):
        kb = kb_diag - i
        s_cur = [s_ref[g] for g in groups]
        s_nxt = scores(jnp.maximum(kb - 1, 0))
        accumulate(kb, s_cur, False)
        for g in groups:
            s_ref[g] = s_nxt[g]
        return carry

    lax.fori_loop(1, kb_diag + 1, sel_body, 0)

    out_t = [part_ref[g] + gate_row(g, 1) * normalized(acc_ref[g], g) for g in groups]
    for r in range(rh):
        cs = slice(r * tq, (r + 1) * tq)
        slab_t = jnp.where(row_half == 0, out_t[0][:, cs], out_t[1][:, cs])
        o_ref[:, r * LANES:(r + 1) * LANES] = slab_t.T.astype(BF16)


def _nsa_call(q, kc, vc, kvs, small, ovt, batch, seq):
    assert seq % KB == 0 and seq >= WINDOW + TQ and seq // SEL_BLOCK <= NSA_HEAD_DIM
    nq = seq // TQ
    n_cmp = kc.shape[1]
    row = lambda b, i: (b * nq + i, 0)
    per_b3 = lambda b, i: (b, 0, 0)
    per_b2 = lambda b, i: (b, 0)
    c2 = lambda b, i: (0, 0)
    rows = NSA_GROUP_HEADS * TQ
    return pl.pallas_call(
        _nsa_kernel,
        grid=(batch, nq),
        in_specs=[pl.BlockSpec((TQ, NSA_WIDTH), row),
                  pl.BlockSpec((1, n_cmp, NSA_KV_WIDTH), per_b3),
                  pl.BlockSpec((1, n_cmp, NSA_KV_WIDTH), per_b3),
                  pl.BlockSpec((seq, KV_WIDTH), per_b2),
                  pl.BlockSpec((TQ, LANES), row),
                  pl.BlockSpec(ovt.shape, c2)],
        out_specs=pl.BlockSpec((TQ, NSA_WIDTH), row),
        out_shape=jax.ShapeDtypeStruct((batch * seq, NSA_WIDTH), BF16),
        scratch_shapes=[pltpu.VMEM((NSA_KV_GROUPS, LANES, rows), BF16),
                        pltpu.VMEM((NSA_KV_GROUPS, 1, rows), F32),
                        pltpu.VMEM((NSA_KV_GROUPS, LANES, rows), F32),
                        pltpu.VMEM((NSA_KV_GROUPS, KB, rows), F32),
                        pltpu.VMEM((NSA_KV_GROUPS, LANES, rows), F32)],
        compiler_params=pltpu.CompilerParams(dimension_semantics=("parallel", "arbitrary"),
                                             vmem_limit_bytes=VMEM_LIMIT),
        name="nsa_attention",
    )(q, kc, vc, kvs, small, ovt)


def _out_kernel(x_ref, oa_ref, ob_ref, mg_ref, wa_ref, wb_ref, wo_ref, g2_ref, wgu_ref, wd_ref, gf_ref, out_ref):
    tm = x_ref.shape[0]
    halves = [slice(i * 128, (i + 1) * 128) for i in range(tm // 128)]
    st = [{} for _ in halves]

    def merge(i, rows):
        a = _dot(oa_ref[rows, :], wa_ref[...])
        b = _dot(ob_ref[rows, :], wb_ref[...])
        st[i]["merged"] = (_sigmoid(mg_ref[rows, 0:D_MODEL].astype(F32)) * a
                           + _sigmoid(mg_ref[rows, D_MODEL:2 * D_MODEL].astype(F32)) * b).astype(BF16)

    def out_proj(i, rows):
        st[i]["x1"] = x_ref[rows, :] + _dot(st[i]["merged"], wo_ref[...])
        st[i]["h2"] = _rms(st[i]["x1"], g2_ref[...]).astype(BF16)

    def ffn_up(i, rows):
        gate = _dot(st[i]["h2"], wgu_ref[:, 0:FFN_HIDDEN])
        up = _dot(st[i]["h2"], wgu_ref[:, FFN_HIDDEN:2 * FFN_HIDDEN])
        st[i]["act"] = (_silu(gate) * up).astype(BF16)

    def ffn_down(i, rows):
        y = st[i]["x1"] + _dot(st[i]["act"], wd_ref[...])
        out_ref[rows, :] = _rms(y, gf_ref[...])

    for stage in (merge, out_proj, ffn_up, ffn_down):
        for i, rows in enumerate(halves):
            stage(i, rows)


def _out_call(x2d, o_a, o_b, mg, wa, wb, wo, g2, wgu, wd, gf, tm=512):
    t = x2d.shape[0]
    row = lambda i: (i, 0)
    const = lambda i: (0, 0)

    def wspec(w):
        return pl.BlockSpec(w.shape, const, pipeline_mode=pl.Buffered(1))

    return pl.pallas_call(
        _out_kernel,
        grid=(t // tm,),
        in_specs=[pl.BlockSpec((tm, D_MODEL), row),
                  pl.BlockSpec((tm, GDN_WIDTH), row),
                  pl.BlockSpec((tm, NSA_WIDTH), row),
                  pl.BlockSpec((tm, 2 * D_MODEL), row),
                  wspec(wa), wspec(wb), wspec(wo),
                  pl.BlockSpec((1, D_MODEL), const),
                  wspec(wgu), wspec(wd),
                  pl.BlockSpec((1, D_MODEL), const)],
        out_specs=pl.BlockSpec((tm, D_MODEL), row),
        out_shape=jax.ShapeDtypeStruct((t, D_MODEL), F32),
        compiler_params=pltpu.CompilerParams(dimension_semantics=("parallel",),
                                             vmem_limit_bytes=VMEM_LIMIT),
        name="merge_ffn",
    )(x2d, o_a, o_b, mg, wa, wb, wo, g2, wgu, wd, gf)


def _q_head_perm():
    idx = []
    for r in range(NSA_GROUP_HEADS):
        for g in range(NSA_KV_GROUPS):
            base = (g * NSA_GROUP_HEADS + r) * NSA_HEAD_DIM
            idx.extend(range(base, base + NSA_HEAD_DIM))
    return np.asarray(idx, dtype=np.int32)


def _reorder_w_in(w_in):
    sizes = (3 * GDN_WIDTH, GDN_WIDTH, GDN_HEADS, GDN_HEADS, NSA_WIDTH) + (NSA_KV_WIDTH,) * 6 + (
        3 * NSA_HEADS, D_MODEL, D_MODEL)
    offs = np.concatenate([[0], np.cumsum(sizes)])
    seg = lambda i: w_in[:, offs[i]:offs[i + 1]]
    qkv, z, beta, alpha, q, kc, vc, ks, vs, kw, vw, gate, ma, mb = [seg(i) for i in range(14)]
    pad = jnp.zeros((D_MODEL, LANES - 2 * GDN_HEADS - 3 * NSA_HEADS), w_in.dtype)
    q = q[:, _q_head_perm()]
    zero_half = jnp.zeros((D_MODEL, NSA_HEAD_DIM), w_in.dtype)
    ks_split = [ks[:, :NSA_HEAD_DIM], zero_half, ks[:, NSA_HEAD_DIM:], zero_half]
    return jnp.concatenate([qkv, z, q, kc, vc] + ks_split + [vs, kw, vw, ma, mb, beta, alpha, gate, pad],
                           axis=1).astype(BF16)


def _rope_tables(seq):
    half = ROPE_DIM // 2
    inv_freq = ROPE_THETA ** (-jnp.arange(half, dtype=F32) / half)
    ang = jnp.arange(seq, dtype=F32)[:, None] * inv_freq
    cos, sin = jnp.cos(ang), jnp.sin(ang)
    ones = jnp.ones((seq, NSA_HEAD_DIM - ROPE_DIM), F32)
    zeros = jnp.zeros((seq, NSA_HEAD_DIM - ROPE_DIM), F32)
    z8 = jnp.zeros((seq, half), F32)
    c = jnp.concatenate([cos, cos, ones], axis=1)
    s1 = jnp.concatenate([z8, sin, zeros], axis=1)
    s2 = jnp.concatenate([-sin, z8, zeros], axis=1)
    tile2 = lambda a: jnp.concatenate([a, a], axis=1)
    return tile2(c), tile2(s1), tile2(s2)


def _cmp_weights(pos, w1, w2):
    g = NSA_KV_GROUPS
    seg = CMP_BLOCK // 2
    eye = jnp.eye(g, dtype=w1.dtype)
    pos_h = pos.reshape(2, seg, 1, NSA_HEAD_DIM)
    pos_flat = jnp.broadcast_to(pos_h, (2, seg, g, NSA_HEAD_DIM)).reshape(2, seg * g * NSA_HEAD_DIM)
    w1_h = w1.reshape(2, seg, NSA_HEAD_DIM, CMP_HIDDEN)
    w1_bd = jnp.einsum('alds,gk->algdks', w1_h, eye).reshape(2, seg * g * NSA_HEAD_DIM, g * CMP_HIDDEN)
    w2_bd = jnp.einsum('hd,gk->ghkd', w2, eye).reshape(g * CMP_HIDDEN, g * NSA_HEAD_DIM)
    return pos_flat.astype(F32), w1_bd.astype(BF16), w2_bd.astype(BF16)


def _selection_constants(seq):
    n_cmp_pad = seq // CMP_STRIDE
    n_blk = seq // SEL_BLOCK
    cmp_start = np.arange(n_cmp_pad) * CMP_STRIDE
    cmp_end = cmp_start + CMP_BLOCK - 1
    sel_start = np.arange(n_blk) * SEL_BLOCK
    ovt = ((cmp_start[None, :] <= sel_start[:, None] + SEL_BLOCK - 1) & (cmp_end[None, :] >= sel_start[:, None]))
    ovt[:, n_cmp_pad - 1] = False
    return jnp.asarray(ovt, BF16)


def _lane_vector(values, offset):
    v = jnp.zeros((1, LANES), F32)
    return v.at[0, offset:offset + values.shape[0]].set(values.astype(F32))


def _hybrid_block(x, mix_norm_gain, w_in, gdn_conv_w, gdn_a_log, gdn_dt_bias, gdn_out_norm_gain,
                  cmp_pos_k, cmp_w1_k, cmp_w2_k, cmp_pos_v, cmp_w1_v, cmp_w2_v,
                  w_branch_gdn, w_branch_nsa, w_out, ffn_norm_gain, w_gate_up, w_down, final_norm_gain):
    batch, seq, _ = x.shape
    x2d = x.reshape(batch * seq, D_MODEL)
    rc, rs1, rs2 = _rope_tables(seq)
    qkv, z, q, kc, vc, kvs, mg, small = _inproj_call(
        x2d, mix_norm_gain.reshape(1, D_MODEL), _reorder_w_in(w_in), rc, rs1, rs2,
        gdn_conv_w.astype(F32), seq)

    o_a = _gdn_call(qkv, z, small,
                    _lane_vector(gdn_a_log, SMALL_ALPHA), _lane_vector(gdn_dt_bias, SMALL_ALPHA),
                    gdn_out_norm_gain.reshape(1, GDN_HEAD_DIM).astype(F32), batch, seq)

    seg_width = (CMP_BLOCK // 2) * NSA_KV_WIDTH
    kc3 = kc.reshape(batch, seq // CMP_STRIDE, seg_width)
    vc3 = vc.reshape(batch, seq // CMP_STRIDE, seg_width)
    kcc, vcc = _cmp_call(kc3, vc3, *_cmp_weights(cmp_pos_k, cmp_w1_k, cmp_w2_k),
                         *_cmp_weights(cmp_pos_v, cmp_w1_v, cmp_w2_v))
    o_b = _nsa_call(q, kcc, vcc, kvs, small, _selection_constants(seq), batch, seq)

    out = _out_call(x2d, o_a, o_b, mg,
                    w_branch_gdn.astype(BF16), w_branch_nsa[_q_head_perm(), :].astype(BF16), w_out.astype(BF16),
                    ffn_norm_gain.reshape(1, D_MODEL), w_gate_up.astype(BF16), w_down.astype(BF16),
                    final_norm_gain.reshape(1, D_MODEL))
    return out.reshape(batch, seq, D_MODEL)


def kernel(x, mix_norm_gain, w_in, gdn_conv_w, gdn_a_log, gdn_dt_bias, gdn_out_norm_gain, cmp_pos_k, cmp_w1_k,
           cmp_w2_k, cmp_pos_v, cmp_w1_v, cmp_w2_v, w_branch_gdn, w_branch_nsa, w_out, ffn_norm_gain, w_gate_up,
           w_down, final_norm_gain):
    assert mix_norm_gain.shape[0] == 1, "single-layer block"
    return _hybrid_block(x, mix_norm_gain[0], w_in[0], gdn_conv_w[0], gdn_a_log[0], gdn_dt_bias[0],
                         gdn_out_norm_gain[0], cmp_pos_k[0], cmp_w1_k[0], cmp_w2_k[0], cmp_pos_v[0], cmp_w1_v[0],
                         cmp_w2_v[0], w_branch_gdn[0], w_branch_nsa[0], w_out[0], ffn_norm_gain[0], w_gate_up[0],
                         w_down[0], final_norm_gain)
```

```python
import functools

import numpy as np
import jax
import jax.numpy as jnp
from jax import lax
from jax.experimental import pallas as pl
from jax.experimental.pallas import tpu as pltpu

F32 = jnp.float32
BF16 = jnp.bfloat16

D_MODEL = 1024
NORM_EPS = 1e-6
GDN_HEADS = 4
GDN_HEAD_DIM = 128
GDN_WIDTH = GDN_HEADS * GDN_HEAD_DIM
GDN_CONV = 4
GDN_CHUNK = 64
INV_BASE = 8
NSA_HEADS = 8
NSA_KV_GROUPS = 2
NSA_GROUP_HEADS = NSA_HEADS // NSA_KV_GROUPS
NSA_HEAD_DIM = 64
NSA_WIDTH = NSA_HEADS * NSA_HEAD_DIM
NSA_KV_WIDTH = NSA_KV_GROUPS * NSA_HEAD_DIM
CMP_BLOCK = 32
CMP_STRIDE = 16
CMP_HIDDEN = 128
SEL_BLOCK = 64
SEL_COUNT = 16
WINDOW = 512
ROPE_THETA = 500000.0
ROPE_DIM = NSA_HEAD_DIM // 4
FORCED_SCORE = 1000.0
NEG_INF = -1e30
FFN_HIDDEN = 2816

LANES = 128
VMEM_LIMIT = 56 * 1024 * 1024

C_QKV = (0, 1536)
C_Z = (1536, 2048)
C_Q = (2048, 2560)
C_KC = (2560, 2688)
C_VC = (2688, 2816)
C_KVS = (2816, 3456)
C_MG = (3456, 5504)
C_SMALL = (5504, 5632)
IN_WIDTH_PADDED = 5632
SMALL_BETA = 0
SMALL_ALPHA = 4
SMALL_GATE = 8

KV_KS = 0
KV_VS = 256
KV_KW = 512
KV_VW = 640
KV_WIDTH = 896
LOG2E = 1.4426950408889634
Q_SCALE = NSA_HEAD_DIM ** -0.5 * LOG2E

TQ = 128
KB = 512


def _dot(a, b):
    return jnp.dot(a, b, preferred_element_type=F32)


def _dot_nt(a, b):
    return lax.dot_general(a, b, (((1,), (1,)), ((), ())), preferred_element_type=F32)


def _dot_tn(a, b):
    return lax.dot_general(a, b, (((0,), (0,)), ((), ())), preferred_element_type=F32)


def _sigmoid(x):
    return 1.0 / (1.0 + jnp.exp(-x))


def _silu(x):
    return x * _sigmoid(x)


def _rms(x, gain):
    return x * lax.rsqrt(jnp.mean(x * x, axis=-1, keepdims=True) + NORM_EPS) * gain


def _rope_slab(y, c, s1, s2):
    return y * c + pltpu.roll(y, 8, 1) * s1 + pltpu.roll(y, LANES - 8, 1) * s2


def _inproj_kernel(x_ref, gain_ref, w_ref, rc_ref, rs1_ref, rs2_ref, convw_ref,
                   qkv_ref, z_ref, q_ref, kc_ref, vc_ref, kvs_ref, mg_ref, small_ref, xbuf, segbuf, *, seq):
    tm = x_ref.shape[0]
    hb = _rms(x_ref[...], gain_ref[...]).astype(BF16)

    def proj(cols):
        return _dot(hb, w_ref[:, cols[0]:cols[1]])

    first_tile = pl.program_id(0) % (seq // tm) == 0

    @pl.when(first_tile)
    def _():
        xbuf[0:8, :] = jnp.zeros((8, 3 * GDN_WIDTH), F32)

    @pl.when(jnp.logical_not(first_tile))
    def _():
        xbuf[0:8, :] = xbuf[tm:tm + 8, :]

    cw = convw_ref[...]
    slab_w = 2 * GDN_HEAD_DIM

    def gdn_slab(k):
        lo = k * slab_w
        cols = slice(lo, lo + slab_w)
        xbuf[8:8 + tm, cols] = proj((C_QKV[0] + lo, C_QKV[0] + lo + slab_w))
        conv = xbuf[8:8 + tm, cols] * cw[GDN_CONV - 1:GDN_CONV, cols]
        for jj in range(GDN_CONV - 1):
            off = 8 - (GDN_CONV - 1) + jj
            conv = conv + xbuf[off:off + tm, cols] * cw[jj:jj + 1, cols]
        act = _silu(conv)
        for h in range(lo // GDN_HEAD_DIM, (lo + slab_w) // GDN_HEAD_DIM):
            a = act[:, h * GDN_HEAD_DIM - lo:(h + 1) * GDN_HEAD_DIM - lo]
            if h < 2 * GDN_HEADS:
                a = a * lax.rsqrt(jnp.sum(a * a, axis=-1, keepdims=True) + NORM_EPS)
            if h < GDN_HEADS:
                a = a * (GDN_HEAD_DIM ** -0.5)
            qkv_ref[:, h * GDN_HEAD_DIM:(h + 1) * GDN_HEAD_DIM] = a.astype(BF16)

    c, s1, s2 = rc_ref[...], rs1_ref[...], rs2_ref[...]

    def task_z():
        z_ref[...] = proj(C_Z).astype(BF16)

    def task_merge(half):
        lo = C_MG[0] + half * D_MODEL
        mg_ref[:, half * D_MODEL:(half + 1) * D_MODEL] = proj((lo, lo + D_MODEL)).astype(BF16)

    def task_small():
        small_ref[...] = proj(C_SMALL)
        segbuf[0] = proj(C_VC)
        segbuf[1] = _rope_slab(proj(C_KC), c, s1, s2)
        n_seg = tm // CMP_STRIDE
        for which, out_ref in ((0, vc_ref), (1, kc_ref)):
            for l in range(CMP_STRIDE):
                rows = segbuf[which, pl.ds(l, n_seg, stride=CMP_STRIDE), :]
                out_ref[:, l * NSA_KV_WIDTH:(l + 1) * NSA_KV_WIDTH] = rows.astype(BF16)

    def task_q():
        q = proj(C_Q)
        for r in range(NSA_GROUP_HEADS):
            sl = slice(r * LANES, (r + 1) * LANES)
            q_ref[:, sl] = (_rope_slab(q[:, sl], c, s1, s2) * Q_SCALE).astype(BF16)

    def task_kv():
        tok = (pl.program_id(0) % (seq // tm)) * tm + lax.broadcasted_iota(jnp.int32, (tm, 1), 0)
        lane = lax.broadcasted_iota(jnp.int32, (tm, LANES), 1)
        group0 = lane < NSA_HEAD_DIM
        block_onehot = jnp.where(lane == NSA_HEAD_DIM + tok // SEL_BLOCK, 1.0, 0.0)
        kvs = proj(C_KVS)
        for g in range(NSA_KV_GROUPS):
            ks = _rope_slab(kvs[:, g * LANES:(g + 1) * LANES], c, s1, s2)
            kvs_ref[:, KV_KS + g * LANES:KV_KS + (g + 1) * LANES] = jnp.where(group0, ks, block_onehot).astype(BF16)
        vs, kw, vw = kvs[:, 256:384], kvs[:, 384:512], kvs[:, 512:640]
        kvs_ref[:, KV_VS:KV_VS + LANES] = jnp.where(group0, vs, 1.0).astype(BF16)
        kvs_ref[:, KV_VS + LANES:KV_VS + 2 * LANES] = jnp.where(group0, 1.0, vs).astype(BF16)
        kvs_ref[:, KV_KW:KV_KW + LANES] = _rope_slab(kw, c, s1, s2).astype(BF16)
        kvs_ref[:, KV_VW:KV_VW + LANES] = jnp.where(group0, vw, 1.0).astype(BF16)
        kvs_ref[:, KV_VW + LANES:KV_VW + 2 * LANES] = jnp.where(group0, 1.0, vw).astype(BF16)

    others = [task_z, functools.partial(task_merge, 0), task_q, functools.partial(task_merge, 1), task_kv,
              task_small]
    n_slabs = 3 * GDN_WIDTH // slab_w
    assert len(others) == n_slabs
    for k in range(n_slabs):
        gdn_slab(k)
        others[k]()


def _inproj_call(x2d, gain, w_r, rc, rs1, rs2, conv_w, seq, tm=512):
    assert seq // SEL_BLOCK <= LANES, "selection-block one-hot must fit one lane tile"
    t = x2d.shape[0]
    n_seq_tiles = seq // tm
    row = lambda i: (i, 0)
    const = lambda i: (0, 0)
    tab = lambda i: (i % n_seq_tiles, 0)
    seg = CMP_STRIDE * NSA_KV_WIDTH
    outs = [(tm, t, 1536, BF16), (tm, t, 512, BF16), (tm, t, 512, BF16),
            (tm // CMP_STRIDE, t // CMP_STRIDE, seg, BF16), (tm // CMP_STRIDE, t // CMP_STRIDE, seg, BF16),
            (tm, t, KV_WIDTH, BF16), (tm, t, 2048, BF16), (tm, t, 128, F32)]
    return pl.pallas_call(
        functools.partial(_inproj_kernel, seq=seq),
        grid=(t // tm,),
        in_specs=[pl.BlockSpec((tm, D_MODEL), row),
                  pl.BlockSpec((1, D_MODEL), const),
                  pl.BlockSpec((D_MODEL, IN_WIDTH_PADDED), const, pipeline_mode=pl.Buffered(1)),
                  pl.BlockSpec((tm, LANES), tab),
                  pl.BlockSpec((tm, LANES), tab),
                  pl.BlockSpec((tm, LANES), tab),
                  pl.BlockSpec((GDN_CONV, 3 * GDN_WIDTH), const)],
        out_specs=[pl.BlockSpec((r, w), row) for r, _, w, _ in outs],
        out_shape=[jax.ShapeDtypeStruct((n, w), d) for _, n, w, d in outs],
        scratch_shapes=[pltpu.VMEM((tm + 8, 3 * GDN_WIDTH), F32),
                        pltpu.VMEM((2, tm, NSA_KV_WIDTH), F32)],
        compiler_params=pltpu.CompilerParams(dimension_semantics=("arbitrary",),
                                             vmem_limit_bytes=VMEM_LIMIT),
        name="inproj",
    )(x2d, gain, w_r, rc, rs1, rs2, conv_w)


def _gdn_kernel(qkv_ref, z_ref, small_ref, alog_ref, dtb_ref, ogain_ref, o_ref,
                state, u_ref, w_ref, qk_ref, qg_ref, kd_ref, gl_ref, *, blocks_per_seq):
    t = pl.program_id(0)
    bufs = (u_ref, w_ref, qk_ref, qg_ref, kd_ref, gl_ref)

    @pl.when(t == 0)
    def _():
        state[...] = jnp.zeros_like(state)
        for ref in bufs:
            ref[...] = jnp.zeros_like(ref)

    fresh = (t + blocks_per_seq - 1) % blocks_per_seq == 0
    for slot in range(2):
        @pl.when(t % 2 == slot)
        def _(slot=slot):
            _gdn_step(qkv_ref, z_ref, small_ref, alog_ref, dtb_ref, ogain_ref, o_ref, state, bufs, slot, fresh)


def _gdn_step(qkv_ref, z_ref, small_ref, alog_ref, dtb_ref, ogain_ref, o_ref, state, bufs, cur, fresh):
    u_ref, w_ref, qk_ref, qg_ref, kd_ref, gl_ref = bufs
    prev = 1 - cur
    cb = qkv_ref.shape[0]
    c = GDN_CHUNK
    n_chunks = cb // c
    heads = range(GDN_HEADS)

    row = lax.broadcasted_iota(jnp.int32, (c, c), 0)
    col = lax.broadcasted_iota(jnp.int32, (c, c), 1)
    causal = row >= col
    strict = row > col
    tril = causal.astype(F32)
    eye = (row == col).astype(F32)
    neg_decay_rate = -jnp.exp(alog_ref[...])
    dtb = dtb_ref[...]
    ogain = ogain_ref[...]

    pairs = [(ci, h) for ci in range(n_chunks) for h in heads]
    n_pairs = len(pairs)
    A = {}

    def a_prepare():
        beta_c, gc_c, gct_c = [], [], []
        for ci in range(n_chunks):
            sm = small_ref[ci * c:(ci + 1) * c, :]
            beta_c.append(_sigmoid(sm))
            xg = sm + dtb
            softplus = jnp.maximum(xg, 0.0) + jnp.log(1.0 + jnp.exp(-jnp.abs(xg)))
            gc_all = jnp.dot(tril, neg_decay_rate * softplus, precision=lax.Precision.HIGHEST,
                             preferred_element_type=F32)
            gc_c.append(gc_all)
            gct_c.append(gc_all.T)

        def head_cols(base, ci, h):
            lo = base + h * GDN_HEAD_DIM
            return qkv_ref[ci * c:(ci + 1) * c, lo:lo + GDN_HEAD_DIM].astype(F32)

        A["qb"], A["kbf"], A["vb"], A["kbeg"], A["decay"], a_l = [], [], [], [], [], []
        for i, (ci, h) in enumerate(pairs):
            q = head_cols(0, ci, h)
            k = head_cols(GDN_WIDTH, ci, h)
            v = head_cols(2 * GDN_WIDTH, ci, h)
            beta = beta_c[ci][:, SMALL_BETA + h:SMALL_BETA + h + 1]
            gc = gc_c[ci][:, SMALL_ALPHA + h:SMALL_ALPHA + h + 1]
            gr = gct_c[ci][SMALL_ALPHA + h:SMALL_ALPHA + h + 1, :]
            decay = jnp.where(causal, jnp.exp(jnp.where(causal, gc - gr, 0.0)), 0.0)
            eg = jnp.exp(gc)
            g_last = gc[c - 1:c, :]
            kb = k * beta
            kbf = k.astype(BF16)
            qg_ref[cur, i] = (q * eg).astype(BF16)
            kd_ref[cur, i] = (k * jnp.exp(g_last - gc)).astype(BF16)
            gl_ref[cur, i] = jnp.broadcast_to(jnp.exp(g_last), gl_ref.shape[2:])
            A["qb"].append(q.astype(BF16))
            A["kbf"].append(kbf)
            A["vb"].append((v * beta).astype(BF16))
            A["kbeg"].append((kb * eg).astype(BF16))
            A["decay"].append(decay)
            a_l.append(jnp.where(strict, _dot_nt(kb.astype(BF16), kbf) * decay, 0.0))
        A["a"] = a_l
        diag_blocks = [jnp.where(row // INV_BASE == col // INV_BASE, a, 0.0) for a in a_l]
        A["t"] = [eye - d for d in diag_blocks]
        A["p"] = diag_blocks

    def a_qk():
        for i in range(n_pairs):
            qk_ref[cur, i] = (_dot_nt(A["qb"][i], A["kbf"][i]) * A["decay"][i]).astype(BF16)

    def a_square():
        A["p"] = [_dot(p.astype(BF16), p.astype(BF16)) for p in A["p"]]

    def a_extend():
        A["t"] = [t_ + _dot(t_.astype(BF16), p.astype(BF16)) for t_, p in zip(A["t"], A["p"])]

    def a_u():
        A["tb"] = [t_.astype(BF16) for t_ in A["t"]]
        for i in range(n_pairs):
            u_ref[cur, i] = _dot(A["tb"][i], A["vb"][i])

    def a_w():
        for i in range(n_pairs):
            w_ref[cur, i] = _dot(A["tb"][i], A["kbeg"][i]).astype(BF16)

    def a_link(size):
        link = (row // (2 * size) == col // (2 * size)) & ((row // size) % 2 == 1) & ((col // size) % 2 == 0)
        A["dn"] = [_dot(t_.astype(BF16), jnp.where(link, a, 0.0).astype(BF16)) for t_, a in zip(A["t"], A["a"])]

    def a_merge():
        A["t"] = [t_ - _dot(dn.astype(BF16), t_.astype(BF16)) for t_, dn in zip(A["t"], A["dn"])]

    a_stages = [a_prepare, a_qk] + [a_square, a_extend] * (INV_BASE.bit_length() - 2)
    size = INV_BASE
    while size < c:
        a_stages += [functools.partial(a_link, size), a_merge]
        size *= 2
    a_stages += [a_u, a_w]

    B = {"s": [jnp.where(fresh, 0.0, state[h]) for h in heads]}

    def b_new_values(ci):
        idx = [ci * GDN_HEADS + h for h in heads]
        B["sb"] = [s.astype(BF16) for s in B["s"]]
        B["vn"] = [(u_ref[prev, i] - _dot(w_ref[prev, i], sb)).astype(BF16) for i, sb in zip(idx, B["sb"])]

    def b_output(ci):
        idx = [ci * GDN_HEADS + h for h in heads]
        o_l = [_dot(qg_ref[prev, i], sb) + _dot(qk_ref[prev, i], vn) for i, sb, vn in zip(idx, B["sb"], B["vn"])]
        B["s"] = [s * gl_ref[prev, i][0:1, :] + _dot_tn(kd_ref[prev, i], vn)
                  for i, s, vn in zip(idx, B["s"], B["vn"])]
        for h, o in zip(heads, o_l):
            hs = slice(h * GDN_HEAD_DIM, (h + 1) * GDN_HEAD_DIM)
            zh = z_ref[ci * c:(ci + 1) * c, hs].astype(F32)
            o = o * lax.rsqrt(jnp.mean(o * o, axis=-1, keepdims=True) + NORM_EPS) * ogain * _silu(zh)
            o_ref[ci * c:(ci + 1) * c, hs] = o.astype(BF16)

    b_stages = []
    for ci in range(n_chunks):
        b_stages += [functools.partial(b_new_values, ci), functools.partial(b_output, ci)]

    emitted_b = 0
    for k, stage in enumerate(a_stages):
        stage()
        while emitted_b < len(b_stages) and emitted_b * len(a_stages) < (k + 1) * len(b_stages):
            b_stages[emitted_b]()
            emitted_b += 1
    for h in heads:
        state[h] = B["s"][h]


def _gdn_call(qkv, z, small, alog_v, dtb_v, ogain, batch, seq, cb=256):
    blocks_per_seq = seq // cb
    n_blocks = batch * blocks_per_seq
    n_pairs = (cb // GDN_CHUNK) * GDN_HEADS
    c, d = GDN_CHUNK, GDN_HEAD_DIM
    phase_a_block = lambda t: (jnp.minimum(t, n_blocks - 1), 0)
    phase_b_block = lambda t: (jnp.maximum(t - 1, 0), 0)
    const = lambda t: (0, 0)
    return pl.pallas_call(
        functools.partial(_gdn_kernel, blocks_per_seq=blocks_per_seq),
        grid=(n_blocks + 1,),
        in_specs=[pl.BlockSpec((cb, 3 * GDN_WIDTH), phase_a_block),
                  pl.BlockSpec((cb, GDN_WIDTH), phase_b_block),
                  pl.BlockSpec((cb, LANES), phase_a_block),
                  pl.BlockSpec((1, LANES), const),
                  pl.BlockSpec((1, LANES), const),
                  pl.BlockSpec((1, GDN_HEAD_DIM), const)],
        out_specs=pl.BlockSpec((cb, GDN_WIDTH), phase_b_block),
        out_shape=jax.ShapeDtypeStruct((batch * seq, GDN_WIDTH), BF16),
        scratch_shapes=[pltpu.VMEM((GDN_HEADS, d, d), F32),
                        pltpu.VMEM((2, n_pairs, c, d), F32),
                        pltpu.VMEM((2, n_pairs, c, d), BF16),
                        pltpu.VMEM((2, n_pairs, c, c), BF16),
                        pltpu.VMEM((2, n_pairs, c, d), BF16),
                        pltpu.VMEM((2, n_pairs, c, d), BF16),
                        pltpu.VMEM((2, n_pairs, 8, d), F32)],
        compiler_params=pltpu.CompilerParams(dimension_semantics=("arbitrary",),
                                             vmem_limit_bytes=VMEM_LIMIT),
        name="gdn",
    )(qkv, z, small, alog_v, dtb_v, ogain)


def _cmp_kernel(kc_ref, vc_ref, posk_ref, w1k_ref, w2k_ref, posv_ref, w1v_ref, w2v_ref, kco_ref, vco_ref):
    for x_ref, pos_ref, w1_ref, w2_ref, out_ref in ((kc_ref, posk_ref, w1k_ref, w2k_ref, kco_ref),
                                                    (vc_ref, posv_ref, w1v_ref, w2v_ref, vco_ref)):
        a = x_ref[0].astype(F32)
        n = a.shape[0]
        p0 = _dot((a + pos_ref[0:1, :]).astype(BF16), w1_ref[0])
        p1 = _dot((a + pos_ref[1:2, :]).astype(BF16), w1_ref[1])
        hid = _silu(p0 + pltpu.roll(p1, n - 1, 0))
        out_ref[0] = _dot(hid.astype(BF16), w2_ref[...]).astype(BF16)


def _cmp_call(kc3, vc3, posk, w1k, w2k, posv, w1v, w2v):
    b, n, width = kc3.shape
    hid2 = NSA_KV_GROUPS * CMP_HIDDEN
    per_b = lambda i: (i, 0, 0)
    c2 = lambda i: (0, 0)
    c3 = lambda i: (0, 0, 0)
    wspecs = [pl.BlockSpec((2, width), c2), pl.BlockSpec((2, width, hid2), c3), pl.BlockSpec((hid2, NSA_KV_WIDTH), c2)]
    return pl.pallas_call(
        _cmp_kernel,
        grid=(b,),
        in_specs=[pl.BlockSpec((1, n, width), per_b), pl.BlockSpec((1, n, width), per_b)] + wspecs + wspecs,
        out_specs=[pl.BlockSpec((1, n, NSA_KV_WIDTH), per_b)] * 2,
        out_shape=[jax.ShapeDtypeStruct((b, n, NSA_KV_WIDTH), BF16)] * 2,
        compiler_params=pltpu.CompilerParams(dimension_semantics=("parallel",),
                                             vmem_limit_bytes=VMEM_LIMIT),
        name="nsa_compress",
    )(kc3, vc3, posk, w1k, w2k, posv, w1v, w2v)


def _nsa_kernel(q_ref, kc_ref, vc_ref, kvs_ref, small_ref, ovt_ref, o_ref,
                qa_ref, m_ref, acc_ref, s_ref, part_ref):
    tq = q_ref.shape[0]
    rh = NSA_GROUP_HEADS
    n_cmp = kc_ref.shape[1]
    n_blk = ovt_ref.shape[0]
    qi = pl.program_id(1)
    s0 = qi * tq

    row_half = lax.broadcasted_iota(jnp.int32, (LANES, tq), 0) // NSA_HEAD_DIM
    t_row = s0 + lax.broadcasted_iota(jnp.int32, (1, tq), 1)
    k_col = lax.broadcasted_iota(jnp.int32, (KB, 1), 0)
    q_t = [q_ref[:, r * LANES:(r + 1) * LANES].astype(F32).T for r in range(rh)]
    gates_t = _sigmoid(small_ref[...]).T
    kc = kc_ref[0]
    vc = vc_ref[0]
    cmp_end = lax.broadcasted_iota(jnp.int32, (n_cmp, 1), 0) * CMP_STRIDE + (CMP_BLOCK - 1)
    vis1 = cmp_end <= t_row
    vis = jnp.concatenate([vis1] * rh, axis=1)

    blk = lax.broadcasted_iota(jnp.int32, (n_blk, tq), 0)
    cur = (s0 + lax.broadcasted_iota(jnp.int32, (n_blk, tq), 1)) // SEL_BLOCK
    valid = blk <= cur
    forced = (blk == 0) | (blk == cur) | (blk == cur - 1)

    def tile4(a):
        return jnp.concatenate([a] * rh, axis=1)

    def normalized(acc, g):
        l_row = acc[(1 - g) * NSA_HEAD_DIM:(1 - g) * NSA_HEAD_DIM + 1, :]
        return acc * (1.0 / l_row)

    groups = range(NSA_KV_GROUPS)
    def gate_row(g, branch):
        cols = [SMALL_GATE + (g * rh + r) * 3 + branch for r in range(rh)]
        return jnp.concatenate([gates_t[c:c + 1, :] for c in cols], axis=1)

    qs_l = [jnp.concatenate([jnp.where(row_half == g, q_t[r], 0.0) for r in range(rh)], axis=1).astype(BF16)
            for g in groups]
    span = WINDOW + TQ
    w0 = pl.multiple_of(jnp.maximum(s0 - WINDOW, 0), TQ)
    kw = kvs_ref[pl.ds(w0, span), KV_KW:KV_KW + LANES]
    s_cmp = [_dot(kc, qs_l[g]) for g in groups]
    s_win = [_dot(kw, qs_l[g]) for g in groups]

    def cmp_branch(g):
        s = jnp.where(vis, s_cmp[g], NEG_INF)
        m = jnp.max(s, axis=0, keepdims=True)
        e = jnp.where(vis, jnp.exp2(s - m), 0.0)
        den = jnp.sum(e, axis=0, keepdims=True)
        p = e * (1.0 / jnp.where(den > 0.0, den, 1.0))
        o_cmp = _dot_tn(vc, p.astype(BF16))
        p_sum = p[:, 0:tq]
        for r in range(1, rh):
            p_sum = p_sum + p[:, r * tq:(r + 1) * tq]
        p_hi = p_sum.astype(BF16)
        p_lo = (p_sum - p_hi.astype(F32)).astype(BF16)
        ovt = ovt_ref[...]
        return o_cmp, _dot(ovt, p_hi) + _dot(ovt, p_lo)

    rel = t_row - (w0 + lax.broadcasted_iota(jnp.int32, (span, 1), 0))
    win_bias = tile4(jnp.where((rel >= 0) & (rel < WINDOW), 0.0, NEG_INF))

    def window_branch(g, o_cmp):
        vw = kvs_ref[pl.ds(w0, span), KV_VW + g * LANES:KV_VW + (g + 1) * LANES]
        sw = s_win[g] + win_bias
        pw = jnp.exp2(sw - jnp.max(sw, axis=0, keepdims=True))
        o_win = normalized(_dot_tn(vw, pw.astype(BF16)), g)
        part_ref[g] = gate_row(g, 0) * o_cmp + gate_row(g, 2) * o_win

    sub = lax.broadcasted_iota(jnp.int32, (8, tq), 0)
    n_slab = n_blk // 8

    def select_blocks(g, imp):
        score = jnp.where(valid, jnp.where(forced, FORCED_SCORE, imp), -1.0)
        slabs = [score[8 * v:8 * (v + 1), :] for v in range(n_slab)]
        ranks = [jnp.zeros((8, tq), F32) for _ in range(n_slab)]
        for i in range(n_blk):
            vi, ri = divmod(i, 8)
            si = jnp.broadcast_to(score[i:i + 1, :], (8, tq))
            for v in range(n_slab):
                if v > vi:
                    hit = jnp.where(si >= slabs[v], 1.0, 0.0)
                elif v < vi:
                    hit = jnp.where(si > slabs[v], 1.0, 0.0)
                else:
                    hit = jnp.where(sub > ri, jnp.where(si >= slabs[v], 1.0, 0.0), jnp.where(si > slabs[v], 1.0, 0.0))
                ranks[v] = ranks[v] + hit
        rank = jnp.concatenate(ranks, axis=0)
        sel_bias = jnp.where((rank < float(SEL_COUNT)) & valid, 0.0, NEG_INF)
        if n_blk < NSA_HEAD_DIM:
            sel_bias = jnp.concatenate([sel_bias, jnp.zeros((NSA_HEAD_DIM - n_blk, tq), F32)], axis=0)
        q_g = qs_l[g][g * NSA_HEAD_DIM:(g + 1) * NSA_HEAD_DIM, :]
        qa_ref[g] = jnp.concatenate([q_g, tile4(sel_bias).astype(BF16)], axis=0)

    o_cmp0, imp0 = cmp_branch(0)
    o_cmp1, imp1 = cmp_branch(1)
    select_blocks(0, imp0)
    window_branch(0, o_cmp0)
    select_blocks(1, imp1)
    window_branch(1, o_cmp1)

    kb_diag = s0 // KB

    def scores(kb):
        k0 = pl.multiple_of(kb * KB, KB)
        return [_dot(kvs_ref[pl.ds(k0, KB), KV_KS + g * LANES:KV_KS + (g + 1) * LANES], qa_ref[g])
                for g in groups]

    def accumulate(kb, s_l, first):
        k0 = pl.multiple_of(kb * KB, KB)
        for g in groups:
            va = kvs_ref[pl.ds(k0, KB), KV_VS + g * LANES:KV_VS + (g + 1) * LANES]
            if first:
                m_new = jnp.max(s_l[g], axis=0, keepdims=True)
                acc_ref[g] = _dot_tn(va, jnp.exp2(s_l[g] - m_new).astype(BF16))
            else:
                m_old = m_ref[g]
                m_new = jnp.maximum(m_old, jnp.max(s_l[g], axis=0, keepdims=True))
                acc_ref[g] = (jnp.exp2(m_old - m_new) * acc_ref[g]
                              + _dot_tn(va, jnp.exp2(s_l[g] - m_new).astype(BF16)))
            m_ref[g] = m_new

    causal_bias = tile4(jnp.where(kb_diag * KB + k_col <= t_row, 0.0, NEG_INF))
    s_diag = [s + causal_bias for s in scores(kb_diag)]
    s_next = scores(jnp.maximum(kb_diag - 1, 0))
    accumulate(kb_diag, s_diag, True)
    for g in groups:
        s_ref[g] = s_next[g]

    def sel_body(i, carry):
        kb = kb_diag - i
        s_cur = [s_ref[g] for g in groups]
        s_nxt = scores(jnp.maximum(kb - 1, 0))
        accumulate(kb, s_cur, False)
        for g in groups:
            s_ref[g] = s_nxt[g]
        return carry

    lax.fori_loop(1, kb_diag + 1, sel_body, 0)

    out_t = [part_ref[g] + gate_row(g, 1) * normalized(acc_ref[g], g) for g in groups]
    for r in range(rh):
        cs = slice(r * tq, (r + 1) * tq)
        slab_t = jnp.where(row_half == 0, out_t[0][:, cs], out_t[1][:, cs])
        o_ref[:, r * LANES:(r + 1) * LANES] = slab_t.T.astype(BF16)


def _nsa_call(q, kc, vc, kvs, small, ovt, batch, seq):
    assert seq % KB == 0 and seq >= WINDOW + TQ and seq // SEL_BLOCK <= NSA_HEAD_DIM
    nq = seq // TQ
    n_cmp = kc.shape[1]
    row = lambda b, i: (b * nq + i, 0)
    per_b3 = lambda b, i: (b, 0, 0)
    per_b2 = lambda b, i: (b, 0)
    c2 = lambda b, i: (0, 0)
    rows = NSA_GROUP_HEADS * TQ
    return pl.pallas_call(
        _nsa_kernel,
        grid=(batch, nq),
        in_specs=[pl.BlockSpec((TQ, NSA_WIDTH), row),
                  pl.BlockSpec((1, n_cmp, NSA_KV_WIDTH), per_b3),
                  pl.BlockSpec((1, n_cmp, NSA_KV_WIDTH), per_b3),
                  pl.BlockSpec((seq, KV_WIDTH), per_b2),
                  pl.BlockSpec((TQ, LANES), row),
                  pl.BlockSpec(ovt.shape, c2)],
        out_specs=pl.BlockSpec((TQ, NSA_WIDTH), row),
        out_shape=jax.ShapeDtypeStruct((batch * seq, NSA_WIDTH), BF16),
        scratch_shapes=[pltpu.VMEM((NSA_KV_GROUPS, LANES, rows), BF16),
                        pltpu.VMEM((NSA_KV_GROUPS, 1, rows), F32),
                        pltpu.VMEM((NSA_KV_GROUPS, LANES, rows), F32),
                        pltpu.VMEM((NSA_KV_GROUPS, KB, rows), F32),
                        pltpu.VMEM((NSA_KV_GROUPS, LANES, rows), F32)],
        compiler_params=pltpu.CompilerParams(dimension_semantics=("parallel", "arbitrary"),
                                             vmem_limit_bytes=VMEM_LIMIT),
        name="nsa_attention",
    )(q, kc, vc, kvs, small, ovt)


def _out_kernel(x_ref, oa_ref, ob_ref, mg_ref, wa_ref, wb_ref, wo_ref, g2_ref, wgu_ref, wd_ref, gf_ref, out_ref):
    tm = x_ref.shape[0]
    halves = [slice(i * 128, (i + 1) * 128) for i in range(tm // 128)]
    st = [{} for _ in halves]

    def merge(i, rows):
        a = _dot(oa_ref[rows, :], wa_ref[...])
        b = _dot(ob_ref[rows, :], wb_ref[...])
        st[i]["merged"] = (_sigmoid(mg_ref[rows, 0:D_MODEL].astype(F32)) * a
                           + _sigmoid(mg_ref[rows, D_MODEL:2 * D_MODEL].astype(F32)) * b).astype(BF16)

    def out_proj(i, rows):
        st[i]["x1"] = x_ref[rows, :] + _dot(st[i]["merged"], wo_ref[...])
        st[i]["h2"] = _rms(st[i]["x1"], g2_ref[...]).astype(BF16)

    def ffn_up(i, rows):
        gate = _dot(st[i]["h2"], wgu_ref[:, 0:FFN_HIDDEN])
        up = _dot(st[i]["h2"], wgu_ref[:, FFN_HIDDEN:2 * FFN_HIDDEN])
        st[i]["act"] = (_silu(gate) * up).astype(BF16)

    def ffn_down(i, rows):
        y = st[i]["x1"] + _dot(st[i]["act"], wd_ref[...])
        out_ref[rows, :] = _rms(y, gf_ref[...])

    for stage in (merge, out_proj, ffn_up, ffn_down):
        for i, rows in enumerate(halves):
            stage(i, rows)


def _out_call(x2d, o_a, o_b, mg, wa, wb, wo, g2, wgu, wd, gf, tm=512):
    t = x2d.shape[0]
    row = lambda i: (i, 0)
    const = lambda i: (0, 0)

    def wspec(w):
        return pl.BlockSpec(w.shape, const, pipeline_mode=pl.Buffered(1))

    return pl.pallas_call(
        _out_kernel,
        grid=(t // tm,),
        in_specs=[pl.BlockSpec((tm, D_MODEL), row),
                  pl.BlockSpec((tm, GDN_WIDTH), row),
                  pl.BlockSpec((tm, NSA_WIDTH), row),
                  pl.BlockSpec((tm, 2 * D_MODEL), row),
                  wspec(wa), wspec(wb), wspec(wo),
                  pl.BlockSpec((1, D_MODEL), const),
                  wspec(wgu), wspec(wd),
                  pl.BlockSpec((1, D_MODEL), const)],
        out_specs=pl.BlockSpec((tm, D_MODEL), row),
        out_shape=jax.ShapeDtypeStruct((t, D_MODEL), F32),
        compiler_params=pltpu.CompilerParams(dimension_semantics=("parallel",),
                                             vmem_limit_bytes=VMEM_LIMIT),
        name="merge_ffn",
    )(x2d, o_a, o_b, mg, wa, wb, wo, g2, wgu, wd, gf)


def _q_head_perm():
    idx = []
    for r in range(NSA_GROUP_HEADS):
        for g in range(NSA_KV_GROUPS):
            base = (g * NSA_GROUP_HEADS + r) * NSA_HEAD_DIM
            idx.extend(range(base, base + NSA_HEAD_DIM))
    return np.asarray(idx, dtype=np.int32)


def _reorder_w_in(w_in):
    sizes = (3 * GDN_WIDTH, GDN_WIDTH, GDN_HEADS, GDN_HEADS, NSA_WIDTH) + (NSA_KV_WIDTH,) * 6 + (
        3 * NSA_HEADS, D_MODEL, D_MODEL)
    offs = np.concatenate([[0], np.cumsum(sizes)])
    seg = lambda i: w_in[:, offs[i]:offs[i + 1]]
    qkv, z, beta, alpha, q, kc, vc, ks, vs, kw, vw, gate, ma, mb = [seg(i) for i in range(14)]
    pad = jnp.zeros((D_MODEL, LANES - 2 * GDN_HEADS - 3 * NSA_HEADS), w_in.dtype)
    q = q[:, _q_head_perm()]
    zero_half = jnp.zeros((D_MODEL, NSA_HEAD_DIM), w_in.dtype)
    ks_split = [ks[:, :NSA_HEAD_DIM], zero_half, ks[:, NSA_HEAD_DIM:], zero_half]
    return jnp.concatenate([qkv, z, q, kc, vc] + ks_split + [vs, kw, vw, ma, mb, beta, alpha, gate, pad],
                           axis=1).astype(BF16)


def _rope_tables(seq):
    half = ROPE_DIM // 2
    inv_freq = ROPE_THETA ** (-jnp.arange(half, dtype=F32) / half)
    ang = jnp.arange(seq, dtype=F32)[:, None] * inv_freq
    cos, sin = jnp.cos(ang), jnp.sin(ang)
    ones = jnp.ones((seq, NSA_HEAD_DIM - ROPE_DIM), F32)
    zeros = jnp.zeros((seq, NSA_HEAD_DIM - ROPE_DIM), F32)
    z8 = jnp.zeros((seq, half), F32)
    c = jnp.concatenate([cos, cos, ones], axis=1)
    s1 = jnp.concatenate([z8, sin, zeros], axis=1)
    s2 = jnp.concatenate([-sin, z8, zeros], axis=1)
    tile2 = lambda a: jnp.concatenate([a, a], axis=1)
    return tile2(c), tile2(s1), tile2(s2)


def _cmp_weights(pos, w1, w2):
    g = NSA_KV_GROUPS
    seg = CMP_BLOCK // 2
    eye = jnp.eye(g, dtype=w1.dtype)
    pos_h = pos.reshape(2, seg, 1, NSA_HEAD_DIM)
    pos_flat = jnp.broadcast_to(pos_h, (2, seg, g, NSA_HEAD_DIM)).reshape(2, seg * g * NSA_HEAD_DIM)
    w1_h = w1.reshape(2, seg, NSA_HEAD_DIM, CMP_HIDDEN)
    w1_bd = jnp.einsum('alds,gk->algdks', w1_h, eye).reshape(2, seg * g * NSA_HEAD_DIM, g * CMP_HIDDEN)
    w2_bd = jnp.einsum('hd,gk->ghkd', w2, eye).reshape(g * CMP_HIDDEN, g * NSA_HEAD_DIM)
    return pos_flat.astype(F32), w1_bd.astype(BF16), w2_bd.astype(BF16)


def _selection_constants(seq):
    n_cmp_pad = seq // CMP_STRIDE
    n_blk = seq // SEL_BLOCK
    cmp_start = np.arange(n_cmp_pad) * CMP_STRIDE
    cmp_end = cmp_start + CMP_BLOCK - 1
    sel_start = np.arange(n_blk) * SEL_BLOCK
    ovt = ((cmp_start[None, :] <= sel_start[:, None] + SEL_BLOCK - 1) & (cmp_end[None, :] >= sel_start[:, None]))
    ovt[:, n_cmp_pad - 1] = False
    return jnp.asarray(ovt, BF16)


def _lane_vector(values, offset):
    v = jnp.zeros((1, LANES), F32)
    return v.at[0, offset:offset + values.shape[0]].set(values.astype(F32))


def _hybrid_block(x, mix_norm_gain, w_in, gdn_conv_w, gdn_a_log, gdn_dt_bias, gdn_out_norm_gain,
                  cmp_pos_k, cmp_w1_k, cmp_w2_k, cmp_pos_v, cmp_w1_v, cmp_w2_v,
                  w_branch_gdn, w_branch_nsa, w_out, ffn_norm_gain, w_gate_up, w_down, final_norm_gain):
    batch, seq, _ = x.shape
    x2d = x.reshape(batch * seq, D_MODEL)
    rc, rs1, rs2 = _rope_tables(seq)
    qkv, z, q, kc, vc, kvs, mg, small = _inproj_call(
        x2d, mix_norm_gain.reshape(1, D_MODEL), _reorder_w_in(w_in), rc, rs1, rs2,
        gdn_conv_w.astype(F32), seq)

    o_a = _gdn_call(qkv, z, small,
                    _lane_vector(gdn_a_log, SMALL_ALPHA), _lane_vector(gdn_dt_bias, SMALL_ALPHA),
                    gdn_out_norm_gain.reshape(1, GDN_HEAD_DIM).astype(F32), batch, seq)

    seg_width = (CMP_BLOCK // 2) * NSA_KV_WIDTH
    kc3 = kc.reshape(batch, seq // CMP_STRIDE, seg_width)
    vc3 = vc.reshape(batch, seq // CMP_STRIDE, seg_width)
    kcc, vcc = _cmp_call(kc3, vc3, *_cmp_weights(cmp_pos_k, cmp_w1_k, cmp_w2_k),
                         *_cmp_weights(cmp_pos_v, cmp_w1_v, cmp_w2_v))
    o_b = _nsa_call(q, kcc, vcc, kvs, small, _selection_constants(seq), batch, seq)

    out = _out_call(x2d, o_a, o_b, mg,
                    w_branch_gdn.astype(BF16), w_branch_nsa[_q_head_perm(), :].astype(BF16), w_out.astype(BF16),
                    ffn_norm_gain.reshape(1, D_MODEL), w_gate_up.astype(BF16), w_down.astype(BF16),
                    final_norm_gain.reshape(1, D_MODEL))
    return out.reshape(batch, seq, D_MODEL)


def kernel(x, mix_norm_gain, w_in, gdn_conv_w, gdn_a_log, gdn_dt_bias, gdn_out_norm_gain, cmp_pos_k, cmp_w1_k,
           cmp_w2_k, cmp_pos_v, cmp_w1_v, cmp_w2_v, w_branch_gdn, w_branch_nsa, w_out, ffn_norm_gain, w_gate_up,
           w_down, final_norm_gain):
    assert mix_norm_gain.shape[0] == 1, "single-layer block"
    return _hybrid_block(x, mix_norm_gain[0], w_in[0], gdn_conv_w[0], gdn_a_log[0], gdn_dt_bias[0],
                         gdn_out_norm_gain[0], cmp_pos_k[0], cmp_w1_k[0], cmp_w2_k[0], cmp_pos_v[0], cmp_w1_v[0],
                         cmp_w2_v[0], w_branch_gdn[0], w_branch_nsa[0], w_out[0], ffn_norm_gain[0], w_gate_up[0],
                         w_down[0], final_norm_gain)
```

```python
import functools

import numpy as np
import jax
import jax.numpy as jnp
from jax import lax
from jax.experimental import pallas as pl
from jax.experimental.pallas import tpu as pltpu

F32 = jnp.float32
BF16 = jnp.bfloat16

D_MODEL = 1024
NORM_EPS = 1e-6
GDN_HEADS = 4
GDN_HEAD_DIM = 128
GDN_WIDTH = GDN_HEADS * GDN_HEAD_DIM
GDN_CONV = 4
GDN_CHUNK = 64
INV_BASE = 8
NSA_HEADS = 8
NSA_KV_GROUPS = 2
NSA_GROUP_HEADS = NSA_HEADS // NSA_KV_GROUPS
NSA_HEAD_DIM = 64
NSA_WIDTH = NSA_HEADS * NSA_HEAD_DIM
NSA_KV_WIDTH = NSA_KV_GROUPS * NSA_HEAD_DIM
CMP_BLOCK = 32
CMP_STRIDE = 16
CMP_HIDDEN = 128
SEL_BLOCK = 64
SEL_COUNT = 16
WINDOW = 512
ROPE_THETA = 500000.0
ROPE_DIM = NSA_HEAD_DIM // 4
FORCED_SCORE = 1000.0
NEG_INF = -1e30
FFN_HIDDEN = 2816

LANES = 128
VMEM_LIMIT = 56 * 1024 * 1024

C_QKV = (0, 1536)
C_Z = (1536, 2048)
C_Q = (2048, 2560)
C_KC = (2560, 2688)
C_VC = (2688, 2816)
C_KVS = (2816, 3456)
C_MG = (3456, 5504)
C_SMALL = (5504, 5632)
IN_WIDTH_PADDED = 5632
SMALL_BETA = 0
SMALL_ALPHA = 4
SMALL_GATE = 8

KV_KS = 0
KV_VS = 256
KV_KW = 512
KV_VW = 640
KV_WIDTH = 896
LOG2E = 1.4426950408889634
Q_SCALE = NSA_HEAD_DIM ** -0.5 * LOG2E

TQ = 128
KB = 512


def _dot(a, b):
    return jnp.dot(a, b, preferred_element_type=F32)


def _dot_nt(a, b):
    return lax.dot_general(a, b, (((1,), (1,)), ((), ())), preferred_element_type=F32)


def _dot_tn(a, b):
    return lax.dot_general(a, b, (((0,), (0,)), ((), ())), preferred_element_type=F32)


def _sigmoid(x):
    return 1.0 / (1.0 + jnp.exp(-x))


def _silu(x):
    return x * _sigmoid(x)


def _rms(x, gain):
    return x * lax.rsqrt(jnp.mean(x * x, axis=-1, keepdims=True) + NORM_EPS) * gain


def _rope_slab(y, c, s1, s2):
    return y * c + pltpu.roll(y, 8, 1) * s1 + pltpu.roll(y, LANES - 8, 1) * s2


def _inproj_kernel(x_ref, gain_ref, w_ref, rc_ref, rs1_ref, rs2_ref, convw_ref,
                   qkv_ref, z_ref, kc_ref, vc_ref, kvs_ref, mg_ref, small_ref, q_ref, gates_ref,
                   xbuf, segbuf, *, seq):
    tm = x_ref.shape[0]
    hb = _rms(x_ref[...], gain_ref[...]).astype(BF16)

    def proj(cols):
        return _dot(hb, w_ref[:, cols[0]:cols[1]])

    first_tile = pl.program_id(0) % (seq // tm) == 0

    @pl.when(first_tile)
    def _():
        xbuf[0:8, :] = jnp.zeros((8, 3 * GDN_WIDTH), F32)

    @pl.when(jnp.logical_not(first_tile))
    def _():
        xbuf[0:8, :] = xbuf[tm:tm + 8, :]

    cw = convw_ref[...]
    slab_w = 2 * GDN_HEAD_DIM

    def gdn_slab(k):
        lo = k * slab_w
        cols = slice(lo, lo + slab_w)
        xbuf[8:8 + tm, cols] = proj((C_QKV[0] + lo, C_QKV[0] + lo + slab_w))
        conv = xbuf[8:8 + tm, cols] * cw[GDN_CONV - 1:GDN_CONV, cols]
        for jj in range(GDN_CONV - 1):
            off = 8 - (GDN_CONV - 1) + jj
            conv = conv + xbuf[off:off + tm, cols] * cw[jj:jj + 1, cols]
        act = _silu(conv)
        for h in range(lo // GDN_HEAD_DIM, (lo + slab_w) // GDN_HEAD_DIM):
            a = act[:, h * GDN_HEAD_DIM - lo:(h + 1) * GDN_HEAD_DIM - lo]
            if h < 2 * GDN_HEADS:
                a = a * lax.rsqrt(jnp.sum(a * a, axis=-1, keepdims=True) + NORM_EPS)
            if h < GDN_HEADS:
                a = a * (GDN_HEAD_DIM ** -0.5)
            qkv_ref[:, h * GDN_HEAD_DIM:(h + 1) * GDN_HEAD_DIM] = a.astype(BF16)

    c, s1, s2 = rc_ref[...], rs1_ref[...], rs2_ref[...]

    def task_z():
        z_ref[...] = proj(C_Z).astype(BF16)

    def task_merge(half):
        lo = C_MG[0] + half * D_MODEL
        mg_ref[:, half * D_MODEL:(half + 1) * D_MODEL] = proj((lo, lo + D_MODEL)).astype(BF16)

    def task_small():
        small = proj(C_SMALL)
        small_ref[...] = small
        gates = _sigmoid(small)
        for ch in range(tm // LANES):
            tok = slice(ch * LANES, (ch + 1) * LANES)
            gates_ref[:, tok] = gates[tok, :].T
        segbuf[0] = proj(C_VC)
        segbuf[1] = _rope_slab(proj(C_KC), c, s1, s2)
        n_seg = tm // CMP_STRIDE
        for which, out_ref in ((0, vc_ref), (1, kc_ref)):
            for l in range(CMP_STRIDE):
                rows = segbuf[which, pl.ds(l, n_seg, stride=CMP_STRIDE), :]
                out_ref[:, l * NSA_KV_WIDTH:(l + 1) * NSA_KV_WIDTH] = rows.astype(BF16)

    def task_q():
        q = proj(C_Q)
        for r in range(NSA_GROUP_HEADS):
            sl = slice(r * LANES, (r + 1) * LANES)
            roped = _rope_slab(q[:, sl], c, s1, s2) * Q_SCALE
            for ch in range(tm // LANES):
                tok = slice(ch * LANES, (ch + 1) * LANES)
                q_ref[sl, tok] = roped[tok, :].T.astype(BF16)

    def task_kv():
        tok = (pl.program_id(0) % (seq // tm)) * tm + lax.broadcasted_iota(jnp.int32, (tm, 1), 0)
        lane = lax.broadcasted_iota(jnp.int32, (tm, LANES), 1)
        group0 = lane < NSA_HEAD_DIM
        block_onehot = jnp.where(lane == NSA_HEAD_DIM + tok // SEL_BLOCK, 1.0, 0.0)
        kvs = proj(C_KVS)
        for g in range(NSA_KV_GROUPS):
            ks = _rope_slab(kvs[:, g * LANES:(g + 1) * LANES], c, s1, s2)
            kvs_ref[:, KV_KS + g * LANES:KV_KS + (g + 1) * LANES] = jnp.where(group0, ks, block_onehot).astype(BF16)
        vs, kw, vw = kvs[:, 256:384], kvs[:, 384:512], kvs[:, 512:640]
        kvs_ref[:, KV_VS:KV_VS + LANES] = jnp.where(group0, vs, 1.0).astype(BF16)
        kvs_ref[:, KV_VS + LANES:KV_VS + 2 * LANES] = jnp.where(group0, 1.0, vs).astype(BF16)
        kvs_ref[:, KV_KW:KV_KW + LANES] = _rope_slab(kw, c, s1, s2).astype(BF16)
        kvs_ref[:, KV_VW:KV_VW + LANES] = jnp.where(group0, vw, 1.0).astype(BF16)
        kvs_ref[:, KV_VW + LANES:KV_VW + 2 * LANES] = jnp.where(group0, 1.0, vw).astype(BF16)

    others = [task_z, functools.partial(task_merge, 0), task_q, functools.partial(task_merge, 1), task_kv,
              task_small]
    n_slabs = 3 * GDN_WIDTH // slab_w
    assert len(others) == n_slabs
    for k in range(n_slabs):
        gdn_slab(k)
        others[k]()


def _inproj_call(x2d, gain, w_r, rc, rs1, rs2, conv_w, seq, tm=512):
    assert seq // SEL_BLOCK <= LANES, "selection-block one-hot must fit one lane tile"
    t = x2d.shape[0]
    n_seq_tiles = seq // tm
    row = lambda i: (i, 0)
    const = lambda i: (0, 0)
    tab = lambda i: (i % n_seq_tiles, 0)
    seg = CMP_STRIDE * NSA_KV_WIDTH
    outs = [(tm, t, 1536, BF16), (tm, t, 512, BF16),
            (tm // CMP_STRIDE, t // CMP_STRIDE, seg, BF16), (tm // CMP_STRIDE, t // CMP_STRIDE, seg, BF16),
            (tm, t, KV_WIDTH, BF16), (tm, t, 2048, BF16), (tm, t, 128, F32)]
    outs_t = [(NSA_WIDTH, BF16), (LANES, F32)]
    col = lambda i: (0, i)
    return pl.pallas_call(
        functools.partial(_inproj_kernel, seq=seq),
        grid=(t // tm,),
        in_specs=[pl.BlockSpec((tm, D_MODEL), row),
                  pl.BlockSpec((1, D_MODEL), const),
                  pl.BlockSpec((D_MODEL, IN_WIDTH_PADDED), const, pipeline_mode=pl.Buffered(1)),
                  pl.BlockSpec((tm, LANES), tab),
                  pl.BlockSpec((tm, LANES), tab),
                  pl.BlockSpec((tm, LANES), tab),
                  pl.BlockSpec((GDN_CONV, 3 * GDN_WIDTH), const)],
        out_specs=([pl.BlockSpec((r, w), row) for r, _, w, _ in outs]
                   + [pl.BlockSpec((h, tm), col) for h, _ in outs_t]),
        out_shape=([jax.ShapeDtypeStruct((n, w), d) for _, n, w, d in outs]
                   + [jax.ShapeDtypeStruct((h, t), d) for h, d in outs_t]),
        scratch_shapes=[pltpu.VMEM((tm + 8, 3 * GDN_WIDTH), F32),
                        pltpu.VMEM((2, tm, NSA_KV_WIDTH), F32)],
        compiler_params=pltpu.CompilerParams(dimension_semantics=("arbitrary",),
                                             vmem_limit_bytes=VMEM_LIMIT),
        name="inproj",
    )(x2d, gain, w_r, rc, rs1, rs2, conv_w)


def _gdn_kernel(qkv_ref, z_ref, small_ref, alog_ref, dtb_ref, ogain_ref, o_ref,
                state, u_ref, w_ref, qk_ref, qg_ref, kd_ref, gl_ref, *, blocks_per_seq):
    t = pl.program_id(0)
    bufs = (u_ref, w_ref, qk_ref, qg_ref, kd_ref, gl_ref)

    @pl.when(t == 0)
    def _():
        state[...] = jnp.zeros_like(state)
        for ref in bufs:
            ref[...] = jnp.zeros_like(ref)

    fresh = (t + blocks_per_seq - 1) % blocks_per_seq == 0
    for slot in range(2):
        @pl.when(t % 2 == slot)
        def _(slot=slot):
            _gdn_step(qkv_ref, z_ref, small_ref, alog_ref, dtb_ref, ogain_ref, o_ref, state, bufs, slot, fresh)


def _gdn_step(qkv_ref, z_ref, small_ref, alog_ref, dtb_ref, ogain_ref, o_ref, state, bufs, cur, fresh):
    u_ref, w_ref, qk_ref, qg_ref, kd_ref, gl_ref = bufs
    prev = 1 - cur
    cb = qkv_ref.shape[0]
    c = GDN_CHUNK
    n_chunks = cb // c
    heads = range(GDN_HEADS)

    row = lax.broadcasted_iota(jnp.int32, (c, c), 0)
    col = lax.broadcasted_iota(jnp.int32, (c, c), 1)
    causal = row >= col
    strict = row > col
    tril = causal.astype(F32)
    eye = (row == col).astype(F32)
    neg_decay_rate = -jnp.exp(alog_ref[...])
    dtb = dtb_ref[...]
    ogain = ogain_ref[...]

    pairs = [(ci, h) for ci in range(n_chunks) for h in heads]
    n_pairs = len(pairs)
    A = {}

    def a_prepare():
        beta_c, gc_c, gct_c = [], [], []
        for ci in range(n_chunks):
            sm = small_ref[ci * c:(ci + 1) * c, :]
            beta_c.append(_sigmoid(sm))
            xg = sm + dtb
            softplus = jnp.maximum(xg, 0.0) + jnp.log(1.0 + jnp.exp(-jnp.abs(xg)))
            gc_all = jnp.dot(tril, neg_decay_rate * softplus, precision=lax.Precision.HIGHEST,
                             preferred_element_type=F32)
            gc_c.append(gc_all)
            gct_c.append(gc_all.T)

        def head_cols(base, ci, h):
            lo = base + h * GDN_HEAD_DIM
            return qkv_ref[ci * c:(ci + 1) * c, lo:lo + GDN_HEAD_DIM].astype(F32)

        A["qb"], A["kbf"], A["vb"], A["kbeg"], A["decay"], a_l = [], [], [], [], [], []
        for i, (ci, h) in enumerate(pairs):
            q = head_cols(0, ci, h)
            k = head_cols(GDN_WIDTH, ci, h)
            v = head_cols(2 * GDN_WIDTH, ci, h)
            beta = beta_c[ci][:, SMALL_BETA + h:SMALL_BETA + h + 1]
            gc = gc_c[ci][:, SMALL_ALPHA + h:SMALL_ALPHA + h + 1]
            gr = gct_c[ci][SMALL_ALPHA + h:SMALL_ALPHA + h + 1, :]
            decay = jnp.where(causal, jnp.exp(jnp.where(causal, gc - gr, 0.0)), 0.0)
            eg = jnp.exp(gc)
            g_last = gc[c - 1:c, :]
            kb = k * beta
            kbf = k.astype(BF16)
            qg_ref[cur, i] = (q * eg).astype(BF16)
            kd_ref[cur, i] = (k * jnp.exp(g_last - gc)).astype(BF16)
            gl_ref[cur, i] = jnp.broadcast_to(jnp.exp(g_last), gl_ref.shape[2:])
            A["qb"].append(q.astype(BF16))
            A["kbf"].append(kbf)
            A["vb"].append((v * beta).astype(BF16))
            A["kbeg"].append((kb * eg).astype(BF16))
            A["decay"].append(decay)
            a_l.append(jnp.where(strict, _dot_nt(kb.astype(BF16), kbf) * decay, 0.0))
        A["a"] = a_l
        diag_blocks = [jnp.where(row // INV_BASE == col // INV_BASE, a, 0.0) for a in a_l]
        A["t"] = [eye - d for d in diag_blocks]
        A["p"] = diag_blocks

    def a_qk():
        for i in range(n_pairs):
            qk_ref[cur, i] = (_dot_nt(A["qb"][i], A["kbf"][i]) * A["decay"][i]).astype(BF16)

    def a_square():
        A["p"] = [_dot(p.astype(BF16), p.astype(BF16)) for p in A["p"]]

    def a_extend():
        A["t"] = [t_ + _dot(t_.astype(BF16), p.astype(BF16)) for t_, p in zip(A["t"], A["p"])]

    def a_u():
        A["tb"] = [t_.astype(BF16) for t_ in A["t"]]
        for i in range(n_pairs):
            u_ref[cur, i] = _dot(A["tb"][i], A["vb"][i])

    def a_w():
        for i in range(n_pairs):
            w_ref[cur, i] = _dot(A["tb"][i], A["kbeg"][i]).astype(BF16)

    def a_link(size):
        link = (row // (2 * size) == col // (2 * size)) & ((row // size) % 2 == 1) & ((col // size) % 2 == 0)
        A["dn"] = [_dot(t_.astype(BF16), jnp.where(link, a, 0.0).astype(BF16)) for t_, a in zip(A["t"], A["a"])]

    def a_merge():
        A["t"] = [t_ - _dot(dn.astype(BF16), t_.astype(BF16)) for t_, dn in zip(A["t"], A["dn"])]

    a_stages = [a_prepare, a_qk] + [a_square, a_extend] * (INV_BASE.bit_length() - 2)
    size = INV_BASE
    while size < c:
        a_stages += [functools.partial(a_link, size), a_merge]
        size *= 2
    a_stages += [a_u, a_w]

    B = {"s": [jnp.where(fresh, 0.0, state[h]) for h in heads]}

    def b_new_values(ci):
        idx = [ci * GDN_HEADS + h for h in heads]
        B["sb"] = [s.astype(BF16) for s in B["s"]]
        B["vn"] = [(u_ref[prev, i] - _dot(w_ref[prev, i], sb)).astype(BF16) for i, sb in zip(idx, B["sb"])]

    def b_output(ci):
        idx = [ci * GDN_HEADS + h for h in heads]
        o_l = [_dot(qg_ref[prev, i], sb) + _dot(qk_ref[prev, i], vn) for i, sb, vn in zip(idx, B["sb"], B["vn"])]
        B["s"] = [s * gl_ref[prev, i][0:1, :] + _dot_tn(kd_ref[prev, i], vn)
                  for i, s, vn in zip(idx, B["s"], B["vn"])]
        for h, o in zip(heads, o_l):
            hs = slice(h * GDN_HEAD_DIM, (h + 1) * GDN_HEAD_DIM)
            zh = z_ref[ci * c:(ci + 1) * c, hs].astype(F32)
            o = o * lax.rsqrt(jnp.mean(o * o, axis=-1, keepdims=True) + NORM_EPS) * ogain * _silu(zh)
            o_ref[ci * c:(ci + 1) * c, hs] = o.astype(BF16)

    b_stages = []
    for ci in range(n_chunks):
        b_stages += [functools.partial(b_new_values, ci), functools.partial(b_output, ci)]

    emitted_b = 0
    for k, stage in enumerate(a_stages):
        while emitted_b < len(b_stages) and emitted_b * len(a_stages) <= k * len(b_stages):
            b_stages[emitted_b]()
            emitted_b += 1
        stage()
    for stage in b_stages[emitted_b:]:
        stage()
    for h in heads:
        state[h] = B["s"][h]


def _gdn_call(qkv, z, small, alog_v, dtb_v, ogain, batch, seq, cb=256):
    blocks_per_seq = seq // cb
    n_blocks = batch * blocks_per_seq
    n_pairs = (cb // GDN_CHUNK) * GDN_HEADS
    c, d = GDN_CHUNK, GDN_HEAD_DIM
    phase_a_block = lambda t: (jnp.minimum(t, n_blocks - 1), 0)
    phase_b_block = lambda t: (jnp.maximum(t - 1, 0), 0)
    const = lambda t: (0, 0)
    return pl.pallas_call(
        functools.partial(_gdn_kernel, blocks_per_seq=blocks_per_seq),
        grid=(n_blocks + 1,),
        in_specs=[pl.BlockSpec((cb, 3 * GDN_WIDTH), phase_a_block),
                  pl.BlockSpec((cb, GDN_WIDTH), phase_b_block),
                  pl.BlockSpec((cb, LANES), phase_a_block),
                  pl.BlockSpec((1, LANES), const),
                  pl.BlockSpec((1, LANES), const),
                  pl.BlockSpec((1, GDN_HEAD_DIM), const)],
        out_specs=pl.BlockSpec((cb, GDN_WIDTH), phase_b_block),
        out_shape=jax.ShapeDtypeStruct((batch * seq, GDN_WIDTH), BF16),
        scratch_shapes=[pltpu.VMEM((GDN_HEADS, d, d), F32),
                        pltpu.VMEM((2, n_pairs, c, d), F32),
                        pltpu.VMEM((2, n_pairs, c, d), BF16),
                        pltpu.VMEM((2, n_pairs, c, c), BF16),
                        pltpu.VMEM((2, n_pairs, c, d), BF16),
                        pltpu.VMEM((2, n_pairs, c, d), BF16),
                        pltpu.VMEM((2, n_pairs, 8, d), F32)],
        compiler_params=pltpu.CompilerParams(dimension_semantics=("arbitrary",),
                                             vmem_limit_bytes=VMEM_LIMIT),
        name="gdn",
    )(qkv, z, small, alog_v, dtb_v, ogain)


def _cmp_kernel(kc_ref, vc_ref, posk_ref, w1k_ref, w2k_ref, posv_ref, w1v_ref, w2v_ref, kco_ref, vco_ref):
    for x_ref, pos_ref, w1_ref, w2_ref, out_ref in ((kc_ref, posk_ref, w1k_ref, w2k_ref, kco_ref),
                                                    (vc_ref, posv_ref, w1v_ref, w2v_ref, vco_ref)):
        a = x_ref[0].astype(F32)
        n = a.shape[0]
        p0 = _dot((a + pos_ref[0:1, :]).astype(BF16), w1_ref[0])
        p1 = _dot((a + pos_ref[1:2, :]).astype(BF16), w1_ref[1])
        hid = _silu(p0 + pltpu.roll(p1, n - 1, 0))
        out_ref[0] = _dot(hid.astype(BF16), w2_ref[...]).astype(BF16)


def _cmp_call(kc3, vc3, posk, w1k, w2k, posv, w1v, w2v):
    b, n, width = kc3.shape
    hid2 = NSA_KV_GROUPS * CMP_HIDDEN
    per_b = lambda i: (i, 0, 0)
    c2 = lambda i: (0, 0)
    c3 = lambda i: (0, 0, 0)
    wspecs = [pl.BlockSpec((2, width), c2), pl.BlockSpec((2, width, hid2), c3), pl.BlockSpec((hid2, NSA_KV_WIDTH), c2)]
    return pl.pallas_call(
        _cmp_kernel,
        grid=(b,),
        in_specs=[pl.BlockSpec((1, n, width), per_b), pl.BlockSpec((1, n, width), per_b)] + wspecs + wspecs,
        out_specs=[pl.BlockSpec((1, n, NSA_KV_WIDTH), per_b)] * 2,
        out_shape=[jax.ShapeDtypeStruct((b, n, NSA_KV_WIDTH), BF16)] * 2,
        compiler_params=pltpu.CompilerParams(dimension_semantics=("parallel",),
                                             vmem_limit_bytes=VMEM_LIMIT),
        name="nsa_compress",
    )(kc3, vc3, posk, w1k, w2k, posv, w1v, w2v)


def _nsa_kernel(q_ref, kc_ref, vc_ref, kvs_ref, gates_ref, ovt_ref, o_ref,
                qa_ref, m_ref, acc_ref, s_ref, part_ref):
    tq = q_ref.shape[1]
    rh = NSA_GROUP_HEADS
    n_cmp = kc_ref.shape[1]
    n_blk = ovt_ref.shape[0]
    qi = pl.program_id(1)
    s0 = qi * tq

    row_half = lax.broadcasted_iota(jnp.int32, (LANES, tq), 0) // NSA_HEAD_DIM
    t_row = s0 + lax.broadcasted_iota(jnp.int32, (1, tq), 1)
    k_col = lax.broadcasted_iota(jnp.int32, (KB, 1), 0)
    q_t = [q_ref[r * LANES:(r + 1) * LANES, :].astype(F32) for r in range(rh)]
    gates_t = gates_ref[...]
    kc = kc_ref[0]
    vc = vc_ref[0]
    cmp_end = lax.broadcasted_iota(jnp.int32, (n_cmp, 1), 0) * CMP_STRIDE + (CMP_BLOCK - 1)
    vis1 = cmp_end <= t_row
    vis = jnp.concatenate([vis1] * rh, axis=1)

    blk = lax.broadcasted_iota(jnp.int32, (n_blk, tq), 0)
    cur = (s0 + lax.broadcasted_iota(jnp.int32, (n_blk, tq), 1)) // SEL_BLOCK
    valid = blk <= cur
    forced = (blk == 0) | (blk == cur) | (blk == cur - 1)

    def tile4(a):
        return jnp.concatenate([a] * rh, axis=1)

    def normalized(acc, g):
        l_row = acc[(1 - g) * NSA_HEAD_DIM:(1 - g) * NSA_HEAD_DIM + 1, :]
        return acc * (1.0 / l_row)

    groups = range(NSA_KV_GROUPS)
    def gate_row(g, branch):
        cols = [SMALL_GATE + (g * rh + r) * 3 + branch for r in range(rh)]
        return jnp.concatenate([gates_t[c:c + 1, :] for c in cols], axis=1)

    qs_l = [jnp.concatenate([jnp.where(row_half == g, q_t[r], 0.0) for r in range(rh)], axis=1).astype(BF16)
            for g in groups]
    span = WINDOW + TQ
    w0 = pl.multiple_of(jnp.maximum(s0 - WINDOW, 0), TQ)
    kw = kvs_ref[pl.ds(w0, span), KV_KW:KV_KW + LANES]
    s_cmp = [_dot(kc, qs_l[g]) for g in groups]
    s_win = [_dot(kw, qs_l[g]) for g in groups]

    def cmp_branch(g):
        s = jnp.where(vis, s_cmp[g], NEG_INF)
        m = jnp.max(s, axis=0, keepdims=True)
        e = jnp.where(vis, jnp.exp2(s - m), 0.0)
        den = jnp.sum(e, axis=0, keepdims=True)
        p = e * (1.0 / jnp.where(den > 0.0, den, 1.0))
        o_cmp = _dot_tn(vc, p.astype(BF16))
        p_sum = p[:, 0:tq]
        for r in range(1, rh):
            p_sum = p_sum + p[:, r * tq:(r + 1) * tq]
        p_hi = p_sum.astype(BF16)
        p_lo = (p_sum - p_hi.astype(F32)).astype(BF16)
        ovt = ovt_ref[...]
        return o_cmp, _dot(ovt, p_hi) + _dot(ovt, p_lo)

    rel = t_row - (w0 + lax.broadcasted_iota(jnp.int32, (span, 1), 0))
    win_bias = tile4(jnp.where((rel >= 0) & (rel < WINDOW), 0.0, NEG_INF))

    def window_branch(g, o_cmp):
        vw = kvs_ref[pl.ds(w0, span), KV_VW + g * LANES:KV_VW + (g + 1) * LANES]
        sw = s_win[g] + win_bias
        pw = jnp.exp2(sw - jnp.max(sw, axis=0, keepdims=True))
        o_win = normalized(_dot_tn(vw, pw.astype(BF16)), g)
        part_ref[g] = gate_row(g, 0) * o_cmp + gate_row(g, 2) * o_win

    sub = lax.broadcasted_iota(jnp.int32, (8, tq), 0)
    n_slab = n_blk // 8

    def select_blocks(g, imp):
        score = jnp.where(valid, jnp.where(forced, FORCED_SCORE, imp), -1.0)
        slabs = [score[8 * v:8 * (v + 1), :] for v in range(n_slab)]
        ranks = [jnp.zeros((8, tq), F32) for _ in range(n_slab)]
        for i in range(n_blk):
            vi, ri = divmod(i, 8)
            si = jnp.broadcast_to(score[i:i + 1, :], (8, tq))
            for v in range(n_slab):
                if v > vi:
                    hit = jnp.where(si >= slabs[v], 1.0, 0.0)
                elif v < vi:
                    hit = jnp.where(si > slabs[v], 1.0, 0.0)
                else:
                    hit = jnp.where(sub > ri, jnp.where(si >= slabs[v], 1.0, 0.0), jnp.where(si > slabs[v], 1.0, 0.0))
                ranks[v] = ranks[v] + hit
        rank = jnp.concatenate(ranks, axis=0)
        sel_bias = jnp.where((rank < float(SEL_COUNT)) & valid, 0.0, NEG_INF)
        if n_blk < NSA_HEAD_DIM:
            sel_bias = jnp.concatenate([sel_bias, jnp.zeros((NSA_HEAD_DIM - n_blk, tq), F32)], axis=0)
        q_g = qs_l[g][g * NSA_HEAD_DIM:(g + 1) * NSA_HEAD_DIM, :]
        qa_ref[g] = jnp.concatenate([q_g, tile4(sel_bias).astype(BF16)], axis=0)

    o_cmp0, imp0 = cmp_branch(0)
    o_cmp1, imp1 = cmp_branch(1)
    select_blocks(0, imp0)
    window_branch(0, o_cmp0)
    select_blocks(1, imp1)
    window_branch(1, o_cmp1)

    kb_diag = s0 // KB

    def scores(kb):
        k0 = pl.multiple_of(kb * KB, KB)
        return [_dot(kvs_ref[pl.ds(k0, KB), KV_KS + g * LANES:KV_KS + (g + 1) * LANES], qa_ref[g])
                for g in groups]

    def accumulate(kb, s_l, first):
        k0 = pl.multiple_of(kb * KB, KB)
        for g in groups:
            va = kvs_ref[pl.ds(k0, KB), KV_VS + g * LANES:KV_VS + (g + 1) * LANES]
            if first:
                m_new = jnp.max(s_l[g], axis=0, keepdims=True)
                acc_ref[g] = _dot_tn(va, jnp.exp2(s_l[g] - m_new).astype(BF16))
            else:
                m_old = m_ref[g]
                m_new = jnp.maximum(m_old, jnp.max(s_l[g], axis=0, keepdims=True))
                acc_ref[g] = (jnp.exp2(m_old - m_new) * acc_ref[g]
                              + _dot_tn(va, jnp.exp2(s_l[g] - m_new).astype(BF16)))
            m_ref[g] = m_new

    causal_bias = tile4(jnp.where(kb_diag * KB + k_col <= t_row, 0.0, NEG_INF))
    s_diag = [s + causal_bias for s in scores(kb_diag)]
    s_next = scores(jnp.maximum(kb_diag - 1, 0))
    accumulate(kb_diag, s_diag, True)
    for g in groups:
        s_ref[g] = s_next[g]

    def sel_body(i, carry):
        kb = kb_diag - i
        s_cur = [s_ref[g] for g in groups]
        s_nxt = scores(jnp.maximum(kb - 1, 0))
        accumulate(kb, s_cur, False)
        for g in groups:
            s_ref[g] = s_nxt[g]
        return carry

    lax.fori_loop(1, kb_diag + 1, sel_body, 0)

    out_t = [part_ref[g] + gate_row(g, 1) * normalized(acc_ref[g], g) for g in groups]
    for r in range(rh):
        cs = slice(r * tq, (r + 1) * tq)
        slab_t = jnp.where(row_half == 0, out_t[0][:, cs], out_t[1][:, cs])
        o_ref[:, r * LANES:(r + 1) * LANES] = slab_t.T.astype(BF16)


def _nsa_call(q_t, kc, vc, kvs, gates_t, ovt, batch, seq):
    assert seq % KB == 0 and seq >= WINDOW + TQ and seq // SEL_BLOCK <= NSA_HEAD_DIM
    nq = seq // TQ
    n_cmp = kc.shape[1]
    row = lambda b, i: (b * nq + i, 0)
    col = lambda b, i: (0, b * nq + i)
    per_b3 = lambda b, i: (b, 0, 0)
    per_b2 = lambda b, i: (b, 0)
    c2 = lambda b, i: (0, 0)
    rows = NSA_GROUP_HEADS * TQ
    return pl.pallas_call(
        _nsa_kernel,
        grid=(batch, nq),
        in_specs=[pl.BlockSpec((NSA_WIDTH, TQ), col),
                  pl.BlockSpec((1, n_cmp, NSA_KV_WIDTH), per_b3),
                  pl.BlockSpec((1, n_cmp, NSA_KV_WIDTH), per_b3),
                  pl.BlockSpec((seq, KV_WIDTH), per_b2),
                  pl.BlockSpec((LANES, TQ), col),
                  pl.BlockSpec(ovt.shape, c2)],
        out_specs=pl.BlockSpec((TQ, NSA_WIDTH), row),
        out_shape=jax.ShapeDtypeStruct((batch * seq, NSA_WIDTH), BF16),
        scratch_shapes=[pltpu.VMEM((NSA_KV_GROUPS, LANES, rows), BF16),
                        pltpu.VMEM((NSA_KV_GROUPS, 1, rows), F32),
                        pltpu.VMEM((NSA_KV_GROUPS, LANES, rows), F32),
                        pltpu.VMEM((NSA_KV_GROUPS, KB, rows), F32),
                        pltpu.VMEM((NSA_KV_GROUPS, LANES, rows), F32)],
        compiler_params=pltpu.CompilerParams(dimension_semantics=("parallel", "arbitrary"),
                                             vmem_limit_bytes=VMEM_LIMIT),
        name="nsa_attention",
    )(q_t, kc, vc, kvs, gates_t, ovt)


def _out_kernel(x_ref, oa_ref, ob_ref, mg_ref, wa_ref, wb_ref, wo_ref, g2_ref, wgu_ref, wd_ref, gf_ref, out_ref):
    tm = x_ref.shape[0]
    halves = [slice(i * 128, (i + 1) * 128) for i in range(tm // 128)]
    st = [{} for _ in halves]

    def merge(i, rows):
        a = _dot(oa_ref[rows, :], wa_ref[...])
        b = _dot(ob_ref[rows, :], wb_ref[...])
        st[i]["merged"] = (_sigmoid(mg_ref[rows, 0:D_MODEL].astype(F32)) * a
                           + _sigmoid(mg_ref[rows, D_MODEL:2 * D_MODEL].astype(F32)) * b).astype(BF16)

    def out_proj(i, rows):
        st[i]["x1"] = x_ref[rows, :] + _dot(st[i]["merged"], wo_ref[...])
        st[i]["h2"] = _rms(st[i]["x1"], g2_ref[...]).astype(BF16)

    def ffn_up(i, rows):
        gate = _dot(st[i]["h2"], wgu_ref[:, 0:FFN_HIDDEN])
        up = _dot(st[i]["h2"], wgu_ref[:, FFN_HIDDEN:2 * FFN_HIDDEN])
        st[i]["act"] = (_silu(gate) * up).astype(BF16)

    def ffn_down(i, rows):
        y = st[i]["x1"] + _dot(st[i]["act"], wd_ref[...])
        out_ref[rows, :] = _rms(y, gf_ref[...])

    for stage in (merge, out_proj, ffn_up, ffn_down):
        for i, rows in enumerate(halves):
            stage(i, rows)


def _out_call(x2d, o_a, o_b, mg, wa, wb, wo, g2, wgu, wd, gf, tm=512):
    t = x2d.shape[0]
    row = lambda i: (i, 0)
    const = lambda i: (0, 0)

    def wspec(w):
        return pl.BlockSpec(w.shape, const, pipeline_mode=pl.Buffered(1))

    return pl.pallas_call(
        _out_kernel,
        grid=(t // tm,),
        in_specs=[pl.BlockSpec((tm, D_MODEL), row),
                  pl.BlockSpec((tm, GDN_WIDTH), row),
                  pl.BlockSpec((tm, NSA_WIDTH), row),
                  pl.BlockSpec((tm, 2 * D_MODEL), row),
                  wspec(wa), wspec(wb), wspec(wo),
                  pl.BlockSpec((1, D_MODEL), const),
                  wspec(wgu), wspec(wd),
                  pl.BlockSpec((1, D_MODEL), const)],
        out_specs=pl.BlockSpec((tm, D_MODEL), row),
        out_shape=jax.ShapeDtypeStruct((t, D_MODEL), F32),
        compiler_params=pltpu.CompilerParams(dimension_semantics=("parallel",),
                                             vmem_limit_bytes=VMEM_LIMIT),
        name="merge_ffn",
    )(x2d, o_a, o_b, mg, wa, wb, wo, g2, wgu, wd, gf)


def _q_head_perm():
    idx = []
    for r in range(NSA_GROUP_HEADS):
        for g in range(NSA_KV_GROUPS):
            base = (g * NSA_GROUP_HEADS + r) * NSA_HEAD_DIM
            idx.extend(range(base, base + NSA_HEAD_DIM))
    return np.asarray(idx, dtype=np.int32)


def _reorder_w_in(w_in):
    sizes = (3 * GDN_WIDTH, GDN_WIDTH, GDN_HEADS, GDN_HEADS, NSA_WIDTH) + (NSA_KV_WIDTH,) * 6 + (
        3 * NSA_HEADS, D_MODEL, D_MODEL)
    offs = np.concatenate([[0], np.cumsum(sizes)])
    seg = lambda i: w_in[:, offs[i]:offs[i + 1]]
    qkv, z, beta, alpha, q, kc, vc, ks, vs, kw, vw, gate, ma, mb = [seg(i) for i in range(14)]
    pad = jnp.zeros((D_MODEL, LANES - 2 * GDN_HEADS - 3 * NSA_HEADS), w_in.dtype)
    q = q[:, _q_head_perm()]
    zero_half = jnp.zeros((D_MODEL, NSA_HEAD_DIM), w_in.dtype)
    ks_split = [ks[:, :NSA_HEAD_DIM], zero_half, ks[:, NSA_HEAD_DIM:], zero_half]
    return jnp.concatenate([qkv, z, q, kc, vc] + ks_split + [vs, kw, vw, ma, mb, beta, alpha, gate, pad],
                           axis=1).astype(BF16)


def _rope_tables(seq):
    half = ROPE_DIM // 2
    inv_freq = ROPE_THETA ** (-jnp.arange(half, dtype=F32) / half)
    ang = jnp.arange(seq, dtype=F32)[:, None] * inv_freq
    cos, sin = jnp.cos(ang), jnp.sin(ang)
    ones = jnp.ones((seq, NSA_HEAD_DIM - ROPE_DIM), F32)
    zeros = jnp.zeros((seq, NSA_HEAD_DIM - ROPE_DIM), F32)
    z8 = jnp.zeros((seq, half), F32)
    c = jnp.concatenate([cos, cos, ones], axis=1)
    s1 = jnp.concatenate([z8, sin, zeros], axis=1)
    s2 = jnp.concatenate([-sin, z8, zeros], axis=1)
    tile2 = lambda a: jnp.concatenate([a, a], axis=1)
    return tile2(c), tile2(s1), tile2(s2)


def _cmp_weights(pos, w1, w2):
    g = NSA_KV_GROUPS
    seg = CMP_BLOCK // 2
    eye = jnp.eye(g, dtype=w1.dtype)
    pos_h = pos.reshape(2, seg, 1, NSA_HEAD_DIM)
    pos_flat = jnp.broadcast_to(pos_h, (2, seg, g, NSA_HEAD_DIM)).reshape(2, seg * g * NSA_HEAD_DIM)
    w1_h = w1.reshape(2, seg, NSA_HEAD_DIM, CMP_HIDDEN)
    w1_bd = jnp.einsum('alds,gk->algdks', w1_h, eye).reshape(2, seg * g * NSA_HEAD_DIM, g * CMP_HIDDEN)
    w2_bd = jnp.einsum('hd,gk->ghkd', w2, eye).reshape(g * CMP_HIDDEN, g * NSA_HEAD_DIM)
    return pos_flat.astype(F32), w1_bd.astype(BF16), w2_bd.astype(BF16)


def _selection_constants(seq):
    n_cmp_pad = seq // CMP_STRIDE
    n_blk = seq // SEL_BLOCK
    cmp_start = np.arange(n_cmp_pad) * CMP_STRIDE
    cmp_end = cmp_start + CMP_BLOCK - 1
    sel_start = np.arange(n_blk) * SEL_BLOCK
    ovt = ((cmp_start[None, :] <= sel_start[:, None] + SEL_BLOCK - 1) & (cmp_end[None, :] >= sel_start[:, None]))
    ovt[:, n_cmp_pad - 1] = False
    return jnp.asarray(ovt, BF16)


def _lane_vector(values, offset):
    v = jnp.zeros((1, LANES), F32)
    return v.at[0, offset:offset + values.shape[0]].set(values.astype(F32))


def _hybrid_block(x, mix_norm_gain, w_in, gdn_conv_w, gdn_a_log, gdn_dt_bias, gdn_out_norm_gain,
                  cmp_pos_k, cmp_w1_k, cmp_w2_k, cmp_pos_v, cmp_w1_v, cmp_w2_v,
                  w_branch_gdn, w_branch_nsa, w_out, ffn_norm_gain, w_gate_up, w_down, final_norm_gain):
    batch, seq, _ = x.shape
    x2d = x.reshape(batch * seq, D_MODEL)
    rc, rs1, rs2 = _rope_tables(seq)
    qkv, z, kc, vc, kvs, mg, small, q_t, gates_t = _inproj_call(
        x2d, mix_norm_gain.reshape(1, D_MODEL), _reorder_w_in(w_in), rc, rs1, rs2,
        gdn_conv_w.astype(F32), seq)

    o_a = _gdn_call(qkv, z, small,
                    _lane_vector(gdn_a_log, SMALL_ALPHA), _lane_vector(gdn_dt_bias, SMALL_ALPHA),
                    gdn_out_norm_gain.reshape(1, GDN_HEAD_DIM).astype(F32), batch, seq)

    seg_width = (CMP_BLOCK // 2) * NSA_KV_WIDTH
    kc3 = kc.reshape(batch, seq // CMP_STRIDE, seg_width)
    vc3 = vc.reshape(batch, seq // CMP_STRIDE, seg_width)
    kcc, vcc = _cmp_call(kc3, vc3, *_cmp_weights(cmp_pos_k, cmp_w1_k, cmp_w2_k),
                         *_cmp_weights(cmp_pos_v, cmp_w1_v, cmp_w2_v))
    o_b = _nsa_call(q_t, kcc, vcc, kvs, gates_t, _selection_constants(seq), batch, seq)

    out = _out_call(x2d, o_a, o_b, mg,
                    w_branch_gdn.astype(BF16), w_branch_nsa[_q_head_perm(), :].astype(BF16), w_out.astype(BF16),
                    ffn_norm_gain.reshape(1, D_MODEL), w_gate_up.astype(BF16), w_down.astype(BF16),
                    final_norm_gain.reshape(1, D_MODEL))
    return out.reshape(batch, seq, D_MODEL)


def kernel(x, mix_norm_gain, w_in, gdn_conv_w, gdn_a_log, gdn_dt_bias, gdn_out_norm_gain, cmp_pos_k, cmp_w1_k,
           cmp_w2_k, cmp_pos_v, cmp_w1_v, cmp_w2_v, w_branch_gdn, w_branch_nsa, w_out, ffn_norm_gain, w_gate_up,
           w_down, final_norm_gain):
    assert mix_norm_gain.shape[0] == 1, "single-layer block"
    return _hybrid_block(x, mix_norm_gain[0], w_in[0], gdn_conv_w[0], gdn_a_log[0], gdn_dt_bias[0],
                         gdn_out_norm_gain[0], cmp_pos_k[0], cmp_w1_k[0], cmp_w2_k[0], cmp_pos_v[0], cmp_w1_v[0],
                         cmp_w2_v[0], w_branch_gdn[0], w_branch_nsa[0], w_out[0], ffn_norm_gain[0], w_gate_up[0],
                         w_down[0], final_norm_gain)
```

```python
import functools

import numpy as np
import jax
import jax.numpy as jnp
from jax import lax
from jax.experimental import pallas as pl
from jax.experimental.pallas import tpu as pltpu

F32 = jnp.float32
BF16 = jnp.bfloat16

D_MODEL = 1024
NORM_EPS = 1e-6
GDN_HEADS = 4
GDN_HEAD_DIM = 128
GDN_WIDTH = GDN_HEADS * GDN_HEAD_DIM
GDN_CONV = 4
GDN_CHUNK = 64
INV_BASE = 8
NSA_HEADS = 8
NSA_KV_GROUPS = 2
NSA_GROUP_HEADS = NSA_HEADS // NSA_KV_GROUPS
NSA_HEAD_DIM = 64
NSA_WIDTH = NSA_HEADS * NSA_HEAD_DIM
NSA_KV_WIDTH = NSA_KV_GROUPS * NSA_HEAD_DIM
CMP_BLOCK = 32
CMP_STRIDE = 16
CMP_HIDDEN = 128
SEL_BLOCK = 64
SEL_COUNT = 16
WINDOW = 512
ROPE_THETA = 500000.0
ROPE_DIM = NSA_HEAD_DIM // 4
FORCED_SCORE = 1000.0
NEG_INF = -1e30
FFN_HIDDEN = 2816

LANES = 128
VMEM_LIMIT = 56 * 1024 * 1024

C_QKV = (0, 1536)
C_Z = (1536, 2048)
C_Q = (2048, 2560)
C_KC = (2560, 2688)
C_VC = (2688, 2816)
C_KVS = (2816, 3456)
C_MG = (3456, 5504)
C_SMALL = (5504, 5632)
IN_WIDTH_PADDED = 5632
SMALL_BETA = 0
SMALL_ALPHA = 4
SMALL_GATE = 8

KV_KS = 0
KV_VS = 256
KV_KW = 512
KV_VW = 640
KV_WIDTH = 896
LOG2E = 1.4426950408889634
Q_SCALE = NSA_HEAD_DIM ** -0.5 * LOG2E

PART_ROWS = 128
TQ = 128
KB = 512


def _dot(a, b):
    return jnp.dot(a, b, preferred_element_type=F32)


def _dot_nt(a, b):
    return lax.dot_general(a, b, (((1,), (1,)), ((), ())), preferred_element_type=F32)


def _dot_tn(a, b):
    return lax.dot_general(a, b, (((0,), (0,)), ((), ())), preferred_element_type=F32)


def _sigmoid(x):
    return 1.0 / (1.0 + jnp.exp(-x))


def _silu(x):
    return x * _sigmoid(x)


def _rms(x, gain):
    return x * lax.rsqrt(jnp.mean(x * x, axis=-1, keepdims=True) + NORM_EPS) * gain


def _rope_slab(y, c, s1, s2):
    return y * c + pltpu.roll(y, 8, 1) * s1 + pltpu.roll(y, LANES - 8, 1) * s2


def _inproj_kernel(x_ref, gain_ref, w_ref, rc_ref, rs1_ref, rs2_ref, convw_ref,
                   qkv_ref, z_ref, kc_ref, vc_ref, kvs_ref, mg_ref, small_ref, q_ref, gates_ref,
                   xbuf, segbuf, *, seq):
    tm = x_ref.shape[0]
    hb = _rms(x_ref[...], gain_ref[...]).astype(BF16)

    def proj(cols):
        return _dot(hb, w_ref[:, cols[0]:cols[1]])

    first_tile = pl.program_id(0) % (seq // tm) == 0

    @pl.when(first_tile)
    def _():
        xbuf[0:8, :] = jnp.zeros((8, 3 * GDN_WIDTH), F32)

    @pl.when(jnp.logical_not(first_tile))
    def _():
        xbuf[0:8, :] = xbuf[tm:tm + 8, :]

    cw = convw_ref[...]
    slab_w = 2 * GDN_HEAD_DIM

    def gdn_slab(k):
        lo = k * slab_w
        cols = slice(lo, lo + slab_w)
        xbuf[8:8 + tm, cols] = proj((C_QKV[0] + lo, C_QKV[0] + lo + slab_w))
        conv = xbuf[8:8 + tm, cols] * cw[GDN_CONV - 1:GDN_CONV, cols]
        for jj in range(GDN_CONV - 1):
            off = 8 - (GDN_CONV - 1) + jj
            conv = conv + xbuf[off:off + tm, cols] * cw[jj:jj + 1, cols]
        act = _silu(conv)
        for h in range(lo // GDN_HEAD_DIM, (lo + slab_w) // GDN_HEAD_DIM):
            a = act[:, h * GDN_HEAD_DIM - lo:(h + 1) * GDN_HEAD_DIM - lo]
            if h < 2 * GDN_HEADS:
                a = a * lax.rsqrt(jnp.sum(a * a, axis=-1, keepdims=True) + NORM_EPS)
            if h < GDN_HEADS:
                a = a * (GDN_HEAD_DIM ** -0.5)
            qkv_ref[:, h * GDN_HEAD_DIM:(h + 1) * GDN_HEAD_DIM] = a.astype(BF16)

    c, s1, s2 = rc_ref[...], rs1_ref[...], rs2_ref[...]

    def task_z():
        z_ref[...] = proj(C_Z).astype(BF16)

    def task_merge(half):
        lo = C_MG[0] + half * D_MODEL
        mg_ref[:, half * D_MODEL:(half + 1) * D_MODEL] = proj((lo, lo + D_MODEL)).astype(BF16)

    def task_small():
        small = proj(C_SMALL)
        small_ref[...] = small
        gates = _sigmoid(small)
        for ch in range(tm // LANES):
            tok = slice(ch * LANES, (ch + 1) * LANES)
            gates_ref[:, tok] = gates[tok, :].T
        segbuf[0] = proj(C_VC)
        segbuf[1] = _rope_slab(proj(C_KC), c, s1, s2)
        n_seg = tm // CMP_STRIDE
        for which, out_ref in ((0, vc_ref), (1, kc_ref)):
            for l in range(CMP_STRIDE):
                rows = segbuf[which, pl.ds(l, n_seg, stride=CMP_STRIDE), :]
                out_ref[:, l * NSA_KV_WIDTH:(l + 1) * NSA_KV_WIDTH] = rows.astype(BF16)

    def task_q():
        q = proj(C_Q)
        for r in range(NSA_GROUP_HEADS):
            sl = slice(r * LANES, (r + 1) * LANES)
            roped = _rope_slab(q[:, sl], c, s1, s2) * Q_SCALE
            for ch in range(tm // LANES):
                tok = slice(ch * LANES, (ch + 1) * LANES)
                q_ref[sl, tok] = roped[tok, :].T.astype(BF16)

    def task_kv():
        tok = (pl.program_id(0) % (seq // tm)) * tm + lax.broadcasted_iota(jnp.int32, (tm, 1), 0)
        lane = lax.broadcasted_iota(jnp.int32, (tm, LANES), 1)
        group0 = lane < NSA_HEAD_DIM
        block_onehot = jnp.where(lane == NSA_HEAD_DIM + tok // SEL_BLOCK, 1.0, 0.0)
        kvs = proj(C_KVS)
        for g in range(NSA_KV_GROUPS):
            ks = _rope_slab(kvs[:, g * LANES:(g + 1) * LANES], c, s1, s2)
            kvs_ref[:, KV_KS + g * LANES:KV_KS + (g + 1) * LANES] = jnp.where(group0, ks, block_onehot).astype(BF16)
        vs, kw, vw = kvs[:, 256:384], kvs[:, 384:512], kvs[:, 512:640]
        kvs_ref[:, KV_VS:KV_VS + LANES] = jnp.where(group0, vs, 1.0).astype(BF16)
        kvs_ref[:, KV_VS + LANES:KV_VS + 2 * LANES] = jnp.where(group0, 1.0, vs).astype(BF16)
        kvs_ref[:, KV_KW:KV_KW + LANES] = _rope_slab(kw, c, s1, s2).astype(BF16)
        kvs_ref[:, KV_VW:KV_VW + LANES] = jnp.where(group0, vw, 1.0).astype(BF16)
        kvs_ref[:, KV_VW + LANES:KV_VW + 2 * LANES] = jnp.where(group0, 1.0, vw).astype(BF16)

    others = [task_z, functools.partial(task_merge, 0), task_q, functools.partial(task_merge, 1), task_kv,
              task_small]
    n_slabs = 3 * GDN_WIDTH // slab_w
    assert len(others) == n_slabs
    for k in range(n_slabs):
        gdn_slab(k)
        others[k]()


def _inproj_call(x2d, gain, w_r, rc, rs1, rs2, conv_w, seq, tm=512):
    assert seq // SEL_BLOCK <= LANES, "selection-block one-hot must fit one lane tile"
    t = x2d.shape[0]
    n_seq_tiles = seq // tm
    row = lambda i: (i, 0)
    const = lambda i: (0, 0)
    tab = lambda i: (i % n_seq_tiles, 0)
    seg = CMP_STRIDE * NSA_KV_WIDTH
    outs = [(tm, t, 1536, BF16), (tm, t, 512, BF16),
            (tm // CMP_STRIDE, t // CMP_STRIDE, seg, BF16), (tm // CMP_STRIDE, t // CMP_STRIDE, seg, BF16),
            (tm, t, KV_WIDTH, BF16), (tm, t, 2048, BF16), (tm, t, 128, F32)]
    outs_t = [(NSA_WIDTH, BF16), (LANES, F32)]
    col = lambda i: (0, i)
    return pl.pallas_call(
        functools.partial(_inproj_kernel, seq=seq),
        grid=(t // tm,),
        in_specs=[pl.BlockSpec((tm, D_MODEL), row),
                  pl.BlockSpec((1, D_MODEL), const),
                  pl.BlockSpec((D_MODEL, IN_WIDTH_PADDED), const, pipeline_mode=pl.Buffered(1)),
                  pl.BlockSpec((tm, LANES), tab),
                  pl.BlockSpec((tm, LANES), tab),
                  pl.BlockSpec((tm, LANES), tab),
                  pl.BlockSpec((GDN_CONV, 3 * GDN_WIDTH), const)],
        out_specs=([pl.BlockSpec((r, w), row) for r, _, w, _ in outs]
                   + [pl.BlockSpec((h, tm), col) for h, _ in outs_t]),
        out_shape=([jax.ShapeDtypeStruct((n, w), d) for _, n, w, d in outs]
                   + [jax.ShapeDtypeStruct((h, t), d) for h, d in outs_t]),
        scratch_shapes=[pltpu.VMEM((tm + 8, 3 * GDN_WIDTH), F32),
                        pltpu.VMEM((2, tm, NSA_KV_WIDTH), F32)],
        compiler_params=pltpu.CompilerParams(dimension_semantics=("arbitrary",),
                                             vmem_limit_bytes=VMEM_LIMIT),
        name="inproj",
    )(x2d, gain, w_r, rc, rs1, rs2, conv_w)


def _gdn_kernel(qkv_ref, z_ref, small_ref, alog_ref, dtb_ref, ogain_ref, o_ref,
                state, u_ref, w_ref, qk_ref, qg_ref, kd_ref, gl_ref, *, blocks_per_seq):
    t = pl.program_id(0)
    bufs = (u_ref, w_ref, qk_ref, qg_ref, kd_ref, gl_ref)

    @pl.when(t == 0)
    def _():
        state[...] = jnp.zeros_like(state)
        for ref in bufs:
            ref[...] = jnp.zeros_like(ref)

    fresh = (t + blocks_per_seq - 1) % blocks_per_seq == 0
    for slot in range(2):
        @pl.when(t % 2 == slot)
        def _(slot=slot):
            _gdn_step(qkv_ref, z_ref, small_ref, alog_ref, dtb_ref, ogain_ref, o_ref, state, bufs, slot, fresh)


def _gdn_step(qkv_ref, z_ref, small_ref, alog_ref, dtb_ref, ogain_ref, o_ref, state, bufs, cur, fresh):
    u_ref, w_ref, qk_ref, qg_ref, kd_ref, gl_ref = bufs
    prev = 1 - cur
    cb = qkv_ref.shape[0]
    c = GDN_CHUNK
    n_chunks = cb // c
    heads = range(GDN_HEADS)

    row = lax.broadcasted_iota(jnp.int32, (c, c), 0)
    col = lax.broadcasted_iota(jnp.int32, (c, c), 1)
    causal = row >= col
    strict = row > col
    tril = causal.astype(F32)
    eye = (row == col).astype(F32)
    neg_decay_rate = -jnp.exp(alog_ref[...])
    dtb = dtb_ref[...]
    ogain = ogain_ref[...]

    pairs = [(ci, h) for ci in range(n_chunks) for h in heads]
    n_pairs = len(pairs)
    A = {}

    def a_prepare():
        beta_c, gc_c, gct_c = [], [], []
        for ci in range(n_chunks):
            sm = small_ref[ci * c:(ci + 1) * c, :]
            beta_c.append(_sigmoid(sm))
            xg = sm + dtb
            softplus = jnp.maximum(xg, 0.0) + jnp.log(1.0 + jnp.exp(-jnp.abs(xg)))
            gc_all = jnp.dot(tril, neg_decay_rate * softplus, precision=lax.Precision.HIGHEST,
                             preferred_element_type=F32)
            gc_c.append(gc_all)
            gct_c.append(gc_all.T)

        def head_cols(base, ci, h):
            lo = base + h * GDN_HEAD_DIM
            return qkv_ref[ci * c:(ci + 1) * c, lo:lo + GDN_HEAD_DIM].astype(F32)

        A["qb"], A["kbf"], A["vb"], A["kbeg"], A["decay"], a_l = [], [], [], [], [], []
        for i, (ci, h) in enumerate(pairs):
            q = head_cols(0, ci, h)
            k = head_cols(GDN_WIDTH, ci, h)
            v = head_cols(2 * GDN_WIDTH, ci, h)
            beta = beta_c[ci][:, SMALL_BETA + h:SMALL_BETA + h + 1]
            gc = gc_c[ci][:, SMALL_ALPHA + h:SMALL_ALPHA + h + 1]
            gr = gct_c[ci][SMALL_ALPHA + h:SMALL_ALPHA + h + 1, :]
            decay = jnp.where(causal, jnp.exp(jnp.where(causal, gc - gr, 0.0)), 0.0)
            eg = jnp.exp(gc)
            g_last = gc[c - 1:c, :]
            kb = k * beta
            kbf = k.astype(BF16)
            qg_ref[cur, i] = (q * eg).astype(BF16)
            kd_ref[cur, i] = (k * jnp.exp(g_last - gc)).astype(BF16)
            gl_ref[cur, i] = jnp.broadcast_to(jnp.exp(g_last), gl_ref.shape[2:])
            A["qb"].append(q.astype(BF16))
            A["kbf"].append(kbf)
            A["vb"].append((v * beta).astype(BF16))
            A["kbeg"].append((kb * eg).astype(BF16))
            A["decay"].append(decay)
            a_l.append(jnp.where(strict, _dot_nt(kb.astype(BF16), kbf) * decay, 0.0))
        A["a"] = a_l
        diag_blocks = [jnp.where(row // INV_BASE == col // INV_BASE, a, 0.0) for a in a_l]
        A["t"] = [eye - d for d in diag_blocks]
        A["p"] = diag_blocks

    def a_qk():
        for i in range(n_pairs):
            qk_ref[cur, i] = (_dot_nt(A["qb"][i], A["kbf"][i]) * A["decay"][i]).astype(BF16)

    def a_square():
        A["p"] = [_dot(p.astype(BF16), p.astype(BF16)) for p in A["p"]]

    def a_extend():
        A["t"] = [t_ + _dot(t_.astype(BF16), p.astype(BF16)) for t_, p in zip(A["t"], A["p"])]

    def a_u():
        A["tb"] = [t_.astype(BF16) for t_ in A["t"]]
        for i in range(n_pairs):
            u_ref[cur, i] = _dot(A["tb"][i], A["vb"][i])

    def a_w():
        for i in range(n_pairs):
            w_ref[cur, i] = _dot(A["tb"][i], A["kbeg"][i]).astype(BF16)

    def a_link(size):
        link = (row // (2 * size) == col // (2 * size)) & ((row // size) % 2 == 1) & ((col // size) % 2 == 0)
        A["dn"] = [_dot(t_.astype(BF16), jnp.where(link, a, 0.0).astype(BF16)) for t_, a in zip(A["t"], A["a"])]

    def a_merge():
        A["t"] = [t_ - _dot(dn.astype(BF16), t_.astype(BF16)) for t_, dn in zip(A["t"], A["dn"])]

    a_stages = [a_prepare, a_qk] + [a_square, a_extend] * (INV_BASE.bit_length() - 2)
    size = INV_BASE
    while size < c:
        a_stages += [functools.partial(a_link, size), a_merge]
        size *= 2
    a_stages += [a_u, a_w]

    B = {"s": [jnp.where(fresh, 0.0, state[h]) for h in heads]}

    def b_new_values(ci):
        idx = [ci * GDN_HEADS + h for h in heads]
        B["sb"] = [s.astype(BF16) for s in B["s"]]
        B["vn"] = [(u_ref[prev, i] - _dot(w_ref[prev, i], sb)).astype(BF16) for i, sb in zip(idx, B["sb"])]

    def b_output(ci):
        idx = [ci * GDN_HEADS + h for h in heads]
        o_l = [_dot(qg_ref[prev, i], sb) + _dot(qk_ref[prev, i], vn) for i, sb, vn in zip(idx, B["sb"], B["vn"])]
        B["s"] = [s * gl_ref[prev, i][0:1, :] + _dot_tn(kd_ref[prev, i], vn)
                  for i, s, vn in zip(idx, B["s"], B["vn"])]
        for h, o in zip(heads, o_l):
            hs = slice(h * GDN_HEAD_DIM, (h + 1) * GDN_HEAD_DIM)
            zh = z_ref[ci * c:(ci + 1) * c, hs].astype(F32)
            o = o * lax.rsqrt(jnp.mean(o * o, axis=-1, keepdims=True) + NORM_EPS) * ogain * _silu(zh)
            o_ref[ci * c:(ci + 1) * c, hs] = o.astype(BF16)

    b_stages = []
    for ci in range(n_chunks):
        b_stages += [functools.partial(b_new_values, ci), functools.partial(b_output, ci)]

    emitted_b = 0
    for k, stage in enumerate(a_stages):
        while emitted_b < len(b_stages) and emitted_b * len(a_stages) <= k * len(b_stages):
            b_stages[emitted_b]()
            emitted_b += 1
        stage()
    for stage in b_stages[emitted_b:]:
        stage()
    for h in heads:
        state[h] = B["s"][h]


def _gdn_call(qkv, z, small, alog_v, dtb_v, ogain, batch, seq, cb=256):
    blocks_per_seq = seq // cb
    n_blocks = batch * blocks_per_seq
    n_pairs = (cb // GDN_CHUNK) * GDN_HEADS
    c, d = GDN_CHUNK, GDN_HEAD_DIM
    phase_a_block = lambda t: (jnp.minimum(t, n_blocks - 1), 0)
    phase_b_block = lambda t: (jnp.maximum(t - 1, 0), 0)
    const = lambda t: (0, 0)
    return pl.pallas_call(
        functools.partial(_gdn_kernel, blocks_per_seq=blocks_per_seq),
        grid=(n_blocks + 1,),
        in_specs=[pl.BlockSpec((cb, 3 * GDN_WIDTH), phase_a_block),
                  pl.BlockSpec((cb, GDN_WIDTH), phase_b_block),
                  pl.BlockSpec((cb, LANES), phase_a_block),
                  pl.BlockSpec((1, LANES), const),
                  pl.BlockSpec((1, LANES), const),
                  pl.BlockSpec((1, GDN_HEAD_DIM), const)],
        out_specs=pl.BlockSpec((cb, GDN_WIDTH), phase_b_block),
        out_shape=jax.ShapeDtypeStruct((batch * seq, GDN_WIDTH), BF16),
        scratch_shapes=[pltpu.VMEM((GDN_HEADS, d, d), F32),
                        pltpu.VMEM((2, n_pairs, c, d), F32),
                        pltpu.VMEM((2, n_pairs, c, d), BF16),
                        pltpu.VMEM((2, n_pairs, c, c), BF16),
                        pltpu.VMEM((2, n_pairs, c, d), BF16),
                        pltpu.VMEM((2, n_pairs, c, d), BF16),
                        pltpu.VMEM((2, n_pairs, 8, d), F32)],
        compiler_params=pltpu.CompilerParams(dimension_semantics=("arbitrary",),
                                             vmem_limit_bytes=VMEM_LIMIT),
        name="gdn",
    )(qkv, z, small, alog_v, dtb_v, ogain)


def _cmp_kernel(kc_ref, vc_ref, posk_ref, w1k_ref, w2k_ref, posv_ref, w1v_ref, w2v_ref, kco_ref, vco_ref):
    for x_ref, pos_ref, w1_ref, w2_ref, out_ref in ((kc_ref, posk_ref, w1k_ref, w2k_ref, kco_ref),
                                                    (vc_ref, posv_ref, w1v_ref, w2v_ref, vco_ref)):
        a = x_ref[0].astype(F32)
        n = a.shape[0]
        p0 = _dot((a + pos_ref[0:1, :]).astype(BF16), w1_ref[0])
        p1 = _dot((a + pos_ref[1:2, :]).astype(BF16), w1_ref[1])
        hid = _silu(p0 + pltpu.roll(p1, n - 1, 0))
        out_ref[0] = _dot(hid.astype(BF16), w2_ref[...]).astype(BF16)


def _cmp_call(kc3, vc3, posk, w1k, w2k, posv, w1v, w2v):
    b, n, width = kc3.shape
    hid2 = NSA_KV_GROUPS * CMP_HIDDEN
    per_b = lambda i: (i, 0, 0)
    c2 = lambda i: (0, 0)
    c3 = lambda i: (0, 0, 0)
    wspecs = [pl.BlockSpec((2, width), c2), pl.BlockSpec((2, width, hid2), c3), pl.BlockSpec((hid2, NSA_KV_WIDTH), c2)]
    return pl.pallas_call(
        _cmp_kernel,
        grid=(b,),
        in_specs=[pl.BlockSpec((1, n, width), per_b), pl.BlockSpec((1, n, width), per_b)] + wspecs + wspecs,
        out_specs=[pl.BlockSpec((1, n, NSA_KV_WIDTH), per_b)] * 2,
        out_shape=[jax.ShapeDtypeStruct((b, n, NSA_KV_WIDTH), BF16)] * 2,
        compiler_params=pltpu.CompilerParams(dimension_semantics=("parallel",),
                                             vmem_limit_bytes=VMEM_LIMIT),
        name="nsa_compress",
    )(kc3, vc3, posk, w1k, w2k, posv, w1v, w2v)


def _nsa_kernel(q_ref, kc_ref, vc_ref, kvs_ref, gates_ref, ovt_ref, o_ref,
                qa_ref, m_ref, acc_ref, s_ref, part_ref):
    tq = q_ref.shape[1]
    rh = NSA_GROUP_HEADS
    n_cmp = kc_ref.shape[1]
    n_blk = ovt_ref.shape[0]
    qi = pl.program_id(1)
    s0 = qi * tq

    row_half = lax.broadcasted_iota(jnp.int32, (LANES, tq), 0) // NSA_HEAD_DIM
    t_row = s0 + lax.broadcasted_iota(jnp.int32, (1, tq), 1)
    k_col = lax.broadcasted_iota(jnp.int32, (KB, 1), 0)
    q_t = [q_ref[r * LANES:(r + 1) * LANES, :].astype(F32) for r in range(rh)]
    gates_t = gates_ref[...]
    kc = kc_ref[0]
    vc = vc_ref[0]
    cmp_end = lax.broadcasted_iota(jnp.int32, (n_cmp, 1), 0) * CMP_STRIDE + (CMP_BLOCK - 1)
    vis1 = cmp_end <= t_row
    vis = jnp.concatenate([vis1] * rh, axis=1)

    blk = lax.broadcasted_iota(jnp.int32, (n_blk, tq), 0)
    cur = (s0 + lax.broadcasted_iota(jnp.int32, (n_blk, tq), 1)) // SEL_BLOCK
    valid = blk <= cur
    forced = (blk == 0) | (blk == cur) | (blk == cur - 1)

    def tile4(a):
        return jnp.concatenate([a] * rh, axis=1)

    def normalized(acc, g):
        l_row = acc[(1 - g) * NSA_HEAD_DIM:(1 - g) * NSA_HEAD_DIM + 1, :]
        return acc * (1.0 / l_row)

    groups = range(NSA_KV_GROUPS)
    def gate_row(g, branch):
        cols = [SMALL_GATE + (g * rh + r) * 3 + branch for r in range(rh)]
        return jnp.concatenate([gates_t[c:c + 1, :] for c in cols], axis=1)

    qs_l = [jnp.concatenate([jnp.where(row_half == g, q_t[r], 0.0) for r in range(rh)], axis=1).astype(BF16)
            for g in groups]
    span = WINDOW + TQ
    w0 = pl.multiple_of(jnp.maximum(s0 - WINDOW, 0), TQ)
    kw = kvs_ref[pl.ds(w0, span), KV_KW:KV_KW + LANES]
    s_cmp = [_dot(kc, qs_l[g]) for g in groups]
    s_win = [_dot(kw, qs_l[g]) for g in groups]

    def cmp_branch(g):
        s = jnp.where(vis, s_cmp[g], NEG_INF)
        m = jnp.max(s, axis=0, keepdims=True)
        e = jnp.where(vis, jnp.exp2(s - m), 0.0)
        den = jnp.sum(e, axis=0, keepdims=True)
        p = e * (1.0 / jnp.where(den > 0.0, den, 1.0))
        o_cmp = _dot_tn(vc, p.astype(BF16))
        p_sum = p[:, 0:tq]
        for r in range(1, rh):
            p_sum = p_sum + p[:, r * tq:(r + 1) * tq]
        p_hi = p_sum.astype(BF16)
        p_lo = (p_sum - p_hi.astype(F32)).astype(BF16)
        ovt = ovt_ref[...]
        return o_cmp, _dot(ovt, p_hi) + _dot(ovt, p_lo)

    rel = t_row - (w0 + lax.broadcasted_iota(jnp.int32, (span, 1), 0))
    win_bias = tile4(jnp.where((rel >= 0) & (rel < WINDOW), 0.0, NEG_INF))

    def window_branch(g, o_cmp):
        vw = kvs_ref[pl.ds(w0, span), KV_VW + g * LANES:KV_VW + (g + 1) * LANES]
        sw = s_win[g] + win_bias
        pw = jnp.exp2(sw - jnp.max(sw, axis=0, keepdims=True))
        o_win = normalized(_dot_tn(vw, pw.astype(BF16)), g)
        part_ref[g] = gate_row(g, 0) * o_cmp + gate_row(g, 2) * o_win

    sub = lax.broadcasted_iota(jnp.int32, (8, tq), 0)
    n_slab = n_blk // 8

    def select_blocks(g, imp):
        score = jnp.where(valid, jnp.where(forced, FORCED_SCORE, imp), -1.0)
        slabs = [score[8 * v:8 * (v + 1), :] for v in range(n_slab)]
        ranks = [jnp.zeros((8, tq), F32) for _ in range(n_slab)]
        for i in range(n_blk):
            vi, ri = divmod(i, 8)
            si = jnp.broadcast_to(score[i:i + 1, :], (8, tq))
            for v in range(n_slab):
                if v > vi:
                    hit = jnp.where(si >= slabs[v], 1.0, 0.0)
                elif v < vi:
                    hit = jnp.where(si > slabs[v], 1.0, 0.0)
                else:
                    hit = jnp.where(sub > ri, jnp.where(si >= slabs[v], 1.0, 0.0), jnp.where(si > slabs[v], 1.0, 0.0))
                ranks[v] = ranks[v] + hit
        rank = jnp.concatenate(ranks, axis=0)
        sel_bias = jnp.where((rank < float(SEL_COUNT)) & valid, 0.0, NEG_INF)
        if n_blk < NSA_HEAD_DIM:
            sel_bias = jnp.concatenate([sel_bias, jnp.zeros((NSA_HEAD_DIM - n_blk, tq), F32)], axis=0)
        q_g = qs_l[g][g * NSA_HEAD_DIM:(g + 1) * NSA_HEAD_DIM, :]
        qa_ref[g] = jnp.concatenate([q_g, tile4(sel_bias).astype(BF16)], axis=0)

    o_cmp0, imp0 = cmp_branch(0)
    o_cmp1, imp1 = cmp_branch(1)
    select_blocks(0, imp0)
    window_branch(0, o_cmp0)
    select_blocks(1, imp1)
    window_branch(1, o_cmp1)

    kb_diag = s0 // KB

    def scores(kb):
        k0 = pl.multiple_of(kb * KB, KB)
        return [_dot(kvs_ref[pl.ds(k0, KB), KV_KS + g * LANES:KV_KS + (g + 1) * LANES], qa_ref[g])
                for g in groups]

    def accumulate(kb, s_l, first):
        k0 = pl.multiple_of(kb * KB, KB)
        for g in groups:
            va = kvs_ref[pl.ds(k0, KB), KV_VS + g * LANES:KV_VS + (g + 1) * LANES]
            if first:
                m_new = jnp.max(s_l[g], axis=0, keepdims=True)
                acc_ref[g] = _dot_tn(va, jnp.exp2(s_l[g] - m_new).astype(BF16))
            else:
                m_old = m_ref[g]
                m_new = jnp.maximum(m_old, jnp.max(s_l[g], axis=0, keepdims=True))
                acc_ref[g] = (jnp.exp2(m_old - m_new) * acc_ref[g]
                              + _dot_tn(va, jnp.exp2(s_l[g] - m_new).astype(BF16)))
            m_ref[g] = m_new

    causal_bias = tile4(jnp.where(kb_diag * KB + k_col <= t_row, 0.0, NEG_INF))
    s_diag = [s + causal_bias for s in scores(kb_diag)]
    s_next = scores(jnp.maximum(kb_diag - 1, 0))
    accumulate(kb_diag, s_diag, True)
    for g in groups:
        s_ref[g] = s_next[g]

    def sel_body(i, carry):
        kb = kb_diag - i
        s_cur = [s_ref[g] for g in groups]
        s_nxt = scores(jnp.maximum(kb - 1, 0))
        accumulate(kb, s_cur, False)
        for g in groups:
            s_ref[g] = s_nxt[g]
        return carry

    lax.fori_loop(1, kb_diag + 1, sel_body, 0)

    out_t = [part_ref[g] + gate_row(g, 1) * normalized(acc_ref[g], g) for g in groups]
    for r in range(rh):
        cs = slice(r * tq, (r + 1) * tq)
        slab_t = jnp.where(row_half == 0, out_t[0][:, cs], out_t[1][:, cs])
        o_ref[:, r * LANES:(r + 1) * LANES] = slab_t.T.astype(BF16)


def _nsa_call(q_t, kc, vc, kvs, gates_t, ovt, batch, seq):
    assert seq % KB == 0 and seq >= WINDOW + TQ and seq // SEL_BLOCK <= NSA_HEAD_DIM
    nq = seq // TQ
    n_cmp = kc.shape[1]
    row = lambda b, i: (b * nq + i, 0)
    col = lambda b, i: (0, b * nq + i)
    per_b3 = lambda b, i: (b, 0, 0)
    per_b2 = lambda b, i: (b, 0)
    c2 = lambda b, i: (0, 0)
    rows = NSA_GROUP_HEADS * TQ
    return pl.pallas_call(
        _nsa_kernel,
        grid=(batch, nq),
        in_specs=[pl.BlockSpec((NSA_WIDTH, TQ), col),
                  pl.BlockSpec((1, n_cmp, NSA_KV_WIDTH), per_b3),
                  pl.BlockSpec((1, n_cmp, NSA_KV_WIDTH), per_b3),
                  pl.BlockSpec((seq, KV_WIDTH), per_b2),
                  pl.BlockSpec((LANES, TQ), col),
                  pl.BlockSpec(ovt.shape, c2)],
        out_specs=pl.BlockSpec((TQ, NSA_WIDTH), row),
        out_shape=jax.ShapeDtypeStruct((batch * seq, NSA_WIDTH), BF16),
        scratch_shapes=[pltpu.VMEM((NSA_KV_GROUPS, LANES, rows), BF16),
                        pltpu.VMEM((NSA_KV_GROUPS, 1, rows), F32),
                        pltpu.VMEM((NSA_KV_GROUPS, LANES, rows), F32),
                        pltpu.VMEM((NSA_KV_GROUPS, KB, rows), F32),
                        pltpu.VMEM((NSA_KV_GROUPS, LANES, rows), F32)],
        compiler_params=pltpu.CompilerParams(dimension_semantics=("parallel", "arbitrary"),
                                             vmem_limit_bytes=VMEM_LIMIT),
        name="nsa_attention",
    )(q_t, kc, vc, kvs, gates_t, ovt)


def _out_kernel(x_ref, oa_ref, ob_ref, mg_ref, wa_ref, wb_ref, wo_ref, g2_ref, wgu_ref, wd_ref, gf_ref, out_ref):
    tm = x_ref.shape[0]
    halves = [slice(i * PART_ROWS, (i + 1) * PART_ROWS) for i in range(tm // PART_ROWS)]
    st = [{} for _ in halves]

    def merge(i, rows):
        a = _dot(oa_ref[rows, :], wa_ref[...])
        b = _dot(ob_ref[rows, :], wb_ref[...])
        st[i]["merged"] = (_sigmoid(mg_ref[rows, 0:D_MODEL].astype(F32)) * a
                           + _sigmoid(mg_ref[rows, D_MODEL:2 * D_MODEL].astype(F32)) * b).astype(BF16)

    def out_proj(i, rows):
        st[i]["x1"] = x_ref[rows, :] + _dot(st[i]["merged"], wo_ref[...])
        st[i]["h2"] = _rms(st[i]["x1"], g2_ref[...]).astype(BF16)

    def ffn_up(i, rows):
        gate = _dot(st[i]["h2"], wgu_ref[:, 0:FFN_HIDDEN])
        up = _dot(st[i]["h2"], wgu_ref[:, FFN_HIDDEN:2 * FFN_HIDDEN])
        st[i]["act"] = (_silu(gate) * up).astype(BF16)

    def ffn_down(i, rows):
        y = st[i]["x1"] + _dot(st[i]["act"], wd_ref[...])
        out_ref[rows, :] = _rms(y, gf_ref[...])

    for stage in (merge, out_proj, ffn_up, ffn_down):
        for i, rows in enumerate(halves):
            stage(i, rows)


def _out_call(x2d, o_a, o_b, mg, wa, wb, wo, g2, wgu, wd, gf, tm=512):
    t = x2d.shape[0]
    row = lambda i: (i, 0)
    const = lambda i: (0, 0)

    def wspec(w):
        return pl.BlockSpec(w.shape, const, pipeline_mode=pl.Buffered(1))

    return pl.pallas_call(
        _out_kernel,
        grid=(t // tm,),
        in_specs=[pl.BlockSpec((tm, D_MODEL), row),
                  pl.BlockSpec((tm, GDN_WIDTH), row),
                  pl.BlockSpec((tm, NSA_WIDTH), row),
                  pl.BlockSpec((tm, 2 * D_MODEL), row),
                  wspec(wa), wspec(wb), wspec(wo),
                  pl.BlockSpec((1, D_MODEL), const),
                  wspec(wgu), wspec(wd),
                  pl.BlockSpec((1, D_MODEL), const)],
        out_specs=pl.BlockSpec((tm, D_MODEL), row),
        out_shape=jax.ShapeDtypeStruct((t, D_MODEL), F32),
        compiler_params=pltpu.CompilerParams(dimension_semantics=("parallel",),
                                             vmem_limit_bytes=VMEM_LIMIT),
        name="merge_ffn",
    )(x2d, o_a, o_b, mg, wa, wb, wo, g2, wgu, wd, gf)


def _q_head_perm():
    idx = []
    for r in range(NSA_GROUP_HEADS):
        for g in range(NSA_KV_GROUPS):
            base = (g * NSA_GROUP_HEADS + r) * NSA_HEAD_DIM
            idx.extend(range(base, base + NSA_HEAD_DIM))
    return np.asarray(idx, dtype=np.int32)


def _reorder_w_in(w_in):
    sizes = (3 * GDN_WIDTH, GDN_WIDTH, GDN_HEADS, GDN_HEADS, NSA_WIDTH) + (NSA_KV_WIDTH,) * 6 + (
        3 * NSA_HEADS, D_MODEL, D_MODEL)
    offs = np.concatenate([[0], np.cumsum(sizes)])
    seg = lambda i: w_in[:, offs[i]:offs[i + 1]]
    qkv, z, beta, alpha, q, kc, vc, ks, vs, kw, vw, gate, ma, mb = [seg(i) for i in range(14)]
    pad = jnp.zeros((D_MODEL, LANES - 2 * GDN_HEADS - 3 * NSA_HEADS), w_in.dtype)
    q = q[:, _q_head_perm()]
    zero_half = jnp.zeros((D_MODEL, NSA_HEAD_DIM), w_in.dtype)
    ks_split = [ks[:, :NSA_HEAD_DIM], zero_half, ks[:, NSA_HEAD_DIM:], zero_half]
    return jnp.concatenate([qkv, z, q, kc, vc] + ks_split + [vs, kw, vw, ma, mb, beta, alpha, gate, pad],
                           axis=1).astype(BF16)


def _rope_tables(seq):
    half = ROPE_DIM // 2
    inv_freq = ROPE_THETA ** (-jnp.arange(half, dtype=F32) / half)
    ang = jnp.arange(seq, dtype=F32)[:, None] * inv_freq
    cos, sin = jnp.cos(ang), jnp.sin(ang)
    ones = jnp.ones((seq, NSA_HEAD_DIM - ROPE_DIM), F32)
    zeros = jnp.zeros((seq, NSA_HEAD_DIM - ROPE_DIM), F32)
    z8 = jnp.zeros((seq, half), F32)
    c = jnp.concatenate([cos, cos, ones], axis=1)
    s1 = jnp.concatenate([z8, sin, zeros], axis=1)
    s2 = jnp.concatenate([-sin, z8, zeros], axis=1)
    tile2 = lambda a: jnp.concatenate([a, a], axis=1)
    return tile2(c), tile2(s1), tile2(s2)


def _cmp_weights(pos, w1, w2):
    g = NSA_KV_GROUPS
    seg = CMP_BLOCK // 2
    eye = jnp.eye(g, dtype=w1.dtype)
    pos_h = pos.reshape(2, seg, 1, NSA_HEAD_DIM)
    pos_flat = jnp.broadcast_to(pos_h, (2, seg, g, NSA_HEAD_DIM)).reshape(2, seg * g * NSA_HEAD_DIM)
    w1_h = w1.reshape(2, seg, NSA_HEAD_DIM, CMP_HIDDEN)
    w1_bd = jnp.einsum('alds,gk->algdks', w1_h, eye).reshape(2, seg * g * NSA_HEAD_DIM, g * CMP_HIDDEN)
    w2_bd = jnp.einsum('hd,gk->ghkd', w2, eye).reshape(g * CMP_HIDDEN, g * NSA_HEAD_DIM)
    return pos_flat.astype(F32), w1_bd.astype(BF16), w2_bd.astype(BF16)


def _selection_constants(seq):
    n_cmp_pad = seq // CMP_STRIDE
    n_blk = seq // SEL_BLOCK
    cmp_start = np.arange(n_cmp_pad) * CMP_STRIDE
    cmp_end = cmp_start + CMP_BLOCK - 1
    sel_start = np.arange(n_blk) * SEL_BLOCK
    ovt = ((cmp_start[None, :] <= sel_start[:, None] + SEL_BLOCK - 1) & (cmp_end[None, :] >= sel_start[:, None]))
    ovt[:, n_cmp_pad - 1] = False
    return jnp.asarray(ovt, BF16)


def _lane_vector(values, offset):
    v = jnp.zeros((1, LANES), F32)
    return v.at[0, offset:offset + values.shape[0]].set(values.astype(F32))


def _hybrid_block(x, mix_norm_gain, w_in, gdn_conv_w, gdn_a_log, gdn_dt_bias, gdn_out_norm_gain,
                  cmp_pos_k, cmp_w1_k, cmp_w2_k, cmp_pos_v, cmp_w1_v, cmp_w2_v,
                  w_branch_gdn, w_branch_nsa, w_out, ffn_norm_gain, w_gate_up, w_down, final_norm_gain):
    batch, seq, _ = x.shape
    x2d = x.reshape(batch * seq, D_MODEL)
    rc, rs1, rs2 = _rope_tables(seq)
    qkv, z, kc, vc, kvs, mg, small, q_t, gates_t = _inproj_call(
        x2d, mix_norm_gain.reshape(1, D_MODEL), _reorder_w_in(w_in), rc, rs1, rs2,
        gdn_conv_w.astype(F32), seq)

    o_a = _gdn_call(qkv, z, small,
                    _lane_vector(gdn_a_log, SMALL_ALPHA), _lane_vector(gdn_dt_bias, SMALL_ALPHA),
                    gdn_out_norm_gain.reshape(1, GDN_HEAD_DIM).astype(F32), batch, seq)

    seg_width = (CMP_BLOCK // 2) * NSA_KV_WIDTH
    kc3 = kc.reshape(batch, seq // CMP_STRIDE, seg_width)
    vc3 = vc.reshape(batch, seq // CMP_STRIDE, seg_width)
    kcc, vcc = _cmp_call(kc3, vc3, *_cmp_weights(cmp_pos_k, cmp_w1_k, cmp_w2_k),
                         *_cmp_weights(cmp_pos_v, cmp_w1_v, cmp_w2_v))
    o_b = _nsa_call(q_t, kcc, vcc, kvs, gates_t, _selection_constants(seq), batch, seq)

    out = _out_call(x2d, o_a, o_b, mg,
                    w_branch_gdn.astype(BF16), w_branch_nsa[_q_head_perm(), :].astype(BF16), w_out.astype(BF16),
                    ffn_norm_gain.reshape(1, D_MODEL), w_gate_up.astype(BF16), w_down.astype(BF16),
                    final_norm_gain.reshape(1, D_MODEL))
    return out.reshape(batch, seq, D_MODEL)


def kernel(x, mix_norm_gain, w_in, gdn_conv_w, gdn_a_log, gdn_dt_bias, gdn_out_norm_gain, cmp_pos_k, cmp_w1_k,
           cmp_w2_k, cmp_pos_v, cmp_w1_v, cmp_w2_v, w_branch_gdn, w_branch_nsa, w_out, ffn_norm_gain, w_gate_up,
           w_down, final_norm_gain):
    assert mix_norm_gain.shape[0] == 1, "single-layer block"
    return _hybrid_block(x, mix_norm_gain[0], w_in[0], gdn_conv_w[0], gdn_a_log[0], gdn_dt_bias[0],
                         gdn_out_norm_gain[0], cmp_pos_k[0], cmp_w1_k[0], cmp_w2_k[0], cmp_pos_v[0], cmp_w1_v[0],
                         cmp_w2_v[0], w_branch_gdn[0], w_branch_nsa[0], w_out[0], ffn_norm_gain[0], w_gate_up[0],
                         w_down[0], final_norm_gain)
```

```python
import functools

import numpy as np
import jax
import jax.numpy as jnp
from jax import lax
from jax.experimental import pallas as pl
from jax.experimental.pallas import tpu as pltpu

F32 = jnp.float32
BF16 = jnp.bfloat16

D_MODEL = 1024
NORM_EPS = 1e-6
GDN_HEADS = 4
GDN_HEAD_DIM = 128
GDN_WIDTH = GDN_HEADS * GDN_HEAD_DIM
GDN_CONV = 4
GDN_CHUNK = 64
INV_BASE = 8
NSA_HEADS = 8
NSA_KV_GROUPS = 2
NSA_GROUP_HEADS = NSA_HEADS // NSA_KV_GROUPS
NSA_HEAD_DIM = 64
NSA_WIDTH = NSA_HEADS * NSA_HEAD_DIM
NSA_KV_WIDTH = NSA_KV_GROUPS * NSA_HEAD_DIM
CMP_BLOCK = 32
CMP_STRIDE = 16
CMP_HIDDEN = 128
SEL_BLOCK = 64
SEL_COUNT = 16
WINDOW = 512
ROPE_THETA = 500000.0
ROPE_DIM = NSA_HEAD_DIM // 4
FORCED_SCORE = 1000.0
NEG_INF = -1e30
FFN_HIDDEN = 2816

LANES = 128
VMEM_LIMIT = 56 * 1024 * 1024

C_QKV = (0, 1536)
C_Z = (1536, 2048)
C_Q = (2048, 2560)
C_KC = (2560, 2688)
C_VC = (2688, 2816)
C_KVS = (2816, 3456)
C_MG = (3456, 5504)
C_SMALL = (5504, 5632)
IN_WIDTH_PADDED = 5632
SMALL_BETA = 0
SMALL_ALPHA = 4
SMALL_GATE = 8

KV_KS = 0
KV_VS = 256
KV_KW = 512
KV_VW = 640
KV_WIDTH = 896
LOG2E = 1.4426950408889634
Q_SCALE = NSA_HEAD_DIM ** -0.5 * LOG2E

PART_ROWS = 128
TQ = 128
NSA_TILES_PER_STEP = 2
KB = 512


def _dot(a, b):
    return jnp.dot(a, b, preferred_element_type=F32)


def _dot_nt(a, b):
    return lax.dot_general(a, b, (((1,), (1,)), ((), ())), preferred_element_type=F32)


def _dot_tn(a, b):
    return lax.dot_general(a, b, (((0,), (0,)), ((), ())), preferred_element_type=F32)


def _sigmoid(x):
    return 1.0 / (1.0 + jnp.exp(-x))


def _silu(x):
    return x * _sigmoid(x)


def _rms(x, gain):
    return x * lax.rsqrt(jnp.mean(x * x, axis=-1, keepdims=True) + NORM_EPS) * gain


def _rope_slab(y, c, s1, s2):
    return y * c + pltpu.roll(y, 8, 1) * s1 + pltpu.roll(y, LANES - 8, 1) * s2


def _inproj_kernel(x_ref, gain_ref, w_ref, rc_ref, rs1_ref, rs2_ref, convw_ref,
                   qkv_ref, z_ref, kc_ref, vc_ref, kvs_ref, mg_ref, small_ref, q_ref, gates_ref,
                   xbuf, segbuf, *, seq):
    tm = x_ref.shape[0]
    hb = _rms(x_ref[...], gain_ref[...]).astype(BF16)

    def proj(cols):
        return _dot(hb, w_ref[:, cols[0]:cols[1]])

    first_tile = pl.program_id(0) % (seq // tm) == 0

    @pl.when(first_tile)
    def _():
        xbuf[0:8, :] = jnp.zeros((8, 3 * GDN_WIDTH), F32)

    @pl.when(jnp.logical_not(first_tile))
    def _():
        xbuf[0:8, :] = xbuf[tm:tm + 8, :]

    cw = convw_ref[...]
    slab_w = 2 * GDN_HEAD_DIM

    def gdn_slab(k):
        lo = k * slab_w
        cols = slice(lo, lo + slab_w)
        xbuf[8:8 + tm, cols] = proj((C_QKV[0] + lo, C_QKV[0] + lo + slab_w))
        conv = xbuf[8:8 + tm, cols] * cw[GDN_CONV - 1:GDN_CONV, cols]
        for jj in range(GDN_CONV - 1):
            off = 8 - (GDN_CONV - 1) + jj
            conv = conv + xbuf[off:off + tm, cols] * cw[jj:jj + 1, cols]
        act = _silu(conv)
        for h in range(lo // GDN_HEAD_DIM, (lo + slab_w) // GDN_HEAD_DIM):
            a = act[:, h * GDN_HEAD_DIM - lo:(h + 1) * GDN_HEAD_DIM - lo]
            if h < 2 * GDN_HEADS:
                a = a * lax.rsqrt(jnp.sum(a * a, axis=-1, keepdims=True) + NORM_EPS)
            if h < GDN_HEADS:
                a = a * (GDN_HEAD_DIM ** -0.5)
            qkv_ref[:, h * GDN_HEAD_DIM:(h + 1) * GDN_HEAD_DIM] = a.astype(BF16)

    c, s1, s2 = rc_ref[...], rs1_ref[...], rs2_ref[...]

    def task_z():
        z_ref[...] = proj(C_Z).astype(BF16)

    def task_merge(half):
        lo = C_MG[0] + half * D_MODEL
        mg_ref[:, half * D_MODEL:(half + 1) * D_MODEL] = proj((lo, lo + D_MODEL)).astype(BF16)

    def task_small():
        small = proj(C_SMALL)
        small_ref[...] = small
        gates = _sigmoid(small)
        for ch in range(tm // LANES):
            tok = slice(ch * LANES, (ch + 1) * LANES)
            gates_ref[:, tok] = gates[tok, :].T
        segbuf[0] = proj(C_VC)
        segbuf[1] = _rope_slab(proj(C_KC), c, s1, s2)
        n_seg = tm // CMP_STRIDE
        for which, out_ref in ((0, vc_ref), (1, kc_ref)):
            for l in range(CMP_STRIDE):
                rows = segbuf[which, pl.ds(l, n_seg, stride=CMP_STRIDE), :]
                out_ref[:, l * NSA_KV_WIDTH:(l + 1) * NSA_KV_WIDTH] = rows.astype(BF16)

    def task_q():
        q = proj(C_Q)
        for r in range(NSA_GROUP_HEADS):
            sl = slice(r * LANES, (r + 1) * LANES)
            roped = _rope_slab(q[:, sl], c, s1, s2) * Q_SCALE
            for ch in range(tm // LANES):
                tok = slice(ch * LANES, (ch + 1) * LANES)
                q_ref[sl, tok] = roped[tok, :].T.astype(BF16)

    def task_kv():
        tok = (pl.program_id(0) % (seq // tm)) * tm + lax.broadcasted_iota(jnp.int32, (tm, 1), 0)
        lane = lax.broadcasted_iota(jnp.int32, (tm, LANES), 1)
        group0 = lane < NSA_HEAD_DIM
        block_onehot = jnp.where(lane == NSA_HEAD_DIM + tok // SEL_BLOCK, 1.0, 0.0)
        kvs = proj(C_KVS)
        for g in range(NSA_KV_GROUPS):
            ks = _rope_slab(kvs[:, g * LANES:(g + 1) * LANES], c, s1, s2)
            kvs_ref[:, KV_KS + g * LANES:KV_KS + (g + 1) * LANES] = jnp.where(group0, ks, block_onehot).astype(BF16)
        vs, kw, vw = kvs[:, 256:384], kvs[:, 384:512], kvs[:, 512:640]
        kvs_ref[:, KV_VS:KV_VS + LANES] = jnp.where(group0, vs, 1.0).astype(BF16)
        kvs_ref[:, KV_VS + LANES:KV_VS + 2 * LANES] = jnp.where(group0, 1.0, vs).astype(BF16)
        kvs_ref[:, KV_KW:KV_KW + LANES] = _rope_slab(kw, c, s1, s2).astype(BF16)
        kvs_ref[:, KV_VW:KV_VW + LANES] = jnp.where(group0, vw, 1.0).astype(BF16)
        kvs_ref[:, KV_VW + LANES:KV_VW + 2 * LANES] = jnp.where(group0, 1.0, vw).astype(BF16)

    others = [task_z, functools.partial(task_merge, 0), task_q, functools.partial(task_merge, 1), task_kv,
              task_small]
    n_slabs = 3 * GDN_WIDTH // slab_w
    assert len(others) == n_slabs
    for k in range(n_slabs):
        gdn_slab(k)
        others[k]()


def _inproj_call(x2d, gain, w_r, rc, rs1, rs2, conv_w, seq, tm=512):
    assert seq // SEL_BLOCK <= LANES, "selection-block one-hot must fit one lane tile"
    t = x2d.shape[0]
    n_seq_tiles = seq // tm
    row = lambda i: (i, 0)
    const = lambda i: (0, 0)
    tab = lambda i: (i % n_seq_tiles, 0)
    seg = CMP_STRIDE * NSA_KV_WIDTH
    outs = [(tm, t, 1536, BF16), (tm, t, 512, BF16),
            (tm // CMP_STRIDE, t // CMP_STRIDE, seg, BF16), (tm // CMP_STRIDE, t // CMP_STRIDE, seg, BF16),
            (tm, t, KV_WIDTH, BF16), (tm, t, 2048, BF16), (tm, t, 128, F32)]
    outs_t = [(NSA_WIDTH, BF16), (LANES, F32)]
    col = lambda i: (0, i)
    return pl.pallas_call(
        functools.partial(_inproj_kernel, seq=seq),
        grid=(t // tm,),
        in_specs=[pl.BlockSpec((tm, D_MODEL), row),
                  pl.BlockSpec((1, D_MODEL), const),
                  pl.BlockSpec((D_MODEL, IN_WIDTH_PADDED), const, pipeline_mode=pl.Buffered(1)),
                  pl.BlockSpec((tm, LANES), tab),
                  pl.BlockSpec((tm, LANES), tab),
                  pl.BlockSpec((tm, LANES), tab),
                  pl.BlockSpec((GDN_CONV, 3 * GDN_WIDTH), const)],
        out_specs=([pl.BlockSpec((r, w), row) for r, _, w, _ in outs]
                   + [pl.BlockSpec((h, tm), col) for h, _ in outs_t]),
        out_shape=([jax.ShapeDtypeStruct((n, w), d) for _, n, w, d in outs]
                   + [jax.ShapeDtypeStruct((h, t), d) for h, d in outs_t]),
        scratch_shapes=[pltpu.VMEM((tm + 8, 3 * GDN_WIDTH), F32),
                        pltpu.VMEM((2, tm, NSA_KV_WIDTH), F32)],
        compiler_params=pltpu.CompilerParams(dimension_semantics=("arbitrary",),
                                             vmem_limit_bytes=VMEM_LIMIT),
        name="inproj",
    )(x2d, gain, w_r, rc, rs1, rs2, conv_w)


def _gdn_kernel(qkv_ref, z_ref, small_ref, alog_ref, dtb_ref, ogain_ref, o_ref,
                state, u_ref, w_ref, qk_ref, qg_ref, kd_ref, gl_ref, *, blocks_per_seq):
    t = pl.program_id(0)
    bufs = (u_ref, w_ref, qk_ref, qg_ref, kd_ref, gl_ref)

    @pl.when(t == 0)
    def _():
        state[...] = jnp.zeros_like(state)
        for ref in bufs:
            ref[...] = jnp.zeros_like(ref)

    fresh = (t + blocks_per_seq - 1) % blocks_per_seq == 0
    for slot in range(2):
        @pl.when(t % 2 == slot)
        def _(slot=slot):
            _gdn_step(qkv_ref, z_ref, small_ref, alog_ref, dtb_ref, ogain_ref, o_ref, state, bufs, slot, fresh)


def _gdn_step(qkv_ref, z_ref, small_ref, alog_ref, dtb_ref, ogain_ref, o_ref, state, bufs, cur, fresh):
    u_ref, w_ref, qk_ref, qg_ref, kd_ref, gl_ref = bufs
    prev = 1 - cur
    cb = qkv_ref.shape[0]
    c = GDN_CHUNK
    n_chunks = cb // c
    heads = range(GDN_HEADS)

    row = lax.broadcasted_iota(jnp.int32, (c, c), 0)
    col = lax.broadcasted_iota(jnp.int32, (c, c), 1)
    causal = row >= col
    strict = row > col
    tril = causal.astype(F32)
    eye = (row == col).astype(F32)
    neg_decay_rate = -jnp.exp(alog_ref[...])
    dtb = dtb_ref[...]
    ogain = ogain_ref[...]

    pairs = [(ci, h) for ci in range(n_chunks) for h in heads]
    n_pairs = len(pairs)
    A = {}

    def a_prepare():
        beta_c, gc_c, gct_c = [], [], []
        for ci in range(n_chunks):
            sm = small_ref[ci * c:(ci + 1) * c, :]
            beta_c.append(_sigmoid(sm))
            xg = sm + dtb
            softplus = jnp.maximum(xg, 0.0) + jnp.log(1.0 + jnp.exp(-jnp.abs(xg)))
            gc_all = jnp.dot(tril, neg_decay_rate * softplus, precision=lax.Precision.HIGHEST,
                             preferred_element_type=F32)
            gc_c.append(gc_all)
            gct_c.append(gc_all.T)

        def head_cols(base, ci, h):
            lo = base + h * GDN_HEAD_DIM
            return qkv_ref[ci * c:(ci + 1) * c, lo:lo + GDN_HEAD_DIM].astype(F32)

        A["qb"], A["kbf"], A["vb"], A["kbeg"], A["decay"], a_l = [], [], [], [], [], []
        for i, (ci, h) in enumerate(pairs):
            q = head_cols(0, ci, h)
            k = head_cols(GDN_WIDTH, ci, h)
            v = head_cols(2 * GDN_WIDTH, ci, h)
            beta = beta_c[ci][:, SMALL_BETA + h:SMALL_BETA + h + 1]
            gc = gc_c[ci][:, SMALL_ALPHA + h:SMALL_ALPHA + h + 1]
            gr = gct_c[ci][SMALL_ALPHA + h:SMALL_ALPHA + h + 1, :]
            decay = jnp.where(causal, jnp.exp(jnp.where(causal, gc - gr, 0.0)), 0.0)
            eg = jnp.exp(gc)
            g_last = gc[c - 1:c, :]
            kb = k * beta
            kbf = k.astype(BF16)
            qg_ref[cur, i] = (q * eg).astype(BF16)
            kd_ref[cur, i] = (k * jnp.exp(g_last - gc)).astype(BF16)
            gl_ref[cur, i] = jnp.broadcast_to(jnp.exp(g_last), gl_ref.shape[2:])
            A["qb"].append(q.astype(BF16))
            A["kbf"].append(kbf)
            A["vb"].append((v * beta).astype(BF16))
            A["kbeg"].append((kb * eg).astype(BF16))
            A["decay"].append(decay)
            a_l.append(jnp.where(strict, _dot_nt(kb.astype(BF16), kbf) * decay, 0.0))
        A["a"] = a_l
        diag_blocks = [jnp.where(row // INV_BASE == col // INV_BASE, a, 0.0) for a in a_l]
        A["t"] = [eye - d for d in diag_blocks]
        A["p"] = diag_blocks

    def a_qk():
        for i in range(n_pairs):
            qk_ref[cur, i] = (_dot_nt(A["qb"][i], A["kbf"][i]) * A["decay"][i]).astype(BF16)

    def a_square():
        A["p"] = [_dot(p.astype(BF16), p.astype(BF16)) for p in A["p"]]

    def a_extend():
        A["t"] = [t_ + _dot(t_.astype(BF16), p.astype(BF16)) for t_, p in zip(A["t"], A["p"])]

    def a_u():
        A["tb"] = [t_.astype(BF16) for t_ in A["t"]]
        for i in range(n_pairs):
            u_ref[cur, i] = _dot(A["tb"][i], A["vb"][i])

    def a_w():
        for i in range(n_pairs):
            w_ref[cur, i] = _dot(A["tb"][i], A["kbeg"][i]).astype(BF16)

    def a_link(size):
        link = (row // (2 * size) == col // (2 * size)) & ((row // size) % 2 == 1) & ((col // size) % 2 == 0)
        A["dn"] = [_dot(t_.astype(BF16), jnp.where(link, a, 0.0).astype(BF16)) for t_, a in zip(A["t"], A["a"])]

    def a_merge():
        A["t"] = [t_ - _dot(dn.astype(BF16), t_.astype(BF16)) for t_, dn in zip(A["t"], A["dn"])]

    a_stages = [a_prepare, a_qk] + [a_square, a_extend] * (INV_BASE.bit_length() - 2)
    size = INV_BASE
    while size < c:
        a_stages += [functools.partial(a_link, size), a_merge]
        size *= 2
    a_stages += [a_u, a_w]

    B = {"s": [jnp.where(fresh, 0.0, state[h]) for h in heads]}

    def b_new_values(ci):
        idx = [ci * GDN_HEADS + h for h in heads]
        B["sb"] = [s.astype(BF16) for s in B["s"]]
        B["vn"] = [(u_ref[prev, i] - _dot(w_ref[prev, i], sb)).astype(BF16) for i, sb in zip(idx, B["sb"])]

    def b_output(ci):
        idx = [ci * GDN_HEADS + h for h in heads]
        o_l = [_dot(qg_ref[prev, i], sb) + _dot(qk_ref[prev, i], vn) for i, sb, vn in zip(idx, B["sb"], B["vn"])]
        B["s"] = [s * gl_ref[prev, i][0:1, :] + _dot_tn(kd_ref[prev, i], vn)
                  for i, s, vn in zip(idx, B["s"], B["vn"])]
        for h, o in zip(heads, o_l):
            hs = slice(h * GDN_HEAD_DIM, (h + 1) * GDN_HEAD_DIM)
            zh = z_ref[ci * c:(ci + 1) * c, hs].astype(F32)
            o = o * lax.rsqrt(jnp.mean(o * o, axis=-1, keepdims=True) + NORM_EPS) * ogain * _silu(zh)
            o_ref[ci * c:(ci + 1) * c, hs] = o.astype(BF16)

    b_stages = []
    for ci in range(n_chunks):
        b_stages += [functools.partial(b_new_values, ci), functools.partial(b_output, ci)]

    emitted_b = 0
    for k, stage in enumerate(a_stages):
        while emitted_b < len(b_stages) and emitted_b * len(a_stages) <= k * len(b_stages):
            b_stages[emitted_b]()
            emitted_b += 1
        stage()
    for stage in b_stages[emitted_b:]:
        stage()
    for h in heads:
        state[h] = B["s"][h]


def _gdn_call(qkv, z, small, alog_v, dtb_v, ogain, batch, seq, cb=256):
    blocks_per_seq = seq // cb
    n_blocks = batch * blocks_per_seq
    n_pairs = (cb // GDN_CHUNK) * GDN_HEADS
    c, d = GDN_CHUNK, GDN_HEAD_DIM
    phase_a_block = lambda t: (jnp.minimum(t, n_blocks - 1), 0)
    phase_b_block = lambda t: (jnp.maximum(t - 1, 0), 0)
    const = lambda t: (0, 0)
    return pl.pallas_call(
        functools.partial(_gdn_kernel, blocks_per_seq=blocks_per_seq),
        grid=(n_blocks + 1,),
        in_specs=[pl.BlockSpec((cb, 3 * GDN_WIDTH), phase_a_block),
                  pl.BlockSpec((cb, GDN_WIDTH), phase_b_block),
                  pl.BlockSpec((cb, LANES), phase_a_block),
                  pl.BlockSpec((1, LANES), const),
                  pl.BlockSpec((1, LANES), const),
                  pl.BlockSpec((1, GDN_HEAD_DIM), const)],
        out_specs=pl.BlockSpec((cb, GDN_WIDTH), phase_b_block),
        out_shape=jax.ShapeDtypeStruct((batch * seq, GDN_WIDTH), BF16),
        scratch_shapes=[pltpu.VMEM((GDN_HEADS, d, d), F32),
                        pltpu.VMEM((2, n_pairs, c, d), F32),
                        pltpu.VMEM((2, n_pairs, c, d), BF16),
                        pltpu.VMEM((2, n_pairs, c, c), BF16),
                        pltpu.VMEM((2, n_pairs, c, d), BF16),
                        pltpu.VMEM((2, n_pairs, c, d), BF16),
                        pltpu.VMEM((2, n_pairs, 8, d), F32)],
        compiler_params=pltpu.CompilerParams(dimension_semantics=("arbitrary",),
                                             vmem_limit_bytes=VMEM_LIMIT),
        name="gdn",
    )(qkv, z, small, alog_v, dtb_v, ogain)


def _cmp_kernel(kc_ref, vc_ref, posk_ref, w1k_ref, w2k_ref, posv_ref, w1v_ref, w2v_ref, kco_ref, vco_ref):
    for x_ref, pos_ref, w1_ref, w2_ref, out_ref in ((kc_ref, posk_ref, w1k_ref, w2k_ref, kco_ref),
                                                    (vc_ref, posv_ref, w1v_ref, w2v_ref, vco_ref)):
        a = x_ref[0].astype(F32)
        n = a.shape[0]
        p0 = _dot((a + pos_ref[0:1, :]).astype(BF16), w1_ref[0])
        p1 = _dot((a + pos_ref[1:2, :]).astype(BF16), w1_ref[1])
        hid = _silu(p0 + pltpu.roll(p1, n - 1, 0))
        out_ref[0] = _dot(hid.astype(BF16), w2_ref[...]).astype(BF16)


def _cmp_call(kc3, vc3, posk, w1k, w2k, posv, w1v, w2v):
    b, n, width = kc3.shape
    hid2 = NSA_KV_GROUPS * CMP_HIDDEN
    per_b = lambda i: (i, 0, 0)
    c2 = lambda i: (0, 0)
    c3 = lambda i: (0, 0, 0)
    wspecs = [pl.BlockSpec((2, width), c2), pl.BlockSpec((2, width, hid2), c3), pl.BlockSpec((hid2, NSA_KV_WIDTH), c2)]
    return pl.pallas_call(
        _cmp_kernel,
        grid=(b,),
        in_specs=[pl.BlockSpec((1, n, width), per_b), pl.BlockSpec((1, n, width), per_b)] + wspecs + wspecs,
        out_specs=[pl.BlockSpec((1, n, NSA_KV_WIDTH), per_b)] * 2,
        out_shape=[jax.ShapeDtypeStruct((b, n, NSA_KV_WIDTH), BF16)] * 2,
        compiler_params=pltpu.CompilerParams(dimension_semantics=("parallel",),
                                             vmem_limit_bytes=VMEM_LIMIT),
        name="nsa_compress",
    )(kc3, vc3, posk, w1k, w2k, posv, w1v, w2v)


def _nsa_kernel(q_ref, kc_ref, vc_ref, kvs_ref, gates_ref, ovt_ref, o_ref, *scratch):
    for sub in range(NSA_TILES_PER_STEP):
        tokens = pl.ds(sub * TQ, TQ)
        _nsa_tile(pl.program_id(1) * NSA_TILES_PER_STEP + sub, q_ref.at[:, tokens], kc_ref, vc_ref, kvs_ref,
                  gates_ref.at[:, tokens], ovt_ref, o_ref.at[tokens, :], *scratch)


def _nsa_tile(qi, q_ref, kc_ref, vc_ref, kvs_ref, gates_ref, ovt_ref, o_ref,
              qa_ref, m_ref, acc_ref, s_ref, part_ref):
    tq = q_ref.shape[1]
    rh = NSA_GROUP_HEADS
    n_cmp = kc_ref.shape[1]
    n_blk = ovt_ref.shape[0]
    s0 = qi * tq

    row_half = lax.broadcasted_iota(jnp.int32, (LANES, tq), 0) // NSA_HEAD_DIM
    t_row = s0 + lax.broadcasted_iota(jnp.int32, (1, tq), 1)
    k_col = lax.broadcasted_iota(jnp.int32, (KB, 1), 0)
    q_t = [q_ref[r * LANES:(r + 1) * LANES, :].astype(F32) for r in range(rh)]
    gates_t = gates_ref[...]
    kc = kc_ref[0]
    vc = vc_ref[0]
    cmp_end = lax.broadcasted_iota(jnp.int32, (n_cmp, 1), 0) * CMP_STRIDE + (CMP_BLOCK - 1)
    vis1 = cmp_end <= t_row
    vis = jnp.concatenate([vis1] * rh, axis=1)

    blk = lax.broadcasted_iota(jnp.int32, (n_blk, tq), 0)
    cur = (s0 + lax.broadcasted_iota(jnp.int32, (n_blk, tq), 1)) // SEL_BLOCK
    valid = blk <= cur
    forced = (blk == 0) | (blk == cur) | (blk == cur - 1)

    def tile4(a):
        return jnp.concatenate([a] * rh, axis=1)

    def normalized(acc, g):
        l_row = acc[(1 - g) * NSA_HEAD_DIM:(1 - g) * NSA_HEAD_DIM + 1, :]
        return acc * (1.0 / l_row)

    groups = range(NSA_KV_GROUPS)
    def gate_row(g, branch):
        cols = [SMALL_GATE + (g * rh + r) * 3 + branch for r in range(rh)]
        return jnp.concatenate([gates_t[c:c + 1, :] for c in cols], axis=1)

    qs_l = [jnp.concatenate([jnp.where(row_half == g, q_t[r], 0.0) for r in range(rh)], axis=1).astype(BF16)
            for g in groups]
    span = WINDOW + TQ
    w0 = pl.multiple_of(jnp.maximum(s0 - WINDOW, 0), TQ)
    kw = kvs_ref[pl.ds(w0, span), KV_KW:KV_KW + LANES]
    s_cmp = [_dot(kc, qs_l[g]) for g in groups]
    s_win = [_dot(kw, qs_l[g]) for g in groups]

    def cmp_branch(g):
        s = jnp.where(vis, s_cmp[g], NEG_INF)
        m = jnp.max(s, axis=0, keepdims=True)
        e = jnp.where(vis, jnp.exp2(s - m), 0.0)
        den = jnp.sum(e, axis=0, keepdims=True)
        p = e * (1.0 / jnp.where(den > 0.0, den, 1.0))
        o_cmp = _dot_tn(vc, p.astype(BF16))
        p_sum = p[:, 0:tq]
        for r in range(1, rh):
            p_sum = p_sum + p[:, r * tq:(r + 1) * tq]
        p_hi = p_sum.astype(BF16)
        p_lo = (p_sum - p_hi.astype(F32)).astype(BF16)
        ovt = ovt_ref[...]
        return o_cmp, _dot(ovt, p_hi) + _dot(ovt, p_lo)

    rel = t_row - (w0 + lax.broadcasted_iota(jnp.int32, (span, 1), 0))
    win_bias = tile4(jnp.where((rel >= 0) & (rel < WINDOW), 0.0, NEG_INF))

    def window_branch(g, o_cmp):
        vw = kvs_ref[pl.ds(w0, span), KV_VW + g * LANES:KV_VW + (g + 1) * LANES]
        sw = s_win[g] + win_bias
        pw = jnp.exp2(sw - jnp.max(sw, axis=0, keepdims=True))
        o_win = normalized(_dot_tn(vw, pw.astype(BF16)), g)
        part_ref[g] = gate_row(g, 0) * o_cmp + gate_row(g, 2) * o_win

    sub = lax.broadcasted_iota(jnp.int32, (8, tq), 0)
    n_slab = n_blk // 8

    def select_blocks(g, imp):
        score = jnp.where(valid, jnp.where(forced, FORCED_SCORE, imp), -1.0)
        slabs = [score[8 * v:8 * (v + 1), :] for v in range(n_slab)]
        ranks = [jnp.zeros((8, tq), F32) for _ in range(n_slab)]
        for i in range(n_blk):
            vi, ri = divmod(i, 8)
            si = jnp.broadcast_to(score[i:i + 1, :], (8, tq))
            for v in range(n_slab):
                if v > vi:
                    hit = jnp.where(si >= slabs[v], 1.0, 0.0)
                elif v < vi:
                    hit = jnp.where(si > slabs[v], 1.0, 0.0)
                else:
                    hit = jnp.where(sub > ri, jnp.where(si >= slabs[v], 1.0, 0.0), jnp.where(si > slabs[v], 1.0, 0.0))
                ranks[v] = ranks[v] + hit
        rank = jnp.concatenate(ranks, axis=0)
        sel_bias = jnp.where((rank < float(SEL_COUNT)) & valid, 0.0, NEG_INF)
        if n_blk < NSA_HEAD_DIM:
            sel_bias = jnp.concatenate([sel_bias, jnp.zeros((NSA_HEAD_DIM - n_blk, tq), F32)], axis=0)
        q_g = qs_l[g][g * NSA_HEAD_DIM:(g + 1) * NSA_HEAD_DIM, :]
        qa_ref[g] = jnp.concatenate([q_g, tile4(sel_bias).astype(BF16)], axis=0)

    o_cmp0, imp0 = cmp_branch(0)
    o_cmp1, imp1 = cmp_branch(1)
    select_blocks(0, imp0)
    window_branch(0, o_cmp0)
    select_blocks(1, imp1)
    window_branch(1, o_cmp1)

    kb_diag = s0 // KB

    def scores(kb):
        k0 = pl.multiple_of(kb * KB, KB)
        return [_dot(kvs_ref[pl.ds(k0, KB), KV_KS + g * LANES:KV_KS + (g + 1) * LANES], qa_ref[g])
                for g in groups]

    def accumulate(kb, s_l, first):
        k0 = pl.multiple_of(kb * KB, KB)
        for g in groups:
            va = kvs_ref[pl.ds(k0, KB), KV_VS + g * LANES:KV_VS + (g + 1) * LANES]
            if first:
                m_new = jnp.max(s_l[g], axis=0, keepdims=True)
                acc_ref[g] = _dot_tn(va, jnp.exp2(s_l[g] - m_new).astype(BF16))
            else:
                m_old = m_ref[g]
                m_new = jnp.maximum(m_old, jnp.max(s_l[g], axis=0, keepdims=True))
                acc_ref[g] = (jnp.exp2(m_old - m_new) * acc_ref[g]
                              + _dot_tn(va, jnp.exp2(s_l[g] - m_new).astype(BF16)))
            m_ref[g] = m_new

    causal_bias = tile4(jnp.where(kb_diag * KB + k_col <= t_row, 0.0, NEG_INF))
    s_diag = [s + causal_bias for s in scores(kb_diag)]
    s_next = scores(jnp.maximum(kb_diag - 1, 0))
    accumulate(kb_diag, s_diag, True)
    for g in groups:
        s_ref[g] = s_next[g]

    def sel_body(i, carry):
        kb = kb_diag - i
        s_cur = [s_ref[g] for g in groups]
        s_nxt = scores(jnp.maximum(kb - 1, 0))
        accumulate(kb, s_cur, False)
        for g in groups:
            s_ref[g] = s_nxt[g]
        return carry

    lax.fori_loop(1, kb_diag + 1, sel_body, 0)

    out_t = [part_ref[g] + gate_row(g, 1) * normalized(acc_ref[g], g) for g in groups]
    for r in range(rh):
        cs = slice(r * tq, (r + 1) * tq)
        slab_t = jnp.where(row_half == 0, out_t[0][:, cs], out_t[1][:, cs])
        o_ref[:, r * LANES:(r + 1) * LANES] = slab_t.T.astype(BF16)


def _nsa_call(q_t, kc, vc, kvs, gates_t, ovt, batch, seq):
    assert seq % KB == 0 and seq >= WINDOW + TQ and seq // SEL_BLOCK <= NSA_HEAD_DIM
    step_tokens = NSA_TILES_PER_STEP * TQ
    assert seq % step_tokens == 0
    nq = seq // step_tokens
    n_cmp = kc.shape[1]
    row = lambda b, i: (b * nq + i, 0)
    col = lambda b, i: (0, b * nq + i)
    per_b3 = lambda b, i: (b, 0, 0)
    per_b2 = lambda b, i: (b, 0)
    c2 = lambda b, i: (0, 0)
    rows = NSA_GROUP_HEADS * TQ
    return pl.pallas_call(
        _nsa_kernel,
        grid=(batch, nq),
        in_specs=[pl.BlockSpec((NSA_WIDTH, step_tokens), col),
                  pl.BlockSpec((1, n_cmp, NSA_KV_WIDTH), per_b3),
                  pl.BlockSpec((1, n_cmp, NSA_KV_WIDTH), per_b3),
                  pl.BlockSpec((seq, KV_WIDTH), per_b2),
                  pl.BlockSpec((LANES, step_tokens), col),
                  pl.BlockSpec(ovt.shape, c2)],
        out_specs=pl.BlockSpec((step_tokens, NSA_WIDTH), row),
        out_shape=jax.ShapeDtypeStruct((batch * seq, NSA_WIDTH), BF16),
        scratch_shapes=[pltpu.VMEM((NSA_KV_GROUPS, LANES, rows), BF16),
                        pltpu.VMEM((NSA_KV_GROUPS, 1, rows), F32),
                        pltpu.VMEM((NSA_KV_GROUPS, LANES, rows), F32),
                        pltpu.VMEM((NSA_KV_GROUPS, KB, rows), F32),
                        pltpu.VMEM((NSA_KV_GROUPS, LANES, rows), F32)],
        compiler_params=pltpu.CompilerParams(dimension_semantics=("parallel", "arbitrary"),
                                             vmem_limit_bytes=VMEM_LIMIT),
        name="nsa_attention",
    )(q_t, kc, vc, kvs, gates_t, ovt)


def _out_kernel(x_ref, oa_ref, ob_ref, mg_ref, wa_ref, wb_ref, wo_ref, g2_ref, wgu_ref, wd_ref, gf_ref, out_ref):
    tm = x_ref.shape[0]
    halves = [slice(i * PART_ROWS, (i + 1) * PART_ROWS) for i in range(tm // PART_ROWS)]
    st = [{} for _ in halves]

    def merge(i, rows):
        a = _dot(oa_ref[rows, :], wa_ref[...])
        b = _dot(ob_ref[rows, :], wb_ref[...])
        st[i]["merged"] = (_sigmoid(mg_ref[rows, 0:D_MODEL].astype(F32)) * a
                           + _sigmoid(mg_ref[rows, D_MODEL:2 * D_MODEL].astype(F32)) * b).astype(BF16)

    def out_proj(i, rows):
        st[i]["x1"] = x_ref[rows, :] + _dot(st[i]["merged"], wo_ref[...])
        st[i]["h2"] = _rms(st[i]["x1"], g2_ref[...]).astype(BF16)

    def ffn_up(i, rows):
        gate = _dot(st[i]["h2"], wgu_ref[:, 0:FFN_HIDDEN])
        up = _dot(st[i]["h2"], wgu_ref[:, FFN_HIDDEN:2 * FFN_HIDDEN])
        st[i]["act"] = (_silu(gate) * up).astype(BF16)

    def ffn_down(i, rows):
        y = st[i]["x1"] + _dot(st[i]["act"], wd_ref[...])
        out_ref[rows, :] = _rms(y, gf_ref[...])

    for stage in (merge, out_proj, ffn_up, ffn_down):
        for i, rows in enumerate(halves):
            stage(i, rows)


def _out_call(x2d, o_a, o_b, mg, wa, wb, wo, g2, wgu, wd, gf, tm=512):
    t = x2d.shape[0]
    row = lambda i: (i, 0)
    const = lambda i: (0, 0)

    def wspec(w):
        return pl.BlockSpec(w.shape, const, pipeline_mode=pl.Buffered(1))

    return pl.pallas_call(
        _out_kernel,
        grid=(t // tm,),
        in_specs=[pl.BlockSpec((tm, D_MODEL), row),
                  pl.BlockSpec((tm, GDN_WIDTH), row),
                  pl.BlockSpec((tm, NSA_WIDTH), row),
                  pl.BlockSpec((tm, 2 * D_MODEL), row),
                  wspec(wa), wspec(wb), wspec(wo),
                  pl.BlockSpec((1, D_MODEL), const),
                  wspec(wgu), wspec(wd),
                  pl.BlockSpec((1, D_MODEL), const)],
        out_specs=pl.BlockSpec((tm, D_MODEL), row),
        out_shape=jax.ShapeDtypeStruct((t, D_MODEL), F32),
        compiler_params=pltpu.CompilerParams(dimension_semantics=("parallel",),
                                             vmem_limit_bytes=VMEM_LIMIT),
        name="merge_ffn",
    )(x2d, o_a, o_b, mg, wa, wb, wo, g2, wgu, wd, gf)


def _q_head_perm():
    idx = []
    for r in range(NSA_GROUP_HEADS):
        for g in range(NSA_KV_GROUPS):
            base = (g * NSA_GROUP_HEADS + r) * NSA_HEAD_DIM
            idx.extend(range(base, base + NSA_HEAD_DIM))
    return np.asarray(idx, dtype=np.int32)


def _reorder_w_in(w_in):
    sizes = (3 * GDN_WIDTH, GDN_WIDTH, GDN_HEADS, GDN_HEADS, NSA_WIDTH) + (NSA_KV_WIDTH,) * 6 + (
        3 * NSA_HEADS, D_MODEL, D_MODEL)
    offs = np.concatenate([[0], np.cumsum(sizes)])
    seg = lambda i: w_in[:, offs[i]:offs[i + 1]]
    qkv, z, beta, alpha, q, kc, vc, ks, vs, kw, vw, gate, ma, mb = [seg(i) for i in range(14)]
    pad = jnp.zeros((D_MODEL, LANES - 2 * GDN_HEADS - 3 * NSA_HEADS), w_in.dtype)
    q = q[:, _q_head_perm()]
    zero_half = jnp.zeros((D_MODEL, NSA_HEAD_DIM), w_in.dtype)
    ks_split = [ks[:, :NSA_HEAD_DIM], zero_half, ks[:, NSA_HEAD_DIM:], zero_half]
    return jnp.concatenate([qkv, z, q, kc, vc] + ks_split + [vs, kw, vw, ma, mb, beta, alpha, gate, pad],
                           axis=1).astype(BF16)


def _rope_tables(seq):
    half = ROPE_DIM // 2
    inv_freq = ROPE_THETA ** (-jnp.arange(half, dtype=F32) / half)
    ang = jnp.arange(seq, dtype=F32)[:, None] * inv_freq
    cos, sin = jnp.cos(ang), jnp.sin(ang)
    ones = jnp.ones((seq, NSA_HEAD_DIM - ROPE_DIM), F32)
    zeros = jnp.zeros((seq, NSA_HEAD_DIM - ROPE_DIM), F32)
    z8 = jnp.zeros((seq, half), F32)
    c = jnp.concatenate([cos, cos, ones], axis=1)
    s1 = jnp.concatenate([z8, sin, zeros], axis=1)
    s2 = jnp.concatenate([-sin, z8, zeros], axis=1)
    tile2 = lambda a: jnp.concatenate([a, a], axis=1)
    return tile2(c), tile2(s1), tile2(s2)


def _cmp_weights(pos, w1, w2):
    g = NSA_KV_GROUPS
    seg = CMP_BLOCK // 2
    eye = jnp.eye(g, dtype=w1.dtype)
    pos_h = pos.reshape(2, seg, 1, NSA_HEAD_DIM)
    pos_flat = jnp.broadcast_to(pos_h, (2, seg, g, NSA_HEAD_DIM)).reshape(2, seg * g * NSA_HEAD_DIM)
    w1_h = w1.reshape(2, seg, NSA_HEAD_DIM, CMP_HIDDEN)
    w1_bd = jnp.einsum('alds,gk->algdks', w1_h, eye).reshape(2, seg * g * NSA_HEAD_DIM, g * CMP_HIDDEN)
    w2_bd = jnp.einsum('hd,gk->ghkd', w2, eye).reshape(g * CMP_HIDDEN, g * NSA_HEAD_DIM)
    return pos_flat.astype(F32), w1_bd.astype(BF16), w2_bd.astype(BF16)


def _selection_constants(seq):
    n_cmp_pad = seq // CMP_STRIDE
    n_blk = seq // SEL_BLOCK
    cmp_start = np.arange(n_cmp_pad) * CMP_STRIDE
    cmp_end = cmp_start + CMP_BLOCK - 1
    sel_start = np.arange(n_blk) * SEL_BLOCK
    ovt = ((cmp_start[None, :] <= sel_start[:, None] + SEL_BLOCK - 1) & (cmp_end[None, :] >= sel_start[:, None]))
    ovt[:, n_cmp_pad - 1] = False
    return jnp.asarray(ovt, BF16)


def _lane_vector(values, offset):
    v = jnp.zeros((1, LANES), F32)
    return v.at[0, offset:offset + values.shape[0]].set(values.astype(F32))


def _hybrid_block(x, mix_norm_gain, w_in, gdn_conv_w, gdn_a_log, gdn_dt_bias, gdn_out_norm_gain,
                  cmp_pos_k, cmp_w1_k, cmp_w2_k, cmp_pos_v, cmp_w1_v, cmp_w2_v,
                  w_branch_gdn, w_branch_nsa, w_out, ffn_norm_gain, w_gate_up, w_down, final_norm_gain):
    batch, seq, _ = x.shape
    x2d = x.reshape(batch * seq, D_MODEL)
    rc, rs1, rs2 = _rope_tables(seq)
    qkv, z, kc, vc, kvs, mg, small, q_t, gates_t = _inproj_call(
        x2d, mix_norm_gain.reshape(1, D_MODEL), _reorder_w_in(w_in), rc, rs1, rs2,
        gdn_conv_w.astype(F32), seq)

    o_a = _gdn_call(qkv, z, small,
                    _lane_vector(gdn_a_log, SMALL_ALPHA), _lane_vector(gdn_dt_bias, SMALL_ALPHA),
                    gdn_out_norm_gain.reshape(1, GDN_HEAD_DIM).astype(F32), batch, seq)

    seg_width = (CMP_BLOCK // 2) * NSA_KV_WIDTH
    kc3 = kc.reshape(batch, seq // CMP_STRIDE, seg_width)
    vc3 = vc.reshape(batch, seq // CMP_STRIDE, seg_width)
    kcc, vcc = _cmp_call(kc3, vc3, *_cmp_weights(cmp_pos_k, cmp_w1_k, cmp_w2_k),
                         *_cmp_weights(cmp_pos_v, cmp_w1_v, cmp_w2_v))
    o_b = _nsa_call(q_t, kcc, vcc, kvs, gates_t, _selection_constants(seq), batch, seq)

    out = _out_call(x2d, o_a, o_b, mg,
                    w_branch_gdn.astype(BF16), w_branch_nsa[_q_head_perm(), :].astype(BF16), w_out.astype(BF16),
                    ffn_norm_gain.reshape(1, D_MODEL), w_gate_up.astype(BF16), w_down.astype(BF16),
                    final_norm_gain.reshape(1, D_MODEL))
    return out.reshape(batch, seq, D_MODEL)


def kernel(x, mix_norm_gain, w_in, gdn_conv_w, gdn_a_log, gdn_dt_bias, gdn_out_norm_gain, cmp_pos_k, cmp_w1_k,
           cmp_w2_k, cmp_pos_v, cmp_w1_v, cmp_w2_v, w_branch_gdn, w_branch_nsa, w_out, ffn_norm_gain, w_gate_up,
           w_down, final_norm_gain):
    assert mix_norm_gain.shape[0] == 1, "single-layer block"
    return _hybrid_block(x, mix_norm_gain[0], w_in[0], gdn_conv_w[0], gdn_a_log[0], gdn_dt_bias[0],
                         gdn_out_norm_gain[0], cmp_pos_k[0], cmp_w1_k[0], cmp_w2_k[0], cmp_pos_v[0], cmp_w1_v[0],
                         cmp_w2_v[0], w_branch_gdn[0], w_branch_nsa[0], w_out[0], ffn_norm_gain[0], w_gate_up[0],
                         w_down[0], final_norm_gain)
```
